```python
import math
import jax
import jax.numpy as jnp
from jax import lax

D_MODEL = 1024
BATCH = 2
SEQ = 8192
DEPTH = 2
DEC_BATCH = 128
DEC_SEQ = 4
PAST_LEN = 2048
PAGE_SIZE = 128

W_MIX = D_MODEL
W_A = W_MIX // 4
H_A = 4
DH_A = W_A // H_A
SB_QBLOCK = 128
SB_BIAS_INIT = -6.0
W_B = W_MIX // 4
G_B = 4
DG_B = W_B // G_B
GMLP_CHUNK = 128
W_C = W_MIX // 4
G_C = 4
DG_C = W_C // G_C
POOL_WINDOWS = (2, 4, 8, 16)
POOL_BUF = max(POOL_WINDOWS) - 1
W_D = W_MIX - W_A - W_B - W_C
H_D = 4
DK_D = W_D // H_D
DV_D = W_D // H_D
CONV_K = 4
DELTA_CHUNK = 64
N_MEM = 256
XA_HEADS = 4
XA_DH = D_MODEL // XA_HEADS
N_EXPERTS = 16
N_GROUPS = 4
EXPERTS_PER_GROUP = N_EXPERTS // N_GROUPS
TOP_K = 2
D_EXPERT = 256
ALPHA = (2 * DEPTH) ** 0.25
INIT_BETA = (8 * DEPTH) ** -0.25
LN_EPS = 1e-5
IN_SPLITS = (W_A, W_A, W_A, W_B, W_B, W_C, 3 * W_D, W_D, H_D, H_D)
N_IN = sum(IN_SPLITS)

kernel_name = 'hybrid_sb_gmlp_pool_gdn_moe_step'


def layer_norm(x, g, b):
    xf = x.astype(jnp.float32)
    mu = jnp.mean(xf, -1, keepdims=True)
    var = jnp.mean(jnp.square(xf - mu), -1, keepdims=True)
    y = (xf - mu) * lax.rsqrt(var + LN_EPS) * g.astype(jnp.float32) + b.astype(jnp.float32)
    return y.astype(x.dtype)


def rms_norm(x, g):
    xf = x.astype(jnp.float32)
    y = xf * lax.rsqrt(jnp.mean(xf * xf, -1, keepdims=True) + 1e-6) * g.astype(jnp.float32)
    return y.astype(x.dtype)


def l2_normalize(x):
    xf = x.astype(jnp.float32)
    return xf * lax.rsqrt(jnp.sum(xf * xf, -1, keepdims=True) + 1e-6)


def gather_pages(pool, page_table):
    rows = pool[page_table]
    b, n, p = rows.shape[:3]
    return rows.reshape((b, n * p) + rows.shape[3:])


def stick_breaking(q, k, v, q_pos, k_pos, bias):
    z = jnp.einsum('bqhd,bkhd->bhqk', q, k).astype(jnp.float32) * (q.shape[-1] ** -0.5)
    z = z + bias.astype(jnp.float32)[None, :, None, None]
    mask = k_pos[None, :] < q_pos[:, None]
    log_beta = jax.nn.log_sigmoid(z)
    log_keep = jnp.where(mask, jax.nn.log_sigmoid(-z), 0.0)
    suffix = lax.cumsum(log_keep, axis=3, reverse=True)
    after = jnp.concatenate([suffix[..., 1:], jnp.zeros_like(suffix[..., :1])], axis=-1)
    att = jnp.where(mask, jnp.exp(log_beta + after), 0.0)
    return jnp.einsum('bhqk,bkhd->bqhd', att.astype(v.dtype), v)


def stick_breaking_prompt(q, k, v, bias):
    b, t, h, dh = q.shape
    nb = t // SB_QBLOCK
    qb = jnp.moveaxis(q.reshape(b, nb, SB_QBLOCK, h, dh), 1, 0)
    k_pos = jnp.arange(t)

    def block(args):
        q_blk, i = args
        return stick_breaking(q_blk, k, v, i * SB_QBLOCK + jnp.arange(SB_QBLOCK), k_pos, bias)

    out = lax.map(block, (qb, jnp.arange(nb)))
    return jnp.moveaxis(out, 0, 1).reshape(b, t, h, dh)


def gmlp_gate(u, v, ws, bs, ln_g, ln_b):
    b, t, _ = u.shape
    c = min(t, GMLP_CHUNK)
    nc = t // c
    u = jax.nn.gelu(u)
    vn = layer_norm(jax.nn.gelu(v).reshape(b, t, G_B, DG_B), ln_g, ln_b)
    w = jnp.where(jnp.tril(jnp.ones((c, c), bool)), ws[:, :c, :c], 0.0).astype(vn.dtype)
    mixed = jnp.einsum('gts,bnsgd->bntgd', w, vn.reshape(b, nc, c, G_B, DG_B))
    mixed = mixed + jnp.swapaxes(bs[:, :c], 0, 1)[:, :, None].astype(vn.dtype)
    return u * mixed.reshape(b, t, W_B), vn.reshape(b, t, W_B)


def pool_mix(xc, prev, start, w_pool, gamma):
    b, t, _ = xc.shape
    xx = jnp.concatenate([prev.astype(xc.dtype), xc], axis=1)
    xf = xx.astype(jnp.float32)
    cs = jnp.concatenate([jnp.zeros_like(xf[:, :1]), jnp.cumsum(xf, axis=1)], axis=1)
    pos = start + jnp.arange(t)
    means = []
    for gi, w in enumerate(POOL_WINDOWS):
        ch = slice(gi * DG_C, (gi + 1) * DG_C)
        win = cs[:, POOL_BUF + 1:POOL_BUF + 1 + t, ch] - cs[:, POOL_BUF + 1 - w:POOL_BUF + 1 - w + t, ch]
        cnt = jnp.minimum(pos + 1, w).astype(jnp.float32)
        means.append(win / cnt[None, :, None])
    d = (jnp.concatenate(means, -1) - xc.astype(jnp.float32)).reshape(b, t, G_C, DG_C)
    y = jnp.einsum('btgc,gcd->btgd', d, w_pool.astype(jnp.float32)).reshape(b, t, W_C) * gamma.astype(jnp.float32)
    return y.astype(xc.dtype), xx[:, t:]


def short_conv(x, prev, w):
    t = x.shape[1]
    xx = jnp.concatenate([prev.astype(x.dtype), x], axis=1)
    y = xx[:, 0:t] * w[0]
    for i in range(1, CONV_K):
        y = y + xx[:, i:i + t] * w[i]
    return jax.nn.silu(y), xx[:, t:]


def gated_delta_rule(q, k, v, g, beta, s0):
    b, t, h, dk = q.shape
    dv = v.shape[-1]
    c = DELTA_CHUNK
    pad = (-t) % c
    n = (t + pad) // c

    def chunks(a):
        a = a.astype(jnp.float32)
        a = jnp.pad(a, [(0, 0), (0, pad)] + [(0, 0)] * (a.ndim - 2))
        a = a.reshape((b, n, c) + a.shape[2:])
        return jnp.moveaxis(a, 3, 1)

    qc = chunks(q) * (dk ** -0.5)
    kc = chunks(k)
    vc = chunks(v)
    gc = jnp.cumsum(chunks(g), axis=-1)
    bc = chunks(beta)
    incl = jnp.tril(jnp.ones((c, c), bool))
    strict = jnp.tril(jnp.ones((c, c), bool), -1)
    diff = gc[..., :, None] - gc[..., None, :]
    decay = jnp.where(incl, jnp.exp(jnp.where(incl, diff, 0.0)), 0.0)
    kb = kc * bc[..., None]
    vb = vc * bc[..., None]
    lower = jnp.where(strict, jnp.einsum('bhnid,bhnjd->bhnij', kb, kc) * decay, 0.0)
    eye = jnp.eye(c, dtype=jnp.float32)
    t_inv = lax.linalg.triangular_solve(eye + lower, jnp.broadcast_to(eye, lower.shape),
                                        left_side=True, lower=True, unit_diagonal=True)
    u = t_inv @ vb
    w = t_inv @ (kb * jnp.exp(gc)[..., None])
    intra = jnp.where(incl, jnp.einsum('bhnid,bhnjd->bhnij', qc, kc) * decay, 0.0)
    qg = qc * jnp.exp(gc)[..., None]
    kg = kc * jnp.exp(gc[..., -1:] - gc)[..., None]
    g_last = jnp.exp(gc[..., -1])
    xs = tuple(jnp.moveaxis(a, 2, 0) for a in (u, w, intra, qg, kg, g_last))

    def step(s, xs_i):
        u_i, w_i, a_i, qg_i, kg_i, gl_i = xs_i
        v_new = u_i - w_i @ s
        o_i = qg_i @ s + a_i @ v_new
        s = s * gl_i[..., None, None] + jnp.swapaxes(kg_i, -1, -2) @ v_new
        return s, o_i

    s_final, o = lax.scan(step, s0.astype(jnp.float32), xs)
    o = jnp.moveaxis(o, 0, 2).reshape(b, h, n * c, dv)
    return jnp.moveaxis(o, 1, 2)[:, :t], s_final


def mixer_sublayer(h, P, l, sb_past, pool_prev, conv_prev, s0, start):
    b, t, _ = h.shape
    z = h @ P['w_in'][l]
    idx = []
    acc = 0
    for s in IN_SPLITS[:-1]:
        acc += s
        idx.append(acc)
    qa, ka, va, ub, vb, xc, qkv, zg, bd, ad = jnp.split(z, idx, axis=-1)
    qa = qa.reshape(b, t, H_A, DH_A)
    ka = ka.reshape(b, t, H_A, DH_A)
    va = va.reshape(b, t, H_A, DH_A)
    if sb_past is None:
        oa = stick_breaking_prompt(qa, ka, va, P['sb_bias'][l])
    else:
        past_k, past_v = sb_past
        p_len = past_k.shape[1]
        k_all = jnp.concatenate([past_k.astype(ka.dtype), ka], axis=1)
        v_all = jnp.concatenate([past_v.astype(va.dtype), va], axis=1)
        oa = stick_breaking(qa, k_all, v_all, p_len + jnp.arange(t), jnp.arange(p_len + t), P['sb_bias'][l])
    ob, v_rows = gmlp_gate(ub, vb, P['gmlp_ws'][l], P['gmlp_bs'][l], P['gmlp_ln_g'][l], P['gmlp_ln_b'][l])
    oc, pool_new = pool_mix(xc, pool_prev, start, P['pool_w'][l], P['pool_gamma'][l])
    qkv, conv_new = short_conv(qkv, conv_prev, P['conv_w'][l])
    qd, kd, vd = jnp.split(qkv, 3, axis=-1)
    qd = l2_normalize(qd.reshape(b, t, H_D, DK_D))
    kd = l2_normalize(kd.reshape(b, t, H_D, DK_D))
    vd = vd.reshape(b, t, H_D, DV_D)
    beta = jax.nn.sigmoid(bd.astype(jnp.float32))
    g = -jnp.exp(P['dn_a_log'][l].astype(jnp.float32)) * jax.nn.softplus(
        ad.astype(jnp.float32) + P['dn_dt_bias'][l].astype(jnp.float32))
    od, s_new = gated_delta_rule(qd, kd, vd, g, beta, s0)
    od = rms_norm(od, P['dn_norm_g'][l]) * jax.nn.silu(zg.reshape(b, t, H_D, DV_D).astype(jnp.float32))
    od = od.reshape(b, t, W_D).astype(h.dtype)
    o = jnp.concatenate([oa.reshape(b, t, W_A), ob, oc, od], axis=-1) @ P['w_out'][l]
    return o, (ka, va, v_rows, pool_new, conv_new, s_new)


def cross_attention(h, mem_k, mem_v, wq, wo):
    b, t, _ = h.shape
    q = (h @ wq).reshape(b, t, XA_HEADS, XA_DH)
    s = jnp.einsum('bthd,bmhd->bhtm', q, mem_k.astype(q.dtype)).astype(jnp.float32) * (XA_DH ** -0.5)
    p = jax.nn.softmax(s, axis=-1).astype(h.dtype)
    o = jnp.einsum('bhtm,bmhd->bthd', p, mem_v.astype(h.dtype)).reshape(b, t, D_MODEL)
    return o @ wo


def moe_ffn(h, w_router, router_bias, wg, wu, wd):
    b, t, d = h.shape
    x = h.reshape(b * t, d)
    s = jax.nn.sigmoid((x @ w_router).astype(jnp.float32))
    sel = (s + router_bias.astype(jnp.float32)).reshape(-1, N_GROUPS, EXPERTS_PER_GROUP)
    group_score = jnp.sum(lax.top_k(sel, TOP_K)[0], axis=-1)
    gidx = jnp.argmax(group_score, axis=-1)
    in_group = jnp.take_along_axis(sel, gidx[:, None, None], axis=1)[:, 0]
    _, local = lax.top_k(in_group, TOP_K)
    eidx = gidx[:, None] * EXPERTS_PER_GROUP + local
    gate = jnp.take_along_axis(s, eidx, axis=-1)
    gate = gate / jnp.sum(gate, -1, keepdims=True)
    combine = jnp.sum(jax.nn.one_hot(eidx, N_EXPERTS, dtype=jnp.float32) * gate[..., None], axis=1)
    hg = jnp.einsum('nd,edf->nef', x, wg)
    hu = jnp.einsum('nd,edf->nef', x, wu)
    act = jax.nn.silu(hg) * hu * combine[:, :, None].astype(x.dtype)
    y = jnp.einsum('nef,efd->nd', act, wd)
    return y.reshape(b, t, d)


def trunk(x, P, sb_past, pool_prev, conv_prev, s0, start, mem_k, mem_v):
    h = layer_norm(x, P['ln0_g'], P['ln0_b'])
    states = []
    for l in range(DEPTH):
        past = None if sb_past is None else sb_past[l]
        m, st = mixer_sublayer(h, P, l, past, pool_prev[l], conv_prev[l], s0[l], start)
        h = layer_norm(ALPHA * h + m, P['ln1_g'][l], P['ln1_b'][l])
        c = cross_attention(h, mem_k[l], mem_v[l], P['xa_wq'][l], P['xa_wo'][l])
        h = layer_norm(ALPHA * h + c, P['ln2_g'][l], P['ln2_b'][l])
        f = moe_ffn(h, P['w_router'], P['router_bias'], P['ex_wg'][l], P['ex_wu'][l], P['ex_wd'][l])
        h = layer_norm(ALPHA * h + f, P['ln3_g'][l], P['ln3_b'][l])
        states.append(st)
    return h, states


def setup_inputs(seed: int = 0) -> dict:
    key = jax.random.key(seed)
    ks = jax.random.split(key, 64)
    cnt = [0]

    def nk():
        cnt[0] += 1
        return ks[cnt[0] - 1]

    def nrm(shape, scale=1.0):
        return jax.random.normal(nk(), shape, jnp.float32) * scale

    def gain(shape):
        return 1.0 + nrm(shape, 0.02)

    n_pages = PAST_LEN // PAGE_SIZE
    n_used = DEC_BATCH * n_pages
    n_phys = n_used + n_used // 4
    page_table = jax.random.permutation(nk(), n_phys)[:n_used].reshape(DEC_BATCH, n_pages).astype(jnp.int32)
    dt = jnp.exp(jax.random.uniform(nk(), (DEPTH, H_D), jnp.float32, math.log(1e-3), math.log(1e-1)))
    dn_dt_bias = dt + jnp.log(-jnp.expm1(-dt))
    dn_a_log = jnp.log(jax.random.uniform(nk(), (DEPTH, H_D), jnp.float32, 1.0, 16.0))
    return {
        'x_prompt': nrm((BATCH, SEQ, D_MODEL)),
        'x_sample': nrm((DEC_BATCH, DEC_SEQ, D_MODEL)),
        'cache_sb_k': nrm((DEPTH, n_phys, PAGE_SIZE, H_A, DH_A)),
        'cache_sb_v': nrm((DEPTH, n_phys, PAGE_SIZE, H_A, DH_A)),
        'state_pool': nrm((DEPTH, DEC_BATCH, POOL_BUF, W_C)),
        'state_conv': nrm((DEPTH, DEC_BATCH, CONV_K - 1, 3 * W_D)),
        'state_delta': nrm((DEPTH, DEC_BATCH, H_D, DK_D, DV_D), 0.1),
        'cache_mem_k': nrm((DEPTH, DEC_BATCH, N_MEM, XA_HEADS, XA_DH)),
        'cache_mem_v': nrm((DEPTH, DEC_BATCH, N_MEM, XA_HEADS, XA_DH)),
        'page_table': page_table,
        'mem_prompt': nrm((BATCH, N_MEM, D_MODEL)),
        'ln0_g': gain((D_MODEL,)),
        'ln0_b': nrm((D_MODEL,), 0.02),
        'w_in': nrm((DEPTH, D_MODEL, N_IN), D_MODEL ** -0.5),
        'sb_bias': SB_BIAS_INIT + nrm((DEPTH, H_A), 0.1),
        'gmlp_ln_g': gain((DEPTH, G_B, DG_B)),
        'gmlp_ln_b': nrm((DEPTH, G_B, DG_B), 0.02),
        'gmlp_ws': nrm((DEPTH, G_B, GMLP_CHUNK, GMLP_CHUNK), GMLP_CHUNK ** -0.5),
        'gmlp_bs': 1.0 + nrm((DEPTH, G_B, GMLP_CHUNK), 0.1),
        'pool_w': nrm((DEPTH, G_C, DG_C, DG_C), DG_C ** -0.5),
        'pool_gamma': 1.0 + nrm((DEPTH, W_C), 0.1),
        'conv_w': nrm((DEPTH, CONV_K, 3 * W_D), CONV_K ** -0.5),
        'dn_a_log': dn_a_log,
        'dn_dt_bias': dn_dt_bias,
        'dn_norm_g': gain((DEPTH, DV_D)),
        'w_out': nrm((DEPTH, W_MIX, D_MODEL), INIT_BETA * W_MIX ** -0.5),
        'ln1_g': gain((DEPTH, D_MODEL)),
        'ln1_b': nrm((DEPTH, D_MODEL), 0.02),
        'xa_wq': nrm((DEPTH, D_MODEL, D_MODEL), D_MODEL ** -0.5),
        'xa_wk': nrm((DEPTH, D_MODEL, D_MODEL), D_MODEL ** -0.5),
        'xa_wv': nrm((DEPTH, D_MODEL, D_MODEL), D_MODEL ** -0.5),
        'xa_wo': nrm((DEPTH, D_MODEL, D_MODEL), INIT_BETA * D_MODEL ** -0.5),
        'ln2_g': gain((DEPTH, D_MODEL)),
        'ln2_b': nrm((DEPTH, D_MODEL), 0.02),
        'w_router': nrm((D_MODEL, N_EXPERTS), D_MODEL ** -0.5),
        'router_bias': nrm((N_EXPERTS,), 0.01),
        'ex_wg': nrm((DEPTH, N_EXPERTS, D_MODEL, D_EXPERT), D_MODEL ** -0.5),
        'ex_wu': nrm((DEPTH, N_EXPERTS, D_MODEL, D_EXPERT), D_MODEL ** -0.5),
        'ex_wd': nrm((DEPTH, N_EXPERTS, D_EXPERT, D_MODEL), INIT_BETA * D_EXPERT ** -0.5),
        'ln3_g': gain((DEPTH, D_MODEL)),
        'ln3_b': nrm((DEPTH, D_MODEL), 0.02),
    }


def reference(x_prompt, x_sample, cache_sb_k, cache_sb_v, state_pool, state_conv, state_delta,
              cache_mem_k, cache_mem_v, page_table, mem_prompt,
              ln0_g, ln0_b, w_in, sb_bias, gmlp_ln_g, gmlp_ln_b, gmlp_ws, gmlp_bs, pool_w, pool_gamma,
              conv_w, dn_a_log, dn_dt_bias, dn_norm_g, w_out, ln1_g, ln1_b,
              xa_wq, xa_wk, xa_wv, xa_wo, ln2_g, ln2_b, w_router, router_bias,
              ex_wg, ex_wu, ex_wd, ln3_g, ln3_b):
    P = {'ln0_g': ln0_g, 'ln0_b': ln0_b, 'w_in': w_in, 'sb_bias': sb_bias,
         'gmlp_ln_g': gmlp_ln_g, 'gmlp_ln_b': gmlp_ln_b,
         'gmlp_ws': gmlp_ws, 'gmlp_bs': gmlp_bs, 'pool_w': pool_w, 'pool_gamma': pool_gamma,
         'conv_w': conv_w, 'dn_a_log': dn_a_log, 'dn_dt_bias': dn_dt_bias, 'dn_norm_g': dn_norm_g,
         'w_out': w_out, 'ln1_g': ln1_g, 'ln1_b': ln1_b, 'xa_wq': xa_wq, 'xa_wo': xa_wo,
         'ln2_g': ln2_g, 'ln2_b': ln2_b, 'w_router': w_router, 'router_bias': router_bias,
         'ex_wg': ex_wg, 'ex_wu': ex_wu, 'ex_wd': ex_wd, 'ln3_g': ln3_g, 'ln3_b': ln3_b}
    bp = x_prompt.shape[0]
    mem_k_p = jnp.stack([(mem_prompt @ xa_wk[l]).reshape(bp, -1, XA_HEADS, XA_DH) for l in range(DEPTH)])
    mem_v_p = jnp.stack([(mem_prompt @ xa_wv[l]).reshape(bp, -1, XA_HEADS, XA_DH) for l in range(DEPTH)])
    pool0 = [jnp.zeros((bp, POOL_BUF, W_C), x_prompt.dtype)] * DEPTH
    conv0 = [jnp.zeros((bp, CONV_K - 1, 3 * W_D), x_prompt.dtype)] * DEPTH
    s00 = [jnp.zeros((bp, H_D, DK_D, DV_D), jnp.float32)] * DEPTH
    y_prompt, st_p = trunk(x_prompt, P, None, pool0, conv0, s00, 0, mem_k_p, mem_v_p)
    past_len = page_table.shape[1] * PAGE_SIZE
    sb_past = [(gather_pages(cache_sb_k[l], page_table), gather_pages(cache_sb_v[l], page_table))
               for l in range(DEPTH)]
    y_sample, st_s = trunk(x_sample, P, sb_past, state_pool, state_conv, state_delta, past_len,
                           cache_mem_k, cache_mem_v)

    def stk(sts, i):
        return jnp.stack([s[i] for s in sts])

    return (y_prompt, y_sample,
            stk(st_p, 0), stk(st_p, 1), stk(st_s, 0), stk(st_s, 1),
            stk(st_s, 2),
            stk(st_p, 3), stk(st_s, 3),
            stk(st_p, 4), stk(st_s, 4),
            stk(st_p, 5), stk(st_s, 5),
            mem_k_p, mem_v_p)
```

```python
import functools

import jax
import jax.numpy as jnp
from jax import lax
from jax.experimental import pallas as pl
from jax.experimental.pallas import tpu as pltpu

F32 = jnp.float32
BF16 = jnp.bfloat16

DEPTH = 2
H_A = 4
DH_A = 64
G_B = 4
GMLP_CHUNK = 128
POOL_WINDOWS = (2, 4, 8, 16)
POOL_BUF = 15
H_D = 4
DK_D = 64
CONV_K = 4
DELTA_CHUNK = 64
XA_HEADS = 4
N_EXPERTS = 16
N_GROUPS = 4
EXPERTS_PER_GROUP = 4
ALPHA = (2 * DEPTH) ** 0.25
LN_EPS = 1e-5
GROUP_LANES = 64
W_Q = 256

SB_BLOCK = 128
VMEM_LIMIT = 48 * 1024 * 1024


def _cparams(*sem):
    return pltpu.CompilerParams(dimension_semantics=sem, vmem_limit_bytes=VMEM_LIMIT)


def _bdot(a, b):
    return jnp.dot(a.astype(BF16), b.astype(BF16), preferred_element_type=F32)


def _bdot_nt(a, b):
    return lax.dot_general(a.astype(BF16), b.astype(BF16), (((1,), (1,)), ((), ())),
                           preferred_element_type=F32)


def _bdot_tn(a, b):
    return lax.dot_general(a.astype(BF16), b.astype(BF16), (((0,), (0,)), ((), ())),
                           preferred_element_type=F32)


def _split(x):
    hi = x.astype(BF16)
    lo = (x - hi.astype(F32)).astype(BF16)
    return hi, lo


def _dot_exact_rhs(x, c):
    hi, lo = _split(x)
    return (jnp.dot(hi, c, preferred_element_type=F32)
            + jnp.dot(lo, c, preferred_element_type=F32))


def _dot_exact_lhs(c, x):
    hi, lo = _split(x)
    return (jnp.dot(c, hi, preferred_element_type=F32)
            + jnp.dot(c, lo, preferred_element_type=F32))


def _dot3(a, b):
    ah, al = _split(a)
    bh, bl = _split(b)
    return (jnp.dot(ah, bh, preferred_element_type=F32)
            + jnp.dot(ah, bl, preferred_element_type=F32)
            + jnp.dot(al, bh, preferred_element_type=F32))


def _iota(shape, dim):
    return lax.broadcasted_iota(jnp.int32, shape, dim)


def _group_ones(n):
    return jnp.where((_iota((n, n), 0) >> 6) == (_iota((n, n), 1) >> 6), 1.0, 0.0).astype(BF16)


def _group_sum(x, ones):
    return _dot_exact_rhs(x, ones)


def _layer_norm(x, g, b):
    mu = jnp.mean(x, axis=-1, keepdims=True)
    xc = x - mu
    var = jnp.mean(xc * xc, axis=-1, keepdims=True)
    return xc * lax.rsqrt(var + LN_EPS) * g + b


def _sigmoid(x):
    return 1.0 / (1.0 + jnp.exp(-x))


def _silu(x):
    return x * _sigmoid(x)


def _softplus(x):
    return jnp.maximum(x, 0.0) + jnp.log(1.0 + jnp.exp(-jnp.abs(x)))


def _gelu_tanh(x):
    return 0.5 * x * (1.0 + jnp.tanh(0.7978845608028654 * (x + 0.044715 * (x * x * x))))


def _group_layer_norm(x, ones, g, b):
    mu = _group_sum(x, ones) * (1.0 / GROUP_LANES)
    xc = x - mu
    var = _group_sum(xc * xc, ones) * (1.0 / GROUP_LANES)
    return xc * lax.rsqrt(var + LN_EPS) * g + b


def _l2_normalize(x, ones):
    return x * lax.rsqrt(_group_sum(x * x, ones) + 1e-6)


def _ln_kernel(x_ref, g_ref, b_ref, o_ref):
    o_ref[...] = _layer_norm(x_ref[...], g_ref[...], b_ref[...])


def _ln(x, g, b, tm):
    n, d = x.shape
    return pl.pallas_call(
        _ln_kernel,
        grid=(n // tm,),
        in_specs=[pl.BlockSpec((tm, d), lambda i: (i, 0)),
                  pl.BlockSpec((1, d), lambda i: (0, 0)),
                  pl.BlockSpec((1, d), lambda i: (0, 0))],
        out_specs=pl.BlockSpec((tm, d), lambda i: (i, 0)),
        out_shape=jax.ShapeDtypeStruct((n, d), F32),
        compiler_params=_cparams("parallel"),
        name="ln0",
    )(x, g.reshape(1, d), b.reshape(1, d))


_SEG = {"q": (0, 256), "k": (256, 512), "v": (512, 768), "ub": (768, 1024), "vb": (1024, 1280),
        "xc": (1280, 1536), "qkv": (1536, 2304), "zg": (2304, 2560), "bd": (2560, 2816),
        "ad": (2816, 3072)}
N_IN_PAD = 3072


def _proj_in_kernel(x_ref, w_ref, q_ref, k_ref, v_ref, kb_ref, vb_ref, ub_ref, vbb_ref, xc_ref,
                    qkv_ref, zg_ref, bd_ref, ad_ref):
    x = x_ref[...].astype(BF16)

    def seg(name):
        a, b = _SEG[name]
        return jnp.dot(x, w_ref[:, a:b], preferred_element_type=F32)

    q_ref[...] = (seg("q") * (DH_A ** -0.5)).astype(BF16)
    k = seg("k")
    k_ref[...] = k
    kb_ref[...] = k.astype(BF16)
    v = seg("v")
    v_ref[...] = v
    vb_ref[...] = v.astype(BF16)
    ub_ref[...] = seg("ub")
    vbb_ref[...] = seg("vb")
    xc_ref[...] = seg("xc")
    qkv_ref[...] = seg("qkv")
    zg_ref[...] = seg("zg")
    bd_ref[...] = seg("bd")
    ad_ref[...] = seg("ad")


def _proj_in(x, w, tm):
    n, d = x.shape
    widths = [(256, BF16), (256, F32), (256, F32), (256, BF16), (256, BF16), (256, F32), (256, F32),
              (256, F32), (768, F32), (256, F32), (256, F32), (256, F32)]
    return pl.pallas_call(
        _proj_in_kernel,
        grid=(n // tm,),
        in_specs=[pl.BlockSpec((tm, d), lambda i: (i, 0)),
                  pl.BlockSpec((d, N_IN_PAD), lambda i: (0, 0))],
        out_specs=[pl.BlockSpec((tm, wd), lambda i: (i, 0)) for wd, _ in widths],
        out_shape=[jax.ShapeDtypeStruct((n, wd), dt) for wd, dt in widths],
        compiler_params=_cparams("parallel"),
        name="proj_in",
    )(x, w)


def _prep_w_in(w_in_l):
    main = w_in_l[:, :2560]
    bd = jnp.repeat(w_in_l[:, 2560:2564], GROUP_LANES, axis=1)
    ad = jnp.repeat(w_in_l[:, 2564:2568], GROUP_LANES, axis=1)
    return jnp.concatenate([main, bd, ad], axis=1).astype(BF16)


def _sb_block(z, tri, uo, carry):
    l = jnp.log(1.0 + jnp.exp(-jnp.abs(z)))
    lk = -(jnp.maximum(z, 0.0) + l)
    lb = lk + z
    if tri is not None:
        lk = jnp.where(tri, lk, 0.0)
    r = _dot_exact_rhs(lk, uo)
    att = jnp.exp(lb + r[:, :SB_BLOCK] + carry)
    if tri is not None:
        att = jnp.where(tri, att, 0.0)
    return att, carry + r[:, SB_BLOCK:]


def _suffix_ones():
    r = _iota((SB_BLOCK, 2 * SB_BLOCK), 0)
    c = _iota((SB_BLOCK, 2 * SB_BLOCK), 1)
    return jnp.where((c >= SB_BLOCK) | (r > c), 1.0, 0.0).astype(BF16)


def _sb_prompt_kernel(bias_ref, q_ref, k_ref, v_ref, o_ref, carry_ref, acc_ref):
    i = pl.program_id(1)
    uo = _suffix_ones()
    lane = _iota((SB_BLOCK, SB_BLOCK), 1)
    head_lane = [lane < GROUP_LANES, lane >= GROUP_LANES]
    tri = _iota((SB_BLOCK, SB_BLOCK), 1) < _iota((SB_BLOCK, SB_BLOCK), 0)
    acc_ref[...] = jnp.zeros_like(acc_ref)
    carry_ref[...] = jnp.zeros_like(carry_ref)
    zero = jnp.zeros((SB_BLOCK, SB_BLOCK), BF16)

    def block(j, mask):
        k0 = pl.multiple_of(j * SB_BLOCK, SB_BLOCK)
        for p in range(2):
            lanes = slice(p * SB_BLOCK, (p + 1) * SB_BLOCK)
            kp = k_ref[pl.ds(k0, SB_BLOCK), lanes]
            vp = v_ref[pl.ds(k0, SB_BLOCK), lanes]
            qp = q_ref[:, lanes]
            upd = jnp.zeros((SB_BLOCK, SB_BLOCK), F32)
            for hh in range(2):
                h = 2 * p + hh
                qm = jnp.where(head_lane[hh], qp, zero)
                z = _bdot_nt(qm, kp) + bias_ref[h]
                att, carry = _sb_block(z, mask, uo, carry_ref[h])
                carry_ref[h] = carry
                vm = jnp.where(head_lane[hh], vp, zero)
                upd = upd + jnp.dot(att.astype(BF16), vm, preferred_element_type=F32)
            acc_ref[:, lanes] += upd

    block(i, tri)

    def body(t, c):
        block(i - 1 - t, None)
        return c

    lax.fori_loop(0, i, body, 0)
    o_ref[...] = acc_ref[...]


def _sb_prompt(q, k, v, bias, batch, seq):
    n = q.shape[0]
    nq = seq // SB_BLOCK
    return pl.pallas_call(
        _sb_prompt_kernel,
        grid_spec=pltpu.PrefetchScalarGridSpec(
            num_scalar_prefetch=0,
            grid=(batch, nq),
            in_specs=[pl.BlockSpec(memory_space=pltpu.SMEM),
                      pl.BlockSpec((SB_BLOCK, W_Q), lambda b, i: (b * nq + i, 0)),
                      pl.BlockSpec((seq, W_Q), lambda b, i: (b, 0)),
                      pl.BlockSpec((seq, W_Q), lambda b, i: (b, 0))],
            out_specs=pl.BlockSpec((SB_BLOCK, W_Q), lambda b, i: (b * nq + i, 0)),
            scratch_shapes=[pltpu.VMEM((H_A, SB_BLOCK, SB_BLOCK), F32),
                            pltpu.VMEM((SB_BLOCK, W_Q), F32)]),
        out_shape=jax.ShapeDtypeStruct((n, W_Q), F32),
        compiler_params=_cparams("parallel", "arbitrary"),
        name="sb_prompt",
    )(bias, q, k, v)


SB_ROWS = 8


def _sb_sample_kernel(n_pages, pt_ref, bias_ref, q_ref, kn_ref, vn_ref, *rest):
    kpages = rest[:n_pages]
    vpages = rest[n_pages:2 * n_pages]
    o_ref = rest[2 * n_pages]
    m = H_A * SB_ROWS
    uo = _suffix_ones()
    row = _iota((m, W_Q), 0)
    lane = _iota((m, W_Q), 1)
    own = (row >> 3) == (lane >> 6)
    q8 = q_ref[0]
    qbd = jnp.where(own, jnp.concatenate([q8] * H_A, axis=0), jnp.zeros((m, W_Q), BF16))
    rowh = _iota((m, SB_BLOCK), 0) >> 3
    bias = jnp.zeros((m, SB_BLOCK), F32)
    for h in range(H_A):
        bias = jnp.where(rowh == h, bias_ref[h], bias)
    tri = _iota((m, SB_BLOCK), 1) < (_iota((m, SB_BLOCK), 0) & (SB_ROWS - 1))
    pad = jnp.zeros((SB_BLOCK - SB_ROWS, W_Q), F32)
    kn = jnp.concatenate([kn_ref[0], pad], axis=0)
    vn = jnp.concatenate([vn_ref[0], pad], axis=0)
    carry = jnp.zeros((m, SB_BLOCK), F32)
    att, carry = _sb_block(_bdot_nt(qbd, kn) + bias, tri, uo, carry)
    acc = _bdot(att, vn)
    for p in range(n_pages - 1, -1, -1):
        att, carry = _sb_block(_bdot_nt(qbd, kpages[p][0]) + bias, None, uo, carry)
        acc = acc + _bdot(att, vpages[p][0])
    acc = jnp.where(own, acc, 0.0)
    out = acc[0:SB_ROWS]
    for h in range(1, H_A):
        out = out + acc[h * SB_ROWS:(h + 1) * SB_ROWS]
    o_ref[0] = out


def _sb_sample(q8, k8, v8, cache_k, cache_v, page_table, bias, layer, n_phys):
    bsz, n_pages = page_table.shape
    page = cache_k.shape[1]
    base = layer * n_phys

    def page_spec(p):
        return pl.BlockSpec((1, page, W_Q), lambda b, pt: (base + pt[b, p], 0, 0))

    tok = pl.BlockSpec((1, SB_ROWS, W_Q), lambda b, pt: (b, 0, 0))
    return pl.pallas_call(
        functools.partial(_sb_sample_kernel, n_pages),
        grid_spec=pltpu.PrefetchScalarGridSpec(
            num_scalar_prefetch=1,
            grid=(bsz,),
            in_specs=([pl.BlockSpec(memory_space=pltpu.SMEM), tok, tok, tok]
                      + [page_spec(p) for p in range(n_pages)] * 2),
            out_specs=tok),
        out_shape=jax.ShapeDtypeStruct((bsz, SB_ROWS, W_Q), F32),
        compiler_params=_cparams("parallel"),
        name="sb_sample",
    )(page_table, bias, q8, k8, v8, *([cache_k] * n_pages), *([cache_v] * n_pages))


def _gmlp_prompt_kernel(u_ref, v_ref, ws_ref, bs_ref, g_ref, b_ref, o_ref):
    c = GMLP_CHUNK
    ones = _group_ones(W_Q)
    tril = _iota((c, c), 0) >= _iota((c, c), 1)
    lane_group = _iota((c, W_Q), 1) >> 6
    w = [jnp.where(tril, ws_ref[g], 0.0).astype(BF16) for g in range(G_B)]
    for n in range(u_ref.shape[0] // c):
        rows = slice(n * c, (n + 1) * c)
        u = _gelu_tanh(u_ref[rows, :])
        vn = _group_layer_norm(_gelu_tanh(v_ref[rows, :]), ones, g_ref[...], b_ref[...])
        vnb = vn.astype(BF16)
        mixed = bs_ref[...]
        for g in range(G_B):
            vg = jnp.where(lane_group == g, vnb, jnp.zeros_like(vnb))
            mixed = mixed + jnp.dot(w[g], vg, preferred_element_type=F32)
        o_ref[rows, :] = u * mixed


def _gmlp_prompt(ub, vb, ws, bs_lanes, g, b, tm):
    n = ub.shape[0]
    c = GMLP_CHUNK
    tile = pl.BlockSpec((tm, W_Q), lambda i: (i, 0))
    vec = pl.BlockSpec((1, W_Q), lambda i: (0, 0))
    return pl.pallas_call(
        _gmlp_prompt_kernel,
        grid=(n // tm,),
        in_specs=[tile, tile,
                  pl.BlockSpec((G_B, c, c), lambda i: (0, 0, 0)),
                  pl.BlockSpec((c, W_Q), lambda i: (0, 0)),
                  vec, vec],
        out_specs=tile,
        out_shape=jax.ShapeDtypeStruct((n, W_Q), F32),
        compiler_params=_cparams("parallel"),
        name="gmlp_prompt",
    )(ub, vb, ws, bs_lanes, g, b)


HALO = 16


def _pool_windows(load, t, start_pos):
    lane = _iota((t, SB_BLOCK), 1)
    pos = start_pos + _iota((t, SB_BLOCK), 0)
    halves = []
    for half, (w_small, w_big) in enumerate(((POOL_WINDOWS[0], POOL_WINDOWS[1]),
                                              (POOL_WINDOWS[2], POOL_WINDOWS[3]))):
        s = load(0, half)
        for i in range(1, w_small):
            s = s + load(i, half)
        big = s
        for i in range(w_small, w_big):
            big = big + load(i, half)
        small_lane = lane < GROUP_LANES
        win = jnp.where(small_lane, s, big)
        w = jnp.where(small_lane, w_small, w_big)
        cnt = jnp.minimum(pos + 1, w).astype(F32)
        halves.append(win / cnt)
    return jnp.concatenate(halves, axis=1)


def _pool_prompt_kernel(tiles_per_seq, x_ref, halo_ref, w_ref, gamma_ref, o_ref, xx_ref):
    i = pl.program_id(0)
    t = x_ref.shape[0]
    first = (i % tiles_per_seq) == 0
    xx_ref[0:HALO, :] = jnp.where(first, 0.0, halo_ref[...])
    xx_ref[HALO:, :] = x_ref[...]

    def load(shift, half):
        return xx_ref[pl.ds(HALO - shift, t), half * SB_BLOCK:(half + 1) * SB_BLOCK]

    start = (i % tiles_per_seq) * t
    means = _pool_windows(load, t, start)
    d = means - x_ref[...]
    o_ref[...] = _bdot(d, w_ref[...]) * gamma_ref[...]


def _pool_prompt(xc, w_bd, gamma, seq, tm):
    n = xc.shape[0]
    hb = tm // HALO
    return pl.pallas_call(
        functools.partial(_pool_prompt_kernel, seq // tm),
        grid=(n // tm,),
        in_specs=[pl.BlockSpec((tm, W_Q), lambda i: (i, 0)),
                  pl.BlockSpec((HALO, W_Q), lambda i: (jnp.maximum(i * hb - 1, 0), 0)),
                  pl.BlockSpec((W_Q, W_Q), lambda i: (0, 0)),
                  pl.BlockSpec((1, W_Q), lambda i: (0, 0))],
        out_specs=pl.BlockSpec((tm, W_Q), lambda i: (i, 0)),
        out_shape=jax.ShapeDtypeStruct((n, W_Q), F32),
        scratch_shapes=[pltpu.VMEM((tm + HALO, W_Q), F32)],
        compiler_params=_cparams("parallel"),
        name="pool_prompt",
    )(xc, xc, w_bd, gamma)


CONV_HALO = 8


def _delta_gates(bd, ad, a_log, dt_bias):
    beta = _sigmoid(bd)
    g = -jnp.exp(a_log) * _softplus(ad + dt_bias)
    return beta, g


def _delta_prep_kernel(tiles_per_seq, x_ref, halo_ref, bd_ref, ad_ref, cw_ref, alog_ref, dt_ref,
                       q_ref, k_ref, v_ref, g_ref, b_ref, xx_ref):
    i = pl.program_id(0)
    t = x_ref.shape[0]
    first = (i % tiles_per_seq) == 0
    xx_ref[0:CONV_HALO, :] = jnp.where(first, 0.0, halo_ref[...])
    xx_ref[CONV_HALO:, :] = x_ref[...]
    ones = _group_ones(W_Q)
    off = CONV_HALO - (CONV_K - 1)
    for part, out in enumerate((q_ref, k_ref, v_ref)):
        lanes = slice(part * W_Q, (part + 1) * W_Q)
        y = xx_ref[pl.ds(off, t), lanes] * cw_ref[0:1, lanes]
        for j in range(1, CONV_K):
            y = y + xx_ref[pl.ds(off + j, t), lanes] * cw_ref[j:j + 1, lanes]
        y = _silu(y)
        out[...] = y if part == 2 else _l2_normalize(y, ones)
    beta, g = _delta_gates(bd_ref[...], ad_ref[...], alog_ref[...], dt_ref[...])
    b_ref[...] = beta
    g_ref[...] = g


def _delta_prep(qkv, bd, ad, conv_w, a_log, dt_bias, seq, tm):
    n, wd = qkv.shape
    hb = tm // CONV_HALO
    tile = pl.BlockSpec((tm, W_Q), lambda i: (i, 0))
    vec = pl.BlockSpec((1, W_Q), lambda i: (0, 0))
    return pl.pallas_call(
        functools.partial(_delta_prep_kernel, seq // tm),
        grid=(n // tm,),
        in_specs=[pl.BlockSpec((tm, wd), lambda i: (i, 0)),
                  pl.BlockSpec((CONV_HALO, wd), lambda i: (jnp.maximum(i * hb - 1, 0), 0)),
                  tile, tile,
                  pl.BlockSpec((CONV_K, wd), lambda i: (0, 0)),
                  vec, vec],
        out_specs=[tile] * 5,
        out_shape=[jax.ShapeDtypeStruct((n, W_Q), F32)] * 5,
        scratch_shapes=[pltpu.VMEM((tm + CONV_HALO, wd), F32)],
        compiler_params=_cparams("parallel"),
        name="delta_prep",
    )(qkv, qkv, bd, ad, conv_w, a_log, dt_bias)


def _delta_kernel(q_ref, k_ref, v_ref, g_ref, b_ref, zg_ref, s0_ref, gn_ref, o_ref, sf_ref, s_ref):
    c = pl.program_id(1)
    n_chunks = pl.num_programs(1)
    cs = DELTA_CHUNK
    hw = H_D * DK_D
    row = _iota((cs, hw), 0)
    col = _iota((cs, hw), 1) & (DK_D - 1)
    incl = row >= col
    strict = row > col
    bdmask = (_iota((hw, hw), 0) >> 6) == (_iota((hw, hw), 1) >> 6)
    ones = _group_ones(hw)
    tril = jnp.where(_iota((cs, cs), 0) >= _iota((cs, cs), 1), 1.0, 0.0).astype(BF16)

    def bd(x):
        return jnp.where(bdmask, jnp.concatenate([x] * H_D, axis=0), 0.0)

    @pl.when(c == 0)
    def _():
        s_ref[...] = jnp.where(bdmask, jnp.concatenate([s0_ref[0]] * H_D, axis=1), 0.0)

    q = q_ref[0] * (DK_D ** -0.5)
    k = k_ref[0]
    v = v_ref[0]
    g = g_ref[0]
    beta = b_ref[0]
    gc = _dot_exact_lhs(tril, g)
    diff = _dot_exact_lhs(tril, jnp.where(strict, g, 0.0))
    decay = jnp.where(incl, jnp.exp(jnp.where(incl, diff, 0.0)), 0.0)
    eg = jnp.exp(gc)
    kb = k * beta
    vb = v * beta
    kbd = bd(k).astype(BF16)
    neg_lower = jnp.where(strict, -(_bdot_nt(kb, kbd) * decay), 0.0)
    t_inv = jnp.where(row == col, 1.0, 0.0) + neg_lower
    p = neg_lower
    for _ in range(5):
        p = _dot3(p, bd(p))
        t_inv = t_inv + _dot3(t_inv, bd(p))
    u = _bdot(t_inv, bd(vb))
    w = _bdot(t_inv, bd(kb * eg))
    intra = jnp.where(incl, _bdot_nt(q, kbd) * decay, 0.0)
    qg = q * eg
    gc_last = gc[cs - 1:cs, :]
    kg = k * jnp.exp(gc_last - gc)
    s = s_ref[...]
    v_new = u - _bdot(w, s)
    o = _bdot(qg, s) + _bdot(intra, bd(v_new))
    s_new = s * jnp.exp(gc_last) + jnp.where(bdmask, _bdot_tn(kg, v_new), 0.0)
    s_ref[...] = s_new
    o = o * lax.rsqrt(_group_sum(o * o, ones) * (1.0 / DK_D) + 1e-6) * gn_ref[...]
    o_ref[0] = o * _silu(zg_ref[0])

    @pl.when(c == n_chunks - 1)
    def _():
        sf = s_new[:, 0:DK_D]
        for h in range(1, H_D):
            sf = sf + s_new[:, h * DK_D:(h + 1) * DK_D]
        sf_ref[0] = sf


def _delta(q, k, v, g, beta, zg, s0, gn):
    bsz, t, hw = q.shape
    tok = pl.BlockSpec((1, DELTA_CHUNK, hw), lambda b, c: (b, c, 0))
    st = pl.BlockSpec((1, hw, DK_D), lambda b, c: (b, 0, 0))
    return pl.pallas_call(
        _delta_kernel,
        grid=(bsz, t // DELTA_CHUNK),
        in_specs=[tok] * 6 + [st, pl.BlockSpec((1, hw), lambda b, c: (0, 0))],
        out_specs=[tok, st],
        out_shape=[jax.ShapeDtypeStruct((bsz, t, hw), F32),
                   jax.ShapeDtypeStruct((bsz, hw, DK_D), F32)],
        scratch_shapes=[pltpu.VMEM((hw, hw), F32)],
        compiler_params=_cparams("parallel", "arbitrary"),
        name="delta",
    )(q, k, v, g, beta, zg, s0, gn)


def _sample_mix_kernel(start_pos, u_ref, v_ref, xc_ref, qkv_ref, bd_ref, ad_ref, pool_prev_ref,
                       conv_prev_ref, wl_ref, bsl_ref, lng_ref, lnb_ref, pw_ref, gamma_ref, cw_ref,
                       alog_ref, dt_ref,
                       ob_ref, vn_ref, oc_ref, pool_new_ref, conv_new_ref, q_ref, k_ref, vd_ref,
                       g_ref, b_ref):
    t_new = u_ref.shape[0]
    bsz = u_ref.shape[1]
    ones = _group_ones(W_Q)
    vn = [_group_layer_norm(_gelu_tanh(v_ref[t]), ones, lng_ref[...], lnb_ref[...])
          for t in range(t_new)]
    for t in range(t_new):
        vn_ref[t] = vn[t]
        mixed = bsl_ref[t:t + 1, :] + wl_ref[t, 0:1, :] * vn[0]
        for s in range(1, t + 1):
            mixed = mixed + wl_ref[t, s:s + 1, :] * vn[s]
        ob_ref[t] = _gelu_tanh(u_ref[t]) * mixed
    n_prev = pool_prev_ref.shape[0]

    def pool_row(i):
        return pool_prev_ref[i] if i < n_prev else xc_ref[i - n_prev]

    lane = _iota((bsz, W_Q), 1) >> 6
    for t in range(t_new):
        acc = pool_row(n_prev + t)
        win = None
        done = 1
        for gi, wdw in enumerate(POOL_WINDOWS):
            for i in range(done, wdw):
                acc = acc + pool_row(n_prev + t - i)
            done = wdw
            cnt = float(min(start_pos + t + 1, wdw))
            mean = acc / cnt
            win = mean if win is None else jnp.where(lane >= gi, mean, win)
        d = win - xc_ref[t]
        oc_ref[t] = _bdot(d, pw_ref[...]) * gamma_ref[...]
    for i in range(n_prev):
        pool_new_ref[i] = pool_row(t_new + i)
    c_prev = conv_prev_ref.shape[0]

    def conv_row(i):
        return conv_prev_ref[i] if i < c_prev else qkv_ref[i - c_prev]

    for t in range(t_new):
        y = conv_row(t) * cw_ref[0:1, :]
        for j in range(1, CONV_K):
            y = y + conv_row(t + j) * cw_ref[j:j + 1, :]
        y = _silu(y)
        q_ref[t] = _l2_normalize(y[:, 0:W_Q], ones)
        k_ref[t] = _l2_normalize(y[:, W_Q:2 * W_Q], ones)
        vd_ref[t] = y[:, 2 * W_Q:3 * W_Q]
        beta, g = _delta_gates(bd_ref[t], ad_ref[t], alog_ref[...], dt_ref[...])
        b_ref[t] = beta
        g_ref[t] = g
    for i in range(c_prev):
        conv_new_ref[i] = conv_row(t_new + i)


def _sample_mix(start_pos, ub, vb, xc, qkv, bd, ad, pool_prev, conv_prev, wl, bsl, lng, lnb, pw,
                gamma, cw, a_log, dt_bias):
    t_new, bsz, _ = ub.shape
    slab = jax.ShapeDtypeStruct((t_new, bsz, W_Q), F32)
    out_shape = [slab, slab, slab,
                 jax.ShapeDtypeStruct(pool_prev.shape, F32),
                 jax.ShapeDtypeStruct(conv_prev.shape, F32),
                 slab, slab, slab, slab, slab]
    return pl.pallas_call(
        functools.partial(_sample_mix_kernel, start_pos),
        out_shape=out_shape,
        compiler_params=pltpu.CompilerParams(vmem_limit_bytes=VMEM_LIMIT),
        name="sample_mix",
    )(ub, vb, xc, qkv, bd, ad, pool_prev, conv_prev, wl, bsl, lng, lnb, pw, gamma, cw, a_log,
      dt_bias)


def _mm_ln_kernel(n_in, *refs):
    xs = refs[:n_in]
    ws = refs[n_in:2 * n_in]
    h_ref, g_ref, b_ref, o_ref = refs[2 * n_in:]
    acc = ALPHA * h_ref[...]
    for x_ref, w_ref in zip(xs, ws):
        acc = acc + _bdot(x_ref[...], w_ref[...])
    o_ref[...] = _layer_norm(acc, g_ref[...], b_ref[...])


def _mm_ln(xs, ws, h, g, b, tm):
    n, d = h.shape
    n_in = len(xs)
    return pl.pallas_call(
        functools.partial(_mm_ln_kernel, n_in),
        grid=(n // tm,),
        in_specs=([pl.BlockSpec((tm, x.shape[1]), lambda i: (i, 0)) for x in xs]
                  + [pl.BlockSpec(w.shape, lambda i: (0, 0)) for w in ws]
                  + [pl.BlockSpec((tm, d), lambda i: (i, 0)),
                     pl.BlockSpec((1, d), lambda i: (0, 0)),
                     pl.BlockSpec((1, d), lambda i: (0, 0))]),
        out_specs=pl.BlockSpec((tm, d), lambda i: (i, 0)),
        out_shape=jax.ShapeDtypeStruct((n, d), F32),
        compiler_params=_cparams("parallel"),
        name="mm_ln",
    )(*xs, *ws, h, g.reshape(1, d), b.reshape(1, d))


def _mm_kernel(scale, x_ref, w_ref, o_ref, ob_ref):
    y = _bdot(x_ref[...], w_ref[...])
    o_ref[...] = y
    ob_ref[...] = (y * scale).astype(BF16)


def _mm(x, w, tm, scale=1.0):
    n, d = x.shape
    m = w.shape[1]
    return pl.pallas_call(
        functools.partial(_mm_kernel, scale),
        grid=(n // tm,),
        in_specs=[pl.BlockSpec((tm, d), lambda i: (i, 0)),
                  pl.BlockSpec((d, m), lambda i: (0, 0))],
        out_specs=[pl.BlockSpec((tm, m), lambda i: (i, 0))] * 2,
        out_shape=[jax.ShapeDtypeStruct((n, m), F32), jax.ShapeDtypeStruct((n, m), BF16)],
        compiler_params=_cparams("parallel"),
        name="mm",
    )(x, w)


def _xattn_heads(q, k_ref, v_ref):
    dh = q.shape[1] // XA_HEADS
    outs = []
    for h in range(XA_HEADS):
        lanes = slice(h * dh, (h + 1) * dh)
        s = _bdot_nt(q[:, lanes], k_ref[0, :, lanes])
        s = s - jnp.max(s, axis=-1, keepdims=True)
        e = jnp.exp(s)
        p = e / jnp.sum(e, axis=-1, keepdims=True)
        outs.append(_bdot(p, v_ref[0, :, lanes]))
    return jnp.concatenate(outs, axis=1)


def _xattn_prompt_kernel(h_ref, wq_ref, wo_ref, k_ref, v_ref, g_ref, b_ref, o_ref):
    h = h_ref[...]
    dh = h.shape[1] // XA_HEADS
    q = (_bdot(h, wq_ref[...]) * (dh ** -0.5)).astype(BF16)
    o = _xattn_heads(q, k_ref, v_ref)
    o_ref[...] = _layer_norm(ALPHA * h + _bdot(o, wo_ref[...]), g_ref[...], b_ref[...])


def _xattn_prompt(h, wq, wo, mem_k, mem_v, g, b, seq, tm):
    n, d = h.shape
    n_mem = mem_k.shape[1]
    tps = seq // tm
    full = pl.BlockSpec((d, d), lambda i: (0, 0))
    mem = pl.BlockSpec((1, n_mem, d), lambda i: (i // tps, 0, 0))
    vec = pl.BlockSpec((1, d), lambda i: (0, 0))
    return pl.pallas_call(
        _xattn_prompt_kernel,
        grid=(n // tm,),
        in_specs=[pl.BlockSpec((tm, d), lambda i: (i, 0)), full, full, mem, mem, vec, vec],
        out_specs=pl.BlockSpec((tm, d), lambda i: (i, 0)),
        out_shape=jax.ShapeDtypeStruct((n, d), F32),
        compiler_params=_cparams("parallel"),
        name="xattn_prompt",
    )(h, wq, wo, mem_k, mem_v, g.reshape(1, d), b.reshape(1, d))


def _xattn_sample_kernel(q_ref, k_ref, v_ref, o_ref):
    o_ref[0] = _xattn_heads(q_ref[0], k_ref, v_ref)


def _xattn_sample(q8, mem_k, mem_v, layer):
    bsz, rows, d = q8.shape
    n_mem = mem_k.shape[1]
    tok = pl.BlockSpec((1, rows, d), lambda b: (b, 0, 0))
    mem = pl.BlockSpec((1, n_mem, d), lambda b: (layer * bsz + b, 0, 0))
    return pl.pallas_call(
        _xattn_sample_kernel,
        grid=(bsz,),
        in_specs=[tok, mem, mem],
        out_specs=tok,
        out_shape=jax.ShapeDtypeStruct((bsz, rows, d), F32),
        compiler_params=_cparams("parallel"),
        name="xattn_sample",
    )(q8, mem_k, mem_v)


def _top2_sum(a, b, c, d):
    m1, n1 = jnp.maximum(a, b), jnp.minimum(a, b)
    m2, n2 = jnp.maximum(c, d), jnp.minimum(c, d)
    return jnp.maximum(m1, m2) + jnp.maximum(jnp.minimum(m1, m2), jnp.maximum(n1, n2))


def _router_kernel(x_ref, wt_ref, bias_ref, o_ref):
    x = x_ref[...]
    wt = wt_ref[...]
    xh, xl = _split(x)
    wh, wl = _split(wt)
    nt = (((1,), (1,)), ((), ()))
    logits = (lax.dot_general(wh, xh, nt, preferred_element_type=F32)
              + lax.dot_general(wh, xl, nt, preferred_element_type=F32)
              + lax.dot_general(wl, xh, nt, preferred_element_type=F32))
    s_all = _sigmoid(logits)
    sel_all = s_all + bias_ref[...]
    s = [s_all[e:e + 1, :] for e in range(N_EXPERTS)]
    sel = [sel_all[e:e + 1, :] for e in range(N_EXPERTS)]
    epg = EXPERTS_PER_GROUP
    score = [_top2_sum(*sel[g * epg:(g + 1) * epg]) for g in range(N_GROUPS)]
    best = functools.reduce(jnp.maximum, score)
    taken = None
    in_group = []
    for g in range(N_GROUPS):
        hit = score[g] == best
        if taken is None:
            pick, taken = hit, hit
        else:
            pick = hit & jnp.logical_not(taken)
            taken = taken | hit
        in_group.append(pick)
    chosen = []
    for g in range(N_GROUPS):
        vals = sel[g * epg:(g + 1) * epg]
        for j in range(epg):
            ahead = jnp.zeros_like(vals[j])
            for i in range(epg):
                if i == j:
                    continue
                before = (vals[i] >= vals[j]) if i < j else (vals[i] > vals[j])
                ahead = ahead + jnp.where(before, 1.0, 0.0)
            chosen.append(in_group[g] & (ahead < 1.5))
    gates = [jnp.where(chosen[e], s[e], 0.0) for e in range(N_EXPERTS)]
    total = functools.reduce(lambda a, b: a + b, gates)
    o_ref[...] = jnp.concatenate(gates, axis=0) / total


def _router(x, w_router_t, bias, tm):
    n, d = x.shape
    e = w_router_t.shape[0]
    return pl.pallas_call(
        _router_kernel,
        grid=(n // tm,),
        in_specs=[pl.BlockSpec((tm, d), lambda i: (i, 0)),
                  pl.BlockSpec((e, d), lambda i: (0, 0)),
                  pl.BlockSpec((e, 1), lambda i: (0, 0))],
        out_specs=pl.BlockSpec((e, tm), lambda i: (0, i)),
        out_shape=jax.ShapeDtypeStruct((e, n), F32),
        compiler_params=_cparams("parallel"),
        name="router",
    )(x, w_router_t, bias.reshape(e, 1))


def _moe_kernel(x_ref, comb_ref, wg_ref, wu_ref, wd_ref, g_ref, b_ref, o_ref, xb_ref, acc_ref):
    e = pl.program_id(1)

    @pl.when(e == 0)
    def _():
        xb_ref[...] = x_ref[...].astype(BF16)
        acc_ref[...] = jnp.zeros_like(acc_ref)

    xb = xb_ref[...]
    hg = jnp.dot(xb, wg_ref[0], preferred_element_type=F32)
    hu = jnp.dot(xb, wu_ref[0], preferred_element_type=F32)
    comb = comb_ref[...]
    gate = jnp.sum(jnp.where(_iota(comb.shape, 1) == e, comb, 0.0), axis=1, keepdims=True)
    act = _silu(hg) * hu * gate
    acc_ref[...] += _bdot(act, wd_ref[0])

    @pl.when(e == pl.num_programs(1) - 1)
    def _():
        o_ref[...] = _layer_norm(ALPHA * x_ref[...] + acc_ref[...], g_ref[...], b_ref[...])


def _moe(x, comb, wg, wu, wd, g, b, tm):
    n, d = x.shape
    ne, _, f = wg.shape
    return pl.pallas_call(
        _moe_kernel,
        grid=(n // tm, ne),
        in_specs=[pl.BlockSpec((tm, d), lambda i, e: (i, 0)),
                  pl.BlockSpec((tm, ne), lambda i, e: (i, 0)),
                  pl.BlockSpec((1, d, f), lambda i, e: (e, 0, 0)),
                  pl.BlockSpec((1, d, f), lambda i, e: (e, 0, 0)),
                  pl.BlockSpec((1, f, d), lambda i, e: (e, 0, 0)),
                  pl.BlockSpec((1, d), lambda i, e: (0, 0)),
                  pl.BlockSpec((1, d), lambda i, e: (0, 0))],
        out_specs=pl.BlockSpec((tm, d), lambda i, e: (i, 0)),
        out_shape=jax.ShapeDtypeStruct((n, d), F32),
        scratch_shapes=[pltpu.VMEM((tm, d), BF16), pltpu.VMEM((tm, d), F32)],
        compiler_params=_cparams("parallel", "arbitrary"),
        name="moe",
    )(x, comb, wg, wu, wd, g.reshape(1, d), b.reshape(1, d))


def _lanes(x):
    if x.ndim == 1:
        x = jnp.repeat(x, GROUP_LANES)
    return x.reshape(1, W_Q).astype(F32)


def _block_diag(w):
    g, a, b = w.shape
    eye = jnp.eye(g, dtype=w.dtype)
    return (eye[:, None, :, None] * w[:, :, None, :]).reshape(g * a, g * b)


def _tile(n, pref):
    t = min(n, pref)
    while n % t:
        t //= 2
    return t


def kernel(x_prompt, x_sample, cache_sb_k, cache_sb_v, state_pool, state_conv, state_delta,
           cache_mem_k, cache_mem_v, page_table, mem_prompt,
           ln0_g, ln0_b, w_in, sb_bias, gmlp_ln_g, gmlp_ln_b, gmlp_ws, gmlp_bs, pool_w, pool_gamma,
           conv_w, dn_a_log, dn_dt_bias, dn_norm_g, w_out, ln1_g, ln1_b,
           xa_wq, xa_wk, xa_wv, xa_wo, ln2_g, ln2_b, w_router, router_bias,
           ex_wg, ex_wu, ex_wd, ln3_g, ln3_b):
    bp, seq, d = x_prompt.shape
    bs, t_new, _ = x_sample.shape
    n_p = bp * seq
    n_s = bs * t_new
    n_mem = mem_prompt.shape[1]
    n_phys, page = cache_sb_k.shape[1], cache_sb_k.shape[2]
    past_len = page_table.shape[1] * page
    xa_dh = d // XA_HEADS
    tm_p = _tile(seq, 512)
    tm_s = _tile(n_s, 512)
    tm_moe_p = _tile(n_p, 1024)
    chunk_pad = DELTA_CHUNK - t_new

    cache_k = cache_sb_k.reshape(DEPTH * n_phys, page, W_Q)
    cache_v = cache_sb_v.reshape(DEPTH * n_phys, page, W_Q)
    memk_s = cache_mem_k.reshape(DEPTH * bs, n_mem, d)
    memv_s = cache_mem_v.reshape(DEPTH * bs, n_mem, d)
    w_router_t = w_router.T
    mem_flat = mem_prompt.reshape(bp * n_mem, d)

    def rows8(x):
        x = x.reshape(bs, t_new, -1)
        return jnp.pad(x, ((0, 0), (0, SB_ROWS - t_new), (0, 0)))

    def tmajor(x):
        return x.reshape(bs, t_new, -1).transpose(1, 0, 2)

    def bmajor(x):
        return x.transpose(1, 0, 2)

    h_p = _ln(x_prompt.reshape(n_p, d), ln0_g, ln0_b, tm_p)
    h_s = _ln(x_sample.reshape(n_s, d), ln0_g, ln0_b, tm_s)

    outs = {k: [] for k in ("kp", "vp", "ks", "vs", "gv", "pp", "ps", "cp", "cs", "dp", "ds",
                            "mk", "mv")}
    for l in range(DEPTH):
        w_in_l = _prep_w_in(w_in[l])
        w_out_l = w_out[l].astype(BF16)
        w_out_parts = [w_out_l[i * W_Q:(i + 1) * W_Q] for i in range(4)]
        ws_l = gmlp_ws[l]
        bs_lanes = jnp.repeat(gmlp_bs[l].T, GROUP_LANES, axis=1)
        lng, lnb = _lanes(gmlp_ln_g[l]), _lanes(gmlp_ln_b[l])
        pw = _block_diag(pool_w[l]).astype(BF16)
        gamma = pool_gamma[l].reshape(1, W_Q)
        cw = conv_w[l]
        a_log, dt_b = _lanes(dn_a_log[l]), _lanes(dn_dt_bias[l])
        gn = jnp.tile(dn_norm_g[l], H_D).reshape(1, W_Q)
        wq, wo = xa_wq[l].astype(BF16), xa_wo[l].astype(BF16)
        wg, wu, wd = ex_wg[l].astype(BF16), ex_wu[l].astype(BF16), ex_wd[l].astype(BF16)

        mk, mk_b = _mm(mem_flat, xa_wk[l].astype(BF16), _tile(bp * n_mem, 256))
        mv, mv_b = _mm(mem_flat, xa_wv[l].astype(BF16), _tile(bp * n_mem, 256))
        outs["mk"].append(mk.reshape(bp, n_mem, XA_HEADS, xa_dh))
        outs["mv"].append(mv.reshape(bp, n_mem, XA_HEADS, xa_dh))

        (q, k, v, k_b, v_b, ub, vb, xc, qkv, zg, bd, ad) = _proj_in(h_p, w_in_l, tm_p)
        outs["kp"].append(k.reshape(bp, seq, H_A, DH_A))
        outs["vp"].append(v.reshape(bp, seq, H_A, DH_A))
        outs["pp"].append(xc.reshape(bp, seq, W_Q)[:, seq - POOL_BUF:])
        outs["cp"].append(qkv.reshape(bp, seq, 3 * W_Q)[:, seq - (CONV_K - 1):])
        oa = _sb_prompt(q, k_b, v_b, sb_bias[l], bp, seq)
        ob = _gmlp_prompt(ub, vb, ws_l, bs_lanes, lng, lnb, tm_p)
        oc = _pool_prompt(xc, pw, gamma, seq, tm_p)
        qd, kd, vd, gd, betad = _delta_prep(qkv, bd, ad, cw, a_log, dt_b, seq, tm_p)
        r3 = lambda x: x.reshape(bp, seq, W_Q)
        od, s_fin = _delta(r3(qd), r3(kd), r3(vd), r3(gd), r3(betad), r3(zg),
                           jnp.zeros((bp, H_D * DK_D, DK_D), F32), gn)
        outs["dp"].append(s_fin.reshape(bp, H_D, DK_D, DK_D))
        h_p = _mm_ln([oa, ob, oc, od.reshape(n_p, W_Q)], w_out_parts, h_p, ln1_g[l], ln1_b[l], tm_p)
        h_p = _xattn_prompt(h_p, wq, wo, mk_b.reshape(bp, n_mem, d), mv_b.reshape(bp, n_mem, d),
                            ln2_g[l], ln2_b[l], seq, tm_p)
        comb = _router(h_p, w_router_t, router_bias, tm_p).T
        h_p = _moe(h_p, comb, wg, wu, wd, ln3_g[l], ln3_b[l], tm_moe_p)

        (q, k, v, _, _, ub, vb, xc, qkv, zg, bd, ad) = _proj_in(h_s, w_in_l, tm_s)
        outs["ks"].append(k.reshape(bs, t_new, H_A, DH_A))
        outs["vs"].append(v.reshape(bs, t_new, H_A, DH_A))
        oa = _sb_sample(rows8(q), rows8(k), rows8(v), cache_k, cache_v, page_table, sb_bias[l],
                        l, n_phys)[:, :t_new].reshape(n_s, W_Q)
        wl = jnp.repeat(jnp.where(jnp.tril(jnp.ones((t_new, t_new), bool)),
                                  ws_l[:, :t_new, :t_new], 0.0).transpose(1, 2, 0),
                        GROUP_LANES, axis=2)
        (ob, vn, oc, pool_new, conv_new, qd, kd, vd, gd, betad) = _sample_mix(
            past_len, tmajor(ub), tmajor(vb), tmajor(xc), tmajor(qkv), tmajor(bd), tmajor(ad),
            bmajor(state_pool[l]), bmajor(state_conv[l]), wl, bs_lanes[:t_new], lng, lnb, pw,
            gamma, cw, a_log, dt_b)
        outs["gv"].append(bmajor(vn))
        outs["ps"].append(bmajor(pool_new))
        outs["cs"].append(bmajor(conv_new))
        padc = lambda x: jnp.pad(bmajor(x), ((0, 0), (0, chunk_pad), (0, 0)))
        zg_pad = jnp.pad(zg.reshape(bs, t_new, W_Q), ((0, 0), (0, chunk_pad), (0, 0)))
        od, s_fin = _delta(padc(qd), padc(kd), padc(vd), padc(gd), padc(betad), zg_pad,
                           state_delta[l].reshape(bs, H_D * DK_D, DK_D), gn)
        outs["ds"].append(s_fin.reshape(bs, H_D, DK_D, DK_D))
        od = od[:, :t_new].reshape(n_s, W_Q)
        flat = lambda x: bmajor(x).reshape(n_s, W_Q)
        h_s = _mm_ln([oa, flat(ob), flat(oc), od], w_out_parts, h_s, ln1_g[l], ln1_b[l], tm_s)
        _, qx = _mm(h_s, wq, tm_s, scale=xa_dh ** -0.5)
        ox = _xattn_sample(rows8(qx), memk_s, memv_s, l)[:, :t_new].reshape(n_s, d)
        h_s = _mm_ln([ox], [wo], h_s, ln2_g[l], ln2_b[l], tm_s)
        comb = _router(h_s, w_router_t, router_bias, tm_s).T
        h_s = _moe(h_s, comb, wg, wu, wd, ln3_g[l], ln3_b[l], tm_s)

    st = lambda key: jnp.stack(outs[key])
    return (h_p.reshape(bp, seq, d), h_s.reshape(bs, t_new, d),
            st("kp"), st("vp"), st("ks"), st("vs"), st("gv"),
            st("pp"), st("ps"), st("cp"), st("cs"), st("dp"), st("ds"),
            st("mk"), st("mv"))
```

```python
import functools

import jax
import jax.numpy as jnp
from jax import lax
from jax.experimental import pallas as pl
from jax.experimental.pallas import tpu as pltpu

F32 = jnp.float32
BF16 = jnp.bfloat16

DEPTH = 2
H_A = 4
DH_A = 64
G_B = 4
GMLP_CHUNK = 128
POOL_WINDOWS = (2, 4, 8, 16)
POOL_BUF = 15
H_D = 4
DK_D = 64
CONV_K = 4
DELTA_CHUNK = 64
XA_HEADS = 4
N_EXPERTS = 16
N_GROUPS = 4
EXPERTS_PER_GROUP = 4
ALPHA = (2 * DEPTH) ** 0.25
LN_EPS = 1e-5
GROUP_LANES = 64
W_Q = 256

SB_BLOCK = 128
VMEM_LIMIT = 48 * 1024 * 1024


def _cparams(*sem):
    return pltpu.CompilerParams(dimension_semantics=sem, vmem_limit_bytes=VMEM_LIMIT)


def _bdot(a, b):
    return jnp.dot(a.astype(BF16), b.astype(BF16), preferred_element_type=F32)


def _bdot_nt(a, b):
    return lax.dot_general(a.astype(BF16), b.astype(BF16), (((1,), (1,)), ((), ())),
                           preferred_element_type=F32)


def _bdot_tn(a, b):
    return lax.dot_general(a.astype(BF16), b.astype(BF16), (((0,), (0,)), ((), ())),
                           preferred_element_type=F32)


def _split(x):
    hi = x.astype(BF16)
    lo = (x - hi.astype(F32)).astype(BF16)
    return hi, lo


def _dot_exact_rhs(x, c):
    hi, lo = _split(x)
    return (jnp.dot(hi, c, preferred_element_type=F32)
            + jnp.dot(lo, c, preferred_element_type=F32))


def _dot_exact_lhs(c, x):
    hi, lo = _split(x)
    return (jnp.dot(c, hi, preferred_element_type=F32)
            + jnp.dot(c, lo, preferred_element_type=F32))


def _dot3(a, b):
    ah, al = _split(a)
    bh, bl = _split(b)
    return (jnp.dot(ah, bh, preferred_element_type=F32)
            + jnp.dot(ah, bl, preferred_element_type=F32)
            + jnp.dot(al, bh, preferred_element_type=F32))


def _iota(shape, dim):
    return lax.broadcasted_iota(jnp.int32, shape, dim)


def _group_ones(n):
    return jnp.where((_iota((n, n), 0) >> 6) == (_iota((n, n), 1) >> 6), 1.0, 0.0).astype(BF16)


def _group_sum(x, ones):
    return _dot_exact_rhs(x, ones)


def _layer_norm(x, g, b):
    mu = jnp.mean(x, axis=-1, keepdims=True)
    xc = x - mu
    var = jnp.mean(xc * xc, axis=-1, keepdims=True)
    return xc * lax.rsqrt(var + LN_EPS) * g + b


def _sigmoid(x):
    return 1.0 / (1.0 + jnp.exp(-x))


def _silu(x):
    return x * _sigmoid(x)


def _softplus(x):
    return jnp.maximum(x, 0.0) + jnp.log(1.0 + jnp.exp(-jnp.abs(x)))


def _gelu_tanh(x):
    return 0.5 * x * (1.0 + jnp.tanh(0.7978845608028654 * (x + 0.044715 * (x * x * x))))


def _group_layer_norm(x, ones, g, b):
    mu = _group_sum(x, ones) * (1.0 / GROUP_LANES)
    xc = x - mu
    var = _group_sum(xc * xc, ones) * (1.0 / GROUP_LANES)
    return xc * lax.rsqrt(var + LN_EPS) * g + b


def _l2_normalize(x, ones):
    return x * lax.rsqrt(_group_sum(x * x, ones) + 1e-6)


def _ln_kernel(x_ref, g_ref, b_ref, o_ref):
    o_ref[...] = _layer_norm(x_ref[...], g_ref[...], b_ref[...])


def _ln(x, g, b, tm):
    n, d = x.shape
    return pl.pallas_call(
        _ln_kernel,
        grid=(n // tm,),
        in_specs=[pl.BlockSpec((tm, d), lambda i: (i, 0)),
                  pl.BlockSpec((1, d), lambda i: (0, 0)),
                  pl.BlockSpec((1, d), lambda i: (0, 0))],
        out_specs=pl.BlockSpec((tm, d), lambda i: (i, 0)),
        out_shape=jax.ShapeDtypeStruct((n, d), F32),
        compiler_params=_cparams("parallel"),
        name="ln0",
    )(x, g.reshape(1, d), b.reshape(1, d))


_SEG = {"q": (0, 256), "k": (256, 512), "v": (512, 768), "ub": (768, 1024), "vb": (1024, 1280),
        "xc": (1280, 1536), "qkv": (1536, 2304), "zg": (2304, 2560), "bd": (2560, 2816),
        "ad": (2816, 3072)}
N_IN_PAD = 3072


def _proj_in_kernel(x_ref, w_ref, q_ref, k_ref, v_ref, kb_ref, vb_ref, ub_ref, vbb_ref, xc_ref,
                    qkv_ref, zg_ref, bd_ref, ad_ref):
    x = x_ref[...].astype(BF16)

    def seg(name):
        a, b = _SEG[name]
        return jnp.dot(x, w_ref[:, a:b], preferred_element_type=F32)

    first_head = _iota((x.shape[0], SB_BLOCK), 1) < GROUP_LANES

    def head_stacked(y):
        yb = y.astype(BF16)
        zero = jnp.zeros((x.shape[0], SB_BLOCK), BF16)
        parts = []
        for p in range(H_A // 2):
            yp = yb[:, p * SB_BLOCK:(p + 1) * SB_BLOCK]
            parts += [jnp.where(first_head, yp, zero), jnp.where(first_head, zero, yp)]
        return jnp.concatenate(parts, axis=1)

    q_ref[...] = (seg("q") * (DH_A ** -0.5)).astype(BF16)
    k = seg("k")
    k_ref[...] = k
    kb_ref[...] = head_stacked(k)
    v = seg("v")
    v_ref[...] = v
    vb_ref[...] = head_stacked(v)
    ub_ref[...] = seg("ub")
    vbb_ref[...] = seg("vb")
    xc_ref[...] = seg("xc")
    qkv_ref[...] = seg("qkv")
    zg_ref[...] = seg("zg")
    bd_ref[...] = seg("bd")
    ad_ref[...] = seg("ad")


def _proj_in(x, w, tm):
    n, d = x.shape
    widths = [(256, BF16), (256, F32), (256, F32), (512, BF16), (512, BF16), (256, F32), (256, F32),
              (256, F32), (768, F32), (256, F32), (256, F32), (256, F32)]
    return pl.pallas_call(
        _proj_in_kernel,
        grid=(n // tm,),
        in_specs=[pl.BlockSpec((tm, d), lambda i: (i, 0)),
                  pl.BlockSpec((d, N_IN_PAD), lambda i: (0, 0))],
        out_specs=[pl.BlockSpec((tm, wd), lambda i: (i, 0)) for wd, _ in widths],
        out_shape=[jax.ShapeDtypeStruct((n, wd), dt) for wd, dt in widths],
        compiler_params=_cparams("parallel"),
        name="proj_in",
    )(x, w)


def _prep_w_in(w_in_l):
    main = w_in_l[:, :2560]
    bd = jnp.repeat(w_in_l[:, 2560:2564], GROUP_LANES, axis=1)
    ad = jnp.repeat(w_in_l[:, 2564:2568], GROUP_LANES, axis=1)
    return jnp.concatenate([main, bd, ad], axis=1).astype(BF16)


def _sb_block(z, tri, uo, carry):
    l = jnp.log(1.0 + jnp.exp(-jnp.abs(z)))
    lk = -(jnp.maximum(z, 0.0) + l)
    lb = lk + z
    if tri is not None:
        lk = jnp.where(tri, lk, 0.0)
    r = _dot_exact_rhs(lk, uo)
    att = jnp.exp(lb + r[:, :SB_BLOCK] + carry)
    if tri is not None:
        att = jnp.where(tri, att, 0.0)
    return att, carry + r[:, SB_BLOCK:]


def _suffix_ones():
    r = _iota((SB_BLOCK, 2 * SB_BLOCK), 0)
    c = _iota((SB_BLOCK, 2 * SB_BLOCK), 1)
    return jnp.where((c >= SB_BLOCK) | (r > c), 1.0, 0.0).astype(BF16)


NEG_BIG = -1e30


LOG2E = 1.4426950408889634


def _pair_rows(x, p):
    w = 2 * SB_BLOCK
    return jnp.concatenate([x[:, p * w:p * w + SB_BLOCK], x[:, p * w + SB_BLOCK:(p + 1) * w]], axis=0)


def _suffix_ones_hilo():
    nb = SB_BLOCK
    r = lax.broadcasted_iota(jnp.int32, (2 * nb, 2 * nb), 0) & (nb - 1)
    c = lax.broadcasted_iota(jnp.int32, (2 * nb, 2 * nb), 1)
    return jnp.where((c >= nb) | (r > c), 1.0, 0.0).astype(BF16)


def _sb_prompt_kernel(bias_ref, q_ref, k_ref, v_ref, uo_ref, o_ref, carry_ref, acc_ref,
                      z_a, att_a, z_b, att_b):
    i = pl.program_id(1)
    nb = SB_BLOCK
    pw = 2 * nb
    head_of_lane = _iota((1, H_A * nb), 1) >> 7
    bias_row = jnp.zeros((1, H_A * nb), F32)
    for h in range(H_A):
        bias_row = jnp.where(head_of_lane == h, bias_ref[h], bias_row)
    col_minus_row = _iota((nb, nb), 1) - _iota((nb, nb), 0)

    acc_ref[...] = jnp.zeros_like(acc_ref)
    carry_ref[...] = jnp.zeros_like(carry_ref)
    z_b[...] = jnp.zeros_like(z_b)
    att_b[...] = jnp.zeros_like(att_b)

    def stages(t, z_w, att_w, z_r, att_r, masked):
        ja = jnp.maximum(i - t, 0)
        k = k_ref[pl.ds(pl.multiple_of(ja * nb, nb), nb), :]
        for p in range(2):
            z_w[:, p * pw:(p + 1) * pw] = lax.dot_general(
                q_ref[:, p * nb:(p + 1) * nb], _pair_rows(k, p), (((1,), (1,)), ((), ())),
                preferred_element_type=F32)
        z = z_r[...] + bias_row
        l = jnp.log(1.0 + jnp.exp2(jnp.abs(z) * (-LOG2E)))
        nlk = jnp.maximum(z, 0.0) + l
        lb = z - nlk
        if masked:
            tb = t - 1
            span = jnp.where((tb >= 0) & (tb <= i), tb * nb, -2 * nb)
            valid = col_minus_row < span
            nlk = nlk * jnp.concatenate([jnp.where(valid, 1.0, 0.0)] * H_A, axis=1)
            lb = lb + jnp.concatenate([jnp.where(valid, 0.0, NEG_BIG)] * H_A, axis=1)
        hi, lo = _split(nlk)
        res = [jnp.dot(jnp.concatenate([hi[:, h * nb:(h + 1) * nb], lo[:, h * nb:(h + 1) * nb]],
                                       axis=1), uo_ref[...], preferred_element_type=F32)
               for h in range(H_A)]
        right = jnp.concatenate([x[:, :nb] for x in res], axis=1)
        total = jnp.concatenate([x[:, nb:] for x in res], axis=1)
        carry = carry_ref[...]
        att_w[...] = jnp.exp(lb - right - carry).astype(BF16)
        carry_ref[...] = carry + total
        jc = jnp.clip(i - t + 2, 0, i)
        v = v_ref[pl.ds(pl.multiple_of(jc * nb, nb), nb), :]
        for p in range(2):
            acc_ref[:, p * nb:(p + 1) * nb] += jnp.dot(att_r[:, p * pw:(p + 1) * pw],
                                                       _pair_rows(v, p),
                                                       preferred_element_type=F32)

    def two_trips(masked, u, carry_unused):
        stages(2 * u, z_a, att_a, z_b, att_b, masked)
        stages(2 * u + 1, z_b, att_b, z_a, att_a, masked)
        return carry_unused

    two_trips(True, 0, 0)
    lax.fori_loop(1, i // 2 + 1, functools.partial(two_trips, False), 0)
    lax.fori_loop(i // 2 + 1, (i + 4) // 2, functools.partial(two_trips, True), 0)
    o_ref[...] = acc_ref[...]


def _sb_prompt(q, k, v, bias, batch, seq):
    n = q.shape[0]
    nq = seq // SB_BLOCK
    hw = H_A * SB_BLOCK
    stage_bufs = [pltpu.VMEM((SB_BLOCK, hw), F32),
                  pltpu.VMEM((SB_BLOCK, hw), BF16)]
    return pl.pallas_call(
        _sb_prompt_kernel,
        grid=(batch, nq),
        in_specs=[pl.BlockSpec(memory_space=pltpu.SMEM),
                  pl.BlockSpec((SB_BLOCK, W_Q), lambda b, i: (b * nq + i, 0)),
                  pl.BlockSpec((seq, hw), lambda b, i: (b, 0)),
                  pl.BlockSpec((seq, hw), lambda b, i: (b, 0)),
                  pl.BlockSpec((2 * SB_BLOCK, 2 * SB_BLOCK), lambda b, i: (0, 0))],
        out_specs=pl.BlockSpec((SB_BLOCK, W_Q), lambda b, i: (b * nq + i, 0)),
        scratch_shapes=[pltpu.VMEM((SB_BLOCK, hw), F32),
                        pltpu.VMEM((SB_BLOCK, W_Q), F32)] + stage_bufs * 2,
        out_shape=jax.ShapeDtypeStruct((n, W_Q), F32),
        compiler_params=_cparams("parallel", "arbitrary"),
        name="sb_prompt",
    )(bias, q, k, v, _suffix_ones_hilo())


SB_ROWS = 8


def _sb_sample_kernel(n_pages, pt_ref, bias_ref, q_ref, kn_ref, vn_ref, *rest):
    kpages = rest[:n_pages]
    vpages = rest[n_pages:2 * n_pages]
    o_ref = rest[2 * n_pages]
    m = H_A * SB_ROWS
    uo = _suffix_ones()
    row = _iota((m, W_Q), 0)
    lane = _iota((m, W_Q), 1)
    own = (row >> 3) == (lane >> 6)
    q8 = q_ref[0]
    qbd = jnp.where(own, jnp.concatenate([q8] * H_A, axis=0), jnp.zeros((m, W_Q), BF16))
    rowh = _iota((m, SB_BLOCK), 0) >> 3
    bias = jnp.zeros((m, SB_BLOCK), F32)
    for h in range(H_A):
        bias = jnp.where(rowh == h, bias_ref[h], bias)
    tri = _iota((m, SB_BLOCK), 1) < (_iota((m, SB_BLOCK), 0) & (SB_ROWS - 1))
    pad = jnp.zeros((SB_BLOCK - SB_ROWS, W_Q), F32)
    kn = jnp.concatenate([kn_ref[0], pad], axis=0)
    vn = jnp.concatenate([vn_ref[0], pad], axis=0)
    carry = jnp.zeros((m, SB_BLOCK), F32)
    att, carry = _sb_block(_bdot_nt(qbd, kn) + bias, tri, uo, carry)
    acc = _bdot(att, vn)
    for p in range(n_pages - 1, -1, -1):
        att, carry = _sb_block(_bdot_nt(qbd, kpages[p][0]) + bias, None, uo, carry)
        acc = acc + _bdot(att, vpages[p][0])
    acc = jnp.where(own, acc, 0.0)
    out = acc[0:SB_ROWS]
    for h in range(1, H_A):
        out = out + acc[h * SB_ROWS:(h + 1) * SB_ROWS]
    o_ref[0] = out


def _sb_sample(q8, k8, v8, cache_k, cache_v, page_table, bias, layer, n_phys):
    bsz, n_pages = page_table.shape
    page = cache_k.shape[1]
    base = layer * n_phys

    def page_spec(p):
        return pl.BlockSpec((1, page, W_Q), lambda b, pt: (base + pt[b, p], 0, 0))

    tok = pl.BlockSpec((1, SB_ROWS, W_Q), lambda b, pt: (b, 0, 0))
    return pl.pallas_call(
        functools.partial(_sb_sample_kernel, n_pages),
        grid_spec=pltpu.PrefetchScalarGridSpec(
            num_scalar_prefetch=1,
            grid=(bsz,),
            in_specs=([pl.BlockSpec(memory_space=pltpu.SMEM), tok, tok, tok]
                      + [page_spec(p) for p in range(n_pages)] * 2),
            out_specs=tok),
        out_shape=jax.ShapeDtypeStruct((bsz, SB_ROWS, W_Q), F32),
        compiler_params=_cparams("parallel"),
        name="sb_sample",
    )(page_table, bias, q8, k8, v8, *([cache_k] * n_pages), *([cache_v] * n_pages))


def _gmlp_prompt_kernel(u_ref, v_ref, ws_ref, bs_ref, g_ref, b_ref, o_ref):
    c = GMLP_CHUNK
    ones = _group_ones(W_Q)
    tril = _iota((c, c), 0) >= _iota((c, c), 1)
    lane_group = _iota((c, W_Q), 1) >> 6
    w = [jnp.where(tril, ws_ref[g], 0.0).astype(BF16) for g in range(G_B)]
    for n in range(u_ref.shape[0] // c):
        rows = slice(n * c, (n + 1) * c)
        u = _gelu_tanh(u_ref[rows, :])
        vn = _group_layer_norm(_gelu_tanh(v_ref[rows, :]), ones, g_ref[...], b_ref[...])
        vnb = vn.astype(BF16)
        mixed = bs_ref[...]
        for g in range(G_B):
            vg = jnp.where(lane_group == g, vnb, jnp.zeros_like(vnb))
            mixed = mixed + jnp.dot(w[g], vg, preferred_element_type=F32)
        o_ref[rows, :] = u * mixed


def _gmlp_prompt(ub, vb, ws, bs_lanes, g, b, tm):
    n = ub.shape[0]
    c = GMLP_CHUNK
    tile = pl.BlockSpec((tm, W_Q), lambda i: (i, 0))
    vec = pl.BlockSpec((1, W_Q), lambda i: (0, 0))
    return pl.pallas_call(
        _gmlp_prompt_kernel,
        grid=(n // tm,),
        in_specs=[tile, tile,
                  pl.BlockSpec((G_B, c, c), lambda i: (0, 0, 0)),
                  pl.BlockSpec((c, W_Q), lambda i: (0, 0)),
                  vec, vec],
        out_specs=tile,
        out_shape=jax.ShapeDtypeStruct((n, W_Q), F32),
        compiler_params=_cparams("parallel"),
        name="gmlp_prompt",
    )(ub, vb, ws, bs_lanes, g, b)


HALO = 16


def _pool_windows(load, t, start_pos):
    lane = _iota((t, SB_BLOCK), 1)
    pos = start_pos + _iota((t, SB_BLOCK), 0)
    halves = []
    for half, (w_small, w_big) in enumerate(((POOL_WINDOWS[0], POOL_WINDOWS[1]),
                                              (POOL_WINDOWS[2], POOL_WINDOWS[3]))):
        s = load(0, half)
        for i in range(1, w_small):
            s = s + load(i, half)
        big = s
        for i in range(w_small, w_big):
            big = big + load(i, half)
        small_lane = lane < GROUP_LANES
        win = jnp.where(small_lane, s, big)
        w = jnp.where(small_lane, w_small, w_big)
        cnt = jnp.minimum(pos + 1, w).astype(F32)
        halves.append(win / cnt)
    return jnp.concatenate(halves, axis=1)


def _pool_prompt_kernel(tiles_per_seq, x_ref, halo_ref, w_ref, gamma_ref, o_ref, xx_ref):
    i = pl.program_id(0)
    t = x_ref.shape[0]
    first = (i % tiles_per_seq) == 0
    xx_ref[0:HALO, :] = jnp.where(first, 0.0, halo_ref[...])
    xx_ref[HALO:, :] = x_ref[...]

    def load(shift, half):
        return xx_ref[pl.ds(HALO - shift, t), half * SB_BLOCK:(half + 1) * SB_BLOCK]

    start = (i % tiles_per_seq) * t
    means = _pool_windows(load, t, start)
    d = means - x_ref[...]
    o_ref[...] = _bdot(d, w_ref[...]) * gamma_ref[...]


def _pool_prompt(xc, w_bd, gamma, seq, tm):
    n = xc.shape[0]
    hb = tm // HALO
    return pl.pallas_call(
        functools.partial(_pool_prompt_kernel, seq // tm),
        grid=(n // tm,),
        in_specs=[pl.BlockSpec((tm, W_Q), lambda i: (i, 0)),
                  pl.BlockSpec((HALO, W_Q), lambda i: (jnp.maximum(i * hb - 1, 0), 0)),
                  pl.BlockSpec((W_Q, W_Q), lambda i: (0, 0)),
                  pl.BlockSpec((1, W_Q), lambda i: (0, 0))],
        out_specs=pl.BlockSpec((tm, W_Q), lambda i: (i, 0)),
        out_shape=jax.ShapeDtypeStruct((n, W_Q), F32),
        scratch_shapes=[pltpu.VMEM((tm + HALO, W_Q), F32)],
        compiler_params=_cparams("parallel"),
        name="pool_prompt",
    )(xc, xc, w_bd, gamma)


CONV_HALO = 8


def _delta_gates(bd, ad, a_log, dt_bias):
    beta = _sigmoid(bd)
    g = -jnp.exp(a_log) * _softplus(ad + dt_bias)
    return beta, g


def _delta_prep_kernel(tiles_per_seq, x_ref, halo_ref, bd_ref, ad_ref, cw_ref, alog_ref, dt_ref,
                       q_ref, k_ref, v_ref, g_ref, b_ref, xx_ref):
    i = pl.program_id(0)
    t = x_ref.shape[0]
    first = (i % tiles_per_seq) == 0
    xx_ref[0:CONV_HALO, :] = jnp.where(first, 0.0, halo_ref[...])
    xx_ref[CONV_HALO:, :] = x_ref[...]
    ones = _group_ones(W_Q)
    off = CONV_HALO - (CONV_K - 1)
    for part, out in enumerate((q_ref, k_ref, v_ref)):
        lanes = slice(part * W_Q, (part + 1) * W_Q)
        y = xx_ref[pl.ds(off, t), lanes] * cw_ref[0:1, lanes]
        for j in range(1, CONV_K):
            y = y + xx_ref[pl.ds(off + j, t), lanes] * cw_ref[j:j + 1, lanes]
        y = _silu(y)
        out[...] = y if part == 2 else _l2_normalize(y, ones)
    beta, g = _delta_gates(bd_ref[...], ad_ref[...], alog_ref[...], dt_ref[...])
    b_ref[...] = beta
    g_ref[...] = g


def _delta_prep(qkv, bd, ad, conv_w, a_log, dt_bias, seq, tm):
    n, wd = qkv.shape
    hb = tm // CONV_HALO
    tile = pl.BlockSpec((tm, W_Q), lambda i: (i, 0))
    vec = pl.BlockSpec((1, W_Q), lambda i: (0, 0))
    return pl.pallas_call(
        functools.partial(_delta_prep_kernel, seq // tm),
        grid=(n // tm,),
        in_specs=[pl.BlockSpec((tm, wd), lambda i: (i, 0)),
                  pl.BlockSpec((CONV_HALO, wd), lambda i: (jnp.maximum(i * hb - 1, 0), 0)),
                  tile, tile,
                  pl.BlockSpec((CONV_K, wd), lambda i: (0, 0)),
                  vec, vec],
        out_specs=[tile] * 5,
        out_shape=[jax.ShapeDtypeStruct((n, W_Q), F32)] * 5,
        scratch_shapes=[pltpu.VMEM((tm + CONV_HALO, wd), F32)],
        compiler_params=_cparams("parallel"),
        name="delta_prep",
    )(qkv, qkv, bd, ad, conv_w, a_log, dt_bias)


def _delta_kernel(q_ref, k_ref, v_ref, g_ref, b_ref, zg_ref, s0_ref, gn_ref, o_ref, sf_ref, s_ref):
    c = pl.program_id(1)
    n_chunks = pl.num_programs(1)
    cs = DELTA_CHUNK
    hw = H_D * DK_D
    row = _iota((cs, hw), 0)
    col = _iota((cs, hw), 1) & (DK_D - 1)
    incl = row >= col
    strict = row > col
    bdmask = (_iota((hw, hw), 0) >> 6) == (_iota((hw, hw), 1) >> 6)
    ones = _group_ones(hw)
    tril = jnp.where(_iota((cs, cs), 0) >= _iota((cs, cs), 1), 1.0, 0.0).astype(BF16)

    def bd(x):
        return jnp.where(bdmask, jnp.concatenate([x] * H_D, axis=0), 0.0)

    @pl.when(c == 0)
    def _():
        s_ref[...] = jnp.where(bdmask, jnp.concatenate([s0_ref[0]] * H_D, axis=1), 0.0)

    q = q_ref[0] * (DK_D ** -0.5)
    k = k_ref[0]
    v = v_ref[0]
    g = g_ref[0]
    beta = b_ref[0]
    gc = _dot_exact_lhs(tril, g)
    diff = _dot_exact_lhs(tril, jnp.where(strict, g, 0.0))
    decay = jnp.where(incl, jnp.exp(jnp.where(incl, diff, 0.0)), 0.0)
    eg = jnp.exp(gc)
    kb = k * beta
    vb = v * beta
    kbd = bd(k).astype(BF16)
    neg_lower = jnp.where(strict, -(_bdot_nt(kb, kbd) * decay), 0.0)
    t_inv = jnp.where(row == col, 1.0, 0.0) + neg_lower
    p = neg_lower
    for _ in range(5):
        p = _dot3(p, bd(p))
        t_inv = t_inv + _dot3(t_inv, bd(p))
    u = _bdot(t_inv, bd(vb))
    w = _bdot(t_inv, bd(kb * eg))
    intra = jnp.where(incl, _bdot_nt(q, kbd) * decay, 0.0)
    qg = q * eg
    gc_last = gc[cs - 1:cs, :]
    kg = k * jnp.exp(gc_last - gc)
    s = s_ref[...]
    v_new = u - _bdot(w, s)
    o = _bdot(qg, s) + _bdot(intra, bd(v_new))
    s_new = s * jnp.exp(gc_last) + jnp.where(bdmask, _bdot_tn(kg, v_new), 0.0)
    s_ref[...] = s_new
    o = o * lax.rsqrt(_group_sum(o * o, ones) * (1.0 / DK_D) + 1e-6) * gn_ref[...]
    o_ref[0] = o * _silu(zg_ref[0])

    @pl.when(c == n_chunks - 1)
    def _():
        sf = s_new[:, 0:DK_D]
        for h in range(1, H_D):
            sf = sf + s_new[:, h * DK_D:(h + 1) * DK_D]
        sf_ref[0] = sf


def _delta(q, k, v, g, beta, zg, s0, gn):
    bsz, t, hw = q.shape
    tok = pl.BlockSpec((1, DELTA_CHUNK, hw), lambda b, c: (b, c, 0))
    st = pl.BlockSpec((1, hw, DK_D), lambda b, c: (b, 0, 0))
    return pl.pallas_call(
        _delta_kernel,
        grid=(bsz, t // DELTA_CHUNK),
        in_specs=[tok] * 6 + [st, pl.BlockSpec((1, hw), lambda b, c: (0, 0))],
        out_specs=[tok, st],
        out_shape=[jax.ShapeDtypeStruct((bsz, t, hw), F32),
                   jax.ShapeDtypeStruct((bsz, hw, DK_D), F32)],
        scratch_shapes=[pltpu.VMEM((hw, hw), F32)],
        compiler_params=_cparams("parallel", "arbitrary"),
        name="delta",
    )(q, k, v, g, beta, zg, s0, gn)


def _sample_mix_kernel(start_pos, u_ref, v_ref, xc_ref, qkv_ref, bd_ref, ad_ref, pool_prev_ref,
                       conv_prev_ref, wl_ref, bsl_ref, lng_ref, lnb_ref, pw_ref, gamma_ref, cw_ref,
                       alog_ref, dt_ref,
                       ob_ref, vn_ref, oc_ref, pool_new_ref, conv_new_ref, q_ref, k_ref, vd_ref,
                       g_ref, b_ref):
    t_new = u_ref.shape[0]
    bsz = u_ref.shape[1]
    ones = _group_ones(W_Q)
    vn = [_group_layer_norm(_gelu_tanh(v_ref[t]), ones, lng_ref[...], lnb_ref[...])
          for t in range(t_new)]
    for t in range(t_new):
        vn_ref[t] = vn[t]
        mixed = bsl_ref[t:t + 1, :] + wl_ref[t, 0:1, :] * vn[0]
        for s in range(1, t + 1):
            mixed = mixed + wl_ref[t, s:s + 1, :] * vn[s]
        ob_ref[t] = _gelu_tanh(u_ref[t]) * mixed
    n_prev = pool_prev_ref.shape[0]

    def pool_row(i):
        return pool_prev_ref[i] if i < n_prev else xc_ref[i - n_prev]

    lane = _iota((bsz, W_Q), 1) >> 6
    for t in range(t_new):
        acc = pool_row(n_prev + t)
        win = None
        done = 1
        for gi, wdw in enumerate(POOL_WINDOWS):
            for i in range(done, wdw):
                acc = acc + pool_row(n_prev + t - i)
            done = wdw
            cnt = float(min(start_pos + t + 1, wdw))
            mean = acc / cnt
            win = mean if win is None else jnp.where(lane >= gi, mean, win)
        d = win - xc_ref[t]
        oc_ref[t] = _bdot(d, pw_ref[...]) * gamma_ref[...]
    for i in range(n_prev):
        pool_new_ref[i] = pool_row(t_new + i)
    c_prev = conv_prev_ref.shape[0]

    def conv_row(i):
        return conv_prev_ref[i] if i < c_prev else qkv_ref[i - c_prev]

    for t in range(t_new):
        y = conv_row(t) * cw_ref[0:1, :]
        for j in range(1, CONV_K):
            y = y + conv_row(t + j) * cw_ref[j:j + 1, :]
        y = _silu(y)
        q_ref[t] = _l2_normalize(y[:, 0:W_Q], ones)
        k_ref[t] = _l2_normalize(y[:, W_Q:2 * W_Q], ones)
        vd_ref[t] = y[:, 2 * W_Q:3 * W_Q]
        beta, g = _delta_gates(bd_ref[t], ad_ref[t], alog_ref[...], dt_ref[...])
        b_ref[t] = beta
        g_ref[t] = g
    for i in range(c_prev):
        conv_new_ref[i] = conv_row(t_new + i)


def _sample_mix(start_pos, ub, vb, xc, qkv, bd, ad, pool_prev, conv_prev, wl, bsl, lng, lnb, pw,
                gamma, cw, a_log, dt_bias):
    t_new, bsz, _ = ub.shape
    slab = jax.ShapeDtypeStruct((t_new, bsz, W_Q), F32)
    out_shape = [slab, slab, slab,
                 jax.ShapeDtypeStruct(pool_prev.shape, F32),
                 jax.ShapeDtypeStruct(conv_prev.shape, F32),
                 slab, slab, slab, slab, slab]
    return pl.pallas_call(
        functools.partial(_sample_mix_kernel, start_pos),
        out_shape=out_shape,
        compiler_params=pltpu.CompilerParams(vmem_limit_bytes=VMEM_LIMIT),
        name="sample_mix",
    )(ub, vb, xc, qkv, bd, ad, pool_prev, conv_prev, wl, bsl, lng, lnb, pw, gamma, cw, a_log,
      dt_bias)


def _mm_ln_kernel(n_in, *refs):
    xs = refs[:n_in]
    ws = refs[n_in:2 * n_in]
    h_ref, g_ref, b_ref, o_ref = refs[2 * n_in:]
    acc = ALPHA * h_ref[...]
    for x_ref, w_ref in zip(xs, ws):
        acc = acc + _bdot(x_ref[...], w_ref[...])
    o_ref[...] = _layer_norm(acc, g_ref[...], b_ref[...])


def _mm_ln(xs, ws, h, g, b, tm):
    n, d = h.shape
    n_in = len(xs)
    return pl.pallas_call(
        functools.partial(_mm_ln_kernel, n_in),
        grid=(n // tm,),
        in_specs=([pl.BlockSpec((tm, x.shape[1]), lambda i: (i, 0)) for x in xs]
                  + [pl.BlockSpec(w.shape, lambda i: (0, 0)) for w in ws]
                  + [pl.BlockSpec((tm, d), lambda i: (i, 0)),
                     pl.BlockSpec((1, d), lambda i: (0, 0)),
                     pl.BlockSpec((1, d), lambda i: (0, 0))]),
        out_specs=pl.BlockSpec((tm, d), lambda i: (i, 0)),
        out_shape=jax.ShapeDtypeStruct((n, d), F32),
        compiler_params=_cparams("parallel"),
        name="mm_ln",
    )(*xs, *ws, h, g.reshape(1, d), b.reshape(1, d))


def _mm_kernel(scale, x_ref, w_ref, o_ref, ob_ref):
    y = _bdot(x_ref[...], w_ref[...])
    o_ref[...] = y
    ob_ref[...] = (y * scale).astype(BF16)


def _mm(x, w, tm, scale=1.0):
    n, d = x.shape
    m = w.shape[1]
    return pl.pallas_call(
        functools.partial(_mm_kernel, scale),
        grid=(n // tm,),
        in_specs=[pl.BlockSpec((tm, d), lambda i: (i, 0)),
                  pl.BlockSpec((d, m), lambda i: (0, 0))],
        out_specs=[pl.BlockSpec((tm, m), lambda i: (i, 0))] * 2,
        out_shape=[jax.ShapeDtypeStruct((n, m), F32), jax.ShapeDtypeStruct((n, m), BF16)],
        compiler_params=_cparams("parallel"),
        name="mm",
    )(x, w)


def _xattn_heads(q, k_ref, v_ref):
    dh = q.shape[1] // XA_HEADS
    outs = []
    for h in range(XA_HEADS):
        lanes = slice(h * dh, (h + 1) * dh)
        s = _bdot_nt(q[:, lanes], k_ref[0, :, lanes])
        s = s - jnp.max(s, axis=-1, keepdims=True)
        e = jnp.exp(s)
        p = e / jnp.sum(e, axis=-1, keepdims=True)
        outs.append(_bdot(p, v_ref[0, :, lanes]))
    return jnp.concatenate(outs, axis=1)


def _xattn_prompt_kernel(h_ref, wq_ref, wo_ref, k_ref, v_ref, g_ref, b_ref, o_ref):
    h = h_ref[...]
    dh = h.shape[1] // XA_HEADS
    q = (_bdot(h, wq_ref[...]) * (dh ** -0.5)).astype(BF16)
    o = _xattn_heads(q, k_ref, v_ref)
    o_ref[...] = _layer_norm(ALPHA * h + _bdot(o, wo_ref[...]), g_ref[...], b_ref[...])


def _xattn_prompt(h, wq, wo, mem_k, mem_v, g, b, seq, tm):
    n, d = h.shape
    n_mem = mem_k.shape[1]
    tps = seq // tm
    full = pl.BlockSpec((d, d), lambda i: (0, 0))
    mem = pl.BlockSpec((1, n_mem, d), lambda i: (i // tps, 0, 0))
    vec = pl.BlockSpec((1, d), lambda i: (0, 0))
    return pl.pallas_call(
        _xattn_prompt_kernel,
        grid=(n // tm,),
        in_specs=[pl.BlockSpec((tm, d), lambda i: (i, 0)), full, full, mem, mem, vec, vec],
        out_specs=pl.BlockSpec((tm, d), lambda i: (i, 0)),
        out_shape=jax.ShapeDtypeStruct((n, d), F32),
        compiler_params=_cparams("parallel"),
        name="xattn_prompt",
    )(h, wq, wo, mem_k, mem_v, g.reshape(1, d), b.reshape(1, d))


def _xattn_sample_kernel(q_ref, k_ref, v_ref, o_ref):
    o_ref[0] = _xattn_heads(q_ref[0], k_ref, v_ref)


def _xattn_sample(q8, mem_k, mem_v, layer):
    bsz, rows, d = q8.shape
    n_mem = mem_k.shape[1]
    tok = pl.BlockSpec((1, rows, d), lambda b: (b, 0, 0))
    mem = pl.BlockSpec((1, n_mem, d), lambda b: (layer * bsz + b, 0, 0))
    return pl.pallas_call(
        _xattn_sample_kernel,
        grid=(bsz,),
        in_specs=[tok, mem, mem],
        out_specs=tok,
        out_shape=jax.ShapeDtypeStruct((bsz, rows, d), F32),
        compiler_params=_cparams("parallel"),
        name="xattn_sample",
    )(q8, mem_k, mem_v)


def _top2_sum(a, b, c, d):
    m1, n1 = jnp.maximum(a, b), jnp.minimum(a, b)
    m2, n2 = jnp.maximum(c, d), jnp.minimum(c, d)
    return jnp.maximum(m1, m2) + jnp.maximum(jnp.minimum(m1, m2), jnp.maximum(n1, n2))


def _router_kernel(x_ref, wt_ref, bias_ref, o_ref):
    x = x_ref[...]
    wt = wt_ref[...]
    xh, xl = _split(x)
    wh, wl = _split(wt)
    nt = (((1,), (1,)), ((), ()))
    logits = (lax.dot_general(wh, xh, nt, preferred_element_type=F32)
              + lax.dot_general(wh, xl, nt, preferred_element_type=F32)
              + lax.dot_general(wl, xh, nt, preferred_element_type=F32))
    s_all = _sigmoid(logits)
    sel_all = s_all + bias_ref[...]
    s = [s_all[e:e + 1, :] for e in range(N_EXPERTS)]
    sel = [sel_all[e:e + 1, :] for e in range(N_EXPERTS)]
    epg = EXPERTS_PER_GROUP
    score = [_top2_sum(*sel[g * epg:(g + 1) * epg]) for g in range(N_GROUPS)]
    best = functools.reduce(jnp.maximum, score)
    taken = None
    in_group = []
    for g in range(N_GROUPS):
        hit = score[g] == best
        if taken is None:
            pick, taken = hit, hit
        else:
            pick = hit & jnp.logical_not(taken)
            taken = taken | hit
        in_group.append(pick)
    chosen = []
    for g in range(N_GROUPS):
        vals = sel[g * epg:(g + 1) * epg]
        for j in range(epg):
            ahead = jnp.zeros_like(vals[j])
            for i in range(epg):
                if i == j:
                    continue
                before = (vals[i] >= vals[j]) if i < j else (vals[i] > vals[j])
                ahead = ahead + jnp.where(before, 1.0, 0.0)
            chosen.append(in_group[g] & (ahead < 1.5))
    gates = [jnp.where(chosen[e], s[e], 0.0) for e in range(N_EXPERTS)]
    total = functools.reduce(lambda a, b: a + b, gates)
    o_ref[...] = jnp.concatenate(gates, axis=0) / total


def _router(x, w_router_t, bias, tm):
    n, d = x.shape
    e = w_router_t.shape[0]
    return pl.pallas_call(
        _router_kernel,
        grid=(n // tm,),
        in_specs=[pl.BlockSpec((tm, d), lambda i: (i, 0)),
                  pl.BlockSpec((e, d), lambda i: (0, 0)),
                  pl.BlockSpec((e, 1), lambda i: (0, 0))],
        out_specs=pl.BlockSpec((e, tm), lambda i: (0, i)),
        out_shape=jax.ShapeDtypeStruct((e, n), F32),
        compiler_params=_cparams("parallel"),
        name="router",
    )(x, w_router_t, bias.reshape(e, 1))


def _moe_kernel(x_ref, comb_ref, wg_ref, wu_ref, wd_ref, g_ref, b_ref, o_ref, xb_ref, acc_ref):
    e = pl.program_id(1)

    @pl.when(e == 0)
    def _():
        xb_ref[...] = x_ref[...].astype(BF16)
        acc_ref[...] = jnp.zeros_like(acc_ref)

    xb = xb_ref[...]
    hg = jnp.dot(xb, wg_ref[0], preferred_element_type=F32)
    hu = jnp.dot(xb, wu_ref[0], preferred_element_type=F32)
    comb = comb_ref[...]
    gate = jnp.sum(jnp.where(_iota(comb.shape, 1) == e, comb, 0.0), axis=1, keepdims=True)
    act = _silu(hg) * hu * gate
    acc_ref[...] += _bdot(act, wd_ref[0])

    @pl.when(e == pl.num_programs(1) - 1)
    def _():
        o_ref[...] = _layer_norm(ALPHA * x_ref[...] + acc_ref[...], g_ref[...], b_ref[...])


def _moe(x, comb, wg, wu, wd, g, b, tm):
    n, d = x.shape
    ne, _, f = wg.shape
    return pl.pallas_call(
        _moe_kernel,
        grid=(n // tm, ne),
        in_specs=[pl.BlockSpec((tm, d), lambda i, e: (i, 0)),
                  pl.BlockSpec((tm, ne), lambda i, e: (i, 0)),
                  pl.BlockSpec((1, d, f), lambda i, e: (e, 0, 0)),
                  pl.BlockSpec((1, d, f), lambda i, e: (e, 0, 0)),
                  pl.BlockSpec((1, f, d), lambda i, e: (e, 0, 0)),
                  pl.BlockSpec((1, d), lambda i, e: (0, 0)),
                  pl.BlockSpec((1, d), lambda i, e: (0, 0))],
        out_specs=pl.BlockSpec((tm, d), lambda i, e: (i, 0)),
        out_shape=jax.ShapeDtypeStruct((n, d), F32),
        scratch_shapes=[pltpu.VMEM((tm, d), BF16), pltpu.VMEM((tm, d), F32)],
        compiler_params=_cparams("parallel", "arbitrary"),
        name="moe",
    )(x, comb, wg, wu, wd, g.reshape(1, d), b.reshape(1, d))


def _lanes(x):
    if x.ndim == 1:
        x = jnp.repeat(x, GROUP_LANES)
    return x.reshape(1, W_Q).astype(F32)


def _block_diag(w):
    g, a, b = w.shape
    eye = jnp.eye(g, dtype=w.dtype)
    return (eye[:, None, :, None] * w[:, :, None, :]).reshape(g * a, g * b)


def _tile(n, pref):
    t = min(n, pref)
    while n % t:
        t //= 2
    return t


def kernel(x_prompt, x_sample, cache_sb_k, cache_sb_v, state_pool, state_conv, state_delta,
           cache_mem_k, cache_mem_v, page_table, mem_prompt,
           ln0_g, ln0_b, w_in, sb_bias, gmlp_ln_g, gmlp_ln_b, gmlp_ws, gmlp_bs, pool_w, pool_gamma,
           conv_w, dn_a_log, dn_dt_bias, dn_norm_g, w_out, ln1_g, ln1_b,
           xa_wq, xa_wk, xa_wv, xa_wo, ln2_g, ln2_b, w_router, router_bias,
           ex_wg, ex_wu, ex_wd, ln3_g, ln3_b):
    bp, seq, d = x_prompt.shape
    bs, t_new, _ = x_sample.shape
    n_p = bp * seq
    n_s = bs * t_new
    n_mem = mem_prompt.shape[1]
    n_phys, page = cache_sb_k.shape[1], cache_sb_k.shape[2]
    past_len = page_table.shape[1] * page
    xa_dh = d // XA_HEADS
    tm_p = _tile(seq, 512)
    tm_s = _tile(n_s, 512)
    tm_moe_p = _tile(n_p, 1024)
    chunk_pad = DELTA_CHUNK - t_new

    cache_k = cache_sb_k.reshape(DEPTH * n_phys, page, W_Q)
    cache_v = cache_sb_v.reshape(DEPTH * n_phys, page, W_Q)
    memk_s = cache_mem_k.reshape(DEPTH * bs, n_mem, d)
    memv_s = cache_mem_v.reshape(DEPTH * bs, n_mem, d)
    w_router_t = w_router.T
    mem_flat = mem_prompt.reshape(bp * n_mem, d)

    def rows8(x):
        x = x.reshape(bs, t_new, -1)
        return jnp.pad(x, ((0, 0), (0, SB_ROWS - t_new), (0, 0)))

    def tmajor(x):
        return x.reshape(bs, t_new, -1).transpose(1, 0, 2)

    def bmajor(x):
        return x.transpose(1, 0, 2)

    h_p = _ln(x_prompt.reshape(n_p, d), ln0_g, ln0_b, tm_p)
    h_s = _ln(x_sample.reshape(n_s, d), ln0_g, ln0_b, tm_s)

    outs = {k: [] for k in ("kp", "vp", "ks", "vs", "gv", "pp", "ps", "cp", "cs", "dp", "ds",
                            "mk", "mv")}
    for l in range(DEPTH):
        w_in_l = _prep_w_in(w_in[l])
        w_out_l = w_out[l].astype(BF16)
        w_out_parts = [w_out_l[i * W_Q:(i + 1) * W_Q] for i in range(4)]
        ws_l = gmlp_ws[l]
        bs_lanes = jnp.repeat(gmlp_bs[l].T, GROUP_LANES, axis=1)
        lng, lnb = _lanes(gmlp_ln_g[l]), _lanes(gmlp_ln_b[l])
        pw = _block_diag(pool_w[l]).astype(BF16)
        gamma = pool_gamma[l].reshape(1, W_Q)
        cw = conv_w[l]
        a_log, dt_b = _lanes(dn_a_log[l]), _lanes(dn_dt_bias[l])
        gn = jnp.tile(dn_norm_g[l], H_D).reshape(1, W_Q)
        wq, wo = xa_wq[l].astype(BF16), xa_wo[l].astype(BF16)
        wg, wu, wd = ex_wg[l].astype(BF16), ex_wu[l].astype(BF16), ex_wd[l].astype(BF16)

        mk, mk_b = _mm(mem_flat, xa_wk[l].astype(BF16), _tile(bp * n_mem, 256))
        mv, mv_b = _mm(mem_flat, xa_wv[l].astype(BF16), _tile(bp * n_mem, 256))
        outs["mk"].append(mk.reshape(bp, n_mem, XA_HEADS, xa_dh))
        outs["mv"].append(mv.reshape(bp, n_mem, XA_HEADS, xa_dh))

        (q, k, v, k_b, v_b, ub, vb, xc, qkv, zg, bd, ad) = _proj_in(h_p, w_in_l, tm_p)
        outs["kp"].append(k.reshape(bp, seq, H_A, DH_A))
        outs["vp"].append(v.reshape(bp, seq, H_A, DH_A))
        outs["pp"].append(xc.reshape(bp, seq, W_Q)[:, seq - POOL_BUF:])
        outs["cp"].append(qkv.reshape(bp, seq, 3 * W_Q)[:, seq - (CONV_K - 1):])
        oa = _sb_prompt(q, k_b, v_b, sb_bias[l], bp, seq)
        ob = _gmlp_prompt(ub, vb, ws_l, bs_lanes, lng, lnb, tm_p)
        oc = _pool_prompt(xc, pw, gamma, seq, tm_p)
        qd, kd, vd, gd, betad = _delta_prep(qkv, bd, ad, cw, a_log, dt_b, seq, tm_p)
        r3 = lambda x: x.reshape(bp, seq, W_Q)
        od, s_fin = _delta(r3(qd), r3(kd), r3(vd), r3(gd), r3(betad), r3(zg),
                           jnp.zeros((bp, H_D * DK_D, DK_D), F32), gn)
        outs["dp"].append(s_fin.reshape(bp, H_D, DK_D, DK_D))
        h_p = _mm_ln([oa, ob, oc, od.reshape(n_p, W_Q)], w_out_parts, h_p, ln1_g[l], ln1_b[l], tm_p)
        h_p = _xattn_prompt(h_p, wq, wo, mk_b.reshape(bp, n_mem, d), mv_b.reshape(bp, n_mem, d),
                            ln2_g[l], ln2_b[l], seq, tm_p)
        comb = _router(h_p, w_router_t, router_bias, tm_p).T
        h_p = _moe(h_p, comb, wg, wu, wd, ln3_g[l], ln3_b[l], tm_moe_p)

        (q, k, v, _, _, ub, vb, xc, qkv, zg, bd, ad) = _proj_in(h_s, w_in_l, tm_s)
        outs["ks"].append(k.reshape(bs, t_new, H_A, DH_A))
        outs["vs"].append(v.reshape(bs, t_new, H_A, DH_A))
        oa = _sb_sample(rows8(q), rows8(k), rows8(v), cache_k, cache_v, page_table, sb_bias[l],
                        l, n_phys)[:, :t_new].reshape(n_s, W_Q)
        wl = jnp.repeat(jnp.where(jnp.tril(jnp.ones((t_new, t_new), bool)),
                                  ws_l[:, :t_new, :t_new], 0.0).transpose(1, 2, 0),
                        GROUP_LANES, axis=2)
        (ob, vn, oc, pool_new, conv_new, qd, kd, vd, gd, betad) = _sample_mix(
            past_len, tmajor(ub), tmajor(vb), tmajor(xc), tmajor(qkv), tmajor(bd), tmajor(ad),
            bmajor(state_pool[l]), bmajor(state_conv[l]), wl, bs_lanes[:t_new], lng, lnb, pw,
            gamma, cw, a_log, dt_b)
        outs["gv"].append(bmajor(vn))
        outs["ps"].append(bmajor(pool_new))
        outs["cs"].append(bmajor(conv_new))
        padc = lambda x: jnp.pad(bmajor(x), ((0, 0), (0, chunk_pad), (0, 0)))
        zg_pad = jnp.pad(zg.reshape(bs, t_new, W_Q), ((0, 0), (0, chunk_pad), (0, 0)))
        od, s_fin = _delta(padc(qd), padc(kd), padc(vd), padc(gd), padc(betad), zg_pad,
                           state_delta[l].reshape(bs, H_D * DK_D, DK_D), gn)
        outs["ds"].append(s_fin.reshape(bs, H_D, DK_D, DK_D))
        od = od[:, :t_new].reshape(n_s, W_Q)
        flat = lambda x: bmajor(x).reshape(n_s, W_Q)
        h_s = _mm_ln([oa, flat(ob), flat(oc), od], w_out_parts, h_s, ln1_g[l], ln1_b[l], tm_s)
        _, qx = _mm(h_s, wq, tm_s, scale=xa_dh ** -0.5)
        ox = _xattn_sample(rows8(qx), memk_s, memv_s, l)[:, :t_new].reshape(n_s, d)
        h_s = _mm_ln([ox], [wo], h_s, ln2_g[l], ln2_b[l], tm_s)
        comb = _router(h_s, w_router_t, router_bias, tm_s).T
        h_s = _moe(h_s, comb, wg, wu, wd, ln3_g[l], ln3_b[l], tm_s)

    st = lambda key: jnp.stack(outs[key])
    return (h_p.reshape(bp, seq, d), h_s.reshape(bs, t_new, d),
            st("kp"), st("vp"), st("ks"), st("vs"), st("gv"),
            st("pp"), st("ps"), st("cp"), st("cs"), st("dp"), st("ds"),
            st("mk"), st("mv"))
```

```python
import functools

import jax
import jax.numpy as jnp
from jax import lax
from jax.experimental import pallas as pl
from jax.experimental.pallas import tpu as pltpu

F32 = jnp.float32
BF16 = jnp.bfloat16

DEPTH = 2
H_A = 4
DH_A = 64
G_B = 4
GMLP_CHUNK = 128
POOL_WINDOWS = (2, 4, 8, 16)
POOL_BUF = 15
H_D = 4
DK_D = 64
CONV_K = 4
DELTA_CHUNK = 64
XA_HEADS = 4
N_EXPERTS = 16
N_GROUPS = 4
EXPERTS_PER_GROUP = 4
ALPHA = (2 * DEPTH) ** 0.25
LN_EPS = 1e-5
GROUP_LANES = 64
W_Q = 256

SB_BLOCK = 128
VMEM_LIMIT = 48 * 1024 * 1024


def _cparams(*sem):
    return pltpu.CompilerParams(dimension_semantics=sem, vmem_limit_bytes=VMEM_LIMIT)


def _bdot(a, b):
    return jnp.dot(a.astype(BF16), b.astype(BF16), preferred_element_type=F32)


def _bdot_nt(a, b):
    return lax.dot_general(a.astype(BF16), b.astype(BF16), (((1,), (1,)), ((), ())),
                           preferred_element_type=F32)


def _bdot_tn(a, b):
    return lax.dot_general(a.astype(BF16), b.astype(BF16), (((0,), (0,)), ((), ())),
                           preferred_element_type=F32)


def _split(x):
    hi = x.astype(BF16)
    lo = (x - hi.astype(F32)).astype(BF16)
    return hi, lo


def _dot_exact_rhs(x, c):
    hi, lo = _split(x)
    return (jnp.dot(hi, c, preferred_element_type=F32)
            + jnp.dot(lo, c, preferred_element_type=F32))


def _dot_exact_lhs(c, x):
    hi, lo = _split(x)
    return (jnp.dot(c, hi, preferred_element_type=F32)
            + jnp.dot(c, lo, preferred_element_type=F32))


def _dot3(a, b):
    ah, al = _split(a)
    bh, bl = _split(b)
    return (jnp.dot(ah, bh, preferred_element_type=F32)
            + jnp.dot(ah, bl, preferred_element_type=F32)
            + jnp.dot(al, bh, preferred_element_type=F32))


def _iota(shape, dim):
    return lax.broadcasted_iota(jnp.int32, shape, dim)


def _group_ones(n):
    return jnp.where((_iota((n, n), 0) >> 6) == (_iota((n, n), 1) >> 6), 1.0, 0.0).astype(BF16)


def _group_sum(x, ones):
    return _dot_exact_rhs(x, ones)


def _layer_norm(x, g, b):
    mu = jnp.mean(x, axis=-1, keepdims=True)
    xc = x - mu
    var = jnp.mean(xc * xc, axis=-1, keepdims=True)
    return xc * lax.rsqrt(var + LN_EPS) * g + b


def _sigmoid(x):
    return 1.0 / (1.0 + jnp.exp(-x))


def _silu(x):
    return x * _sigmoid(x)


def _softplus(x):
    return jnp.maximum(x, 0.0) + jnp.log(1.0 + jnp.exp(-jnp.abs(x)))


def _gelu_tanh(x):
    return 0.5 * x * (1.0 + jnp.tanh(0.7978845608028654 * (x + 0.044715 * (x * x * x))))


def _group_layer_norm(x, ones, g, b):
    mu = _group_sum(x, ones) * (1.0 / GROUP_LANES)
    xc = x - mu
    var = _group_sum(xc * xc, ones) * (1.0 / GROUP_LANES)
    return xc * lax.rsqrt(var + LN_EPS) * g + b


def _l2_normalize(x, ones):
    return x * lax.rsqrt(_group_sum(x * x, ones) + 1e-6)


def _ln_kernel(x_ref, g_ref, b_ref, o_ref):
    o_ref[...] = _layer_norm(x_ref[...], g_ref[...], b_ref[...])


def _ln(x, g, b, tm):
    n, d = x.shape
    return pl.pallas_call(
        _ln_kernel,
        grid=(n // tm,),
        in_specs=[pl.BlockSpec((tm, d), lambda i: (i, 0)),
                  pl.BlockSpec((1, d), lambda i: (0, 0)),
                  pl.BlockSpec((1, d), lambda i: (0, 0))],
        out_specs=pl.BlockSpec((tm, d), lambda i: (i, 0)),
        out_shape=jax.ShapeDtypeStruct((n, d), F32),
        compiler_params=_cparams("parallel"),
        name="ln0",
    )(x, g.reshape(1, d), b.reshape(1, d))


_SEG = {"q": (0, 256), "k": (256, 512), "v": (512, 768), "ub": (768, 1024), "vb": (1024, 1280),
        "xc": (1280, 1536), "qkv": (1536, 2304), "zg": (2304, 2560), "bd": (2560, 2816),
        "ad": (2816, 3072)}
N_IN_PAD = 3072


def _proj_in_kernel(x_ref, w_ref, q_ref, k_ref, v_ref, kb_ref, vb_ref, ub_ref, vbb_ref, xc_ref,
                    qkv_ref, zg_ref, bd_ref, ad_ref):
    x = x_ref[...].astype(BF16)

    def seg(name):
        a, b = _SEG[name]
        return jnp.dot(x, w_ref[:, a:b], preferred_element_type=F32)

    first_head = _iota((x.shape[0], SB_BLOCK), 1) < GROUP_LANES

    def head_stacked(y):
        yb = y.astype(BF16)
        zero = jnp.zeros((x.shape[0], SB_BLOCK), BF16)
        parts = []
        for p in range(H_A // 2):
            yp = yb[:, p * SB_BLOCK:(p + 1) * SB_BLOCK]
            parts += [jnp.where(first_head, yp, zero), jnp.where(first_head, zero, yp)]
        return jnp.concatenate(parts, axis=1)

    q_ref[...] = (seg("q") * (DH_A ** -0.5)).astype(BF16)
    k = seg("k")
    k_ref[...] = k
    kb_ref[...] = head_stacked(k)
    v = seg("v")
    v_ref[...] = v
    vb_ref[...] = head_stacked(v)
    ub_ref[...] = seg("ub")
    vbb_ref[...] = seg("vb")
    xc_ref[...] = seg("xc")
    qkv_ref[...] = seg("qkv")
    zg_ref[...] = seg("zg")
    bd_ref[...] = seg("bd")
    ad_ref[...] = seg("ad")


def _proj_in(x, w, tm):
    n, d = x.shape
    widths = [(256, BF16), (256, F32), (256, F32), (512, BF16), (512, BF16), (256, F32), (256, F32),
              (256, F32), (768, F32), (256, F32), (256, F32), (256, F32)]
    return pl.pallas_call(
        _proj_in_kernel,
        grid=(n // tm,),
        in_specs=[pl.BlockSpec((tm, d), lambda i: (i, 0)),
                  pl.BlockSpec((d, N_IN_PAD), lambda i: (0, 0))],
        out_specs=[pl.BlockSpec((tm, wd), lambda i: (i, 0)) for wd, _ in widths],
        out_shape=[jax.ShapeDtypeStruct((n, wd), dt) for wd, dt in widths],
        compiler_params=_cparams("parallel"),
        name="proj_in",
    )(x, w)


def _prep_w_in(w_in_l):
    main = w_in_l[:, :2560]
    bd = jnp.repeat(w_in_l[:, 2560:2564], GROUP_LANES, axis=1)
    ad = jnp.repeat(w_in_l[:, 2564:2568], GROUP_LANES, axis=1)
    return jnp.concatenate([main, bd, ad], axis=1).astype(BF16)


NEG_BIG = -1e30


LOG2E = 1.4426950408889634


def _pair_rows(x, p):
    w = 2 * SB_BLOCK
    return jnp.concatenate([x[:, p * w:p * w + SB_BLOCK], x[:, p * w + SB_BLOCK:(p + 1) * w]], axis=0)


def _suffix_ones_hilo():
    nb = SB_BLOCK
    r = lax.broadcasted_iota(jnp.int32, (2 * nb, 2 * nb), 0) & (nb - 1)
    c = lax.broadcasted_iota(jnp.int32, (2 * nb, 2 * nb), 1)
    return jnp.where((c >= nb) | (r > c), 1.0, 0.0).astype(BF16)


def _sb_prompt_kernel(bias_ref, q_ref, k_ref, v_ref, uo_ref, o_ref, carry_ref, acc_ref,
                      z_a, att_a, z_b, att_b):
    i = pl.program_id(1)
    nb = SB_BLOCK
    pw = 2 * nb
    head_of_lane = _iota((1, H_A * nb), 1) >> 7
    bias_row = jnp.zeros((1, H_A * nb), F32)
    for h in range(H_A):
        bias_row = jnp.where(head_of_lane == h, bias_ref[h], bias_row)
    col_minus_row = _iota((nb, nb), 1) - _iota((nb, nb), 0)

    acc_ref[...] = jnp.zeros_like(acc_ref)
    carry_ref[...] = jnp.zeros_like(carry_ref)
    z_b[...] = jnp.zeros_like(z_b)
    att_b[...] = jnp.zeros_like(att_b)

    def stages(t, z_w, att_w, z_r, att_r, masked):
        ja = jnp.maximum(i - t, 0)
        k = k_ref[pl.ds(pl.multiple_of(ja * nb, nb), nb), :]
        for p in range(2):
            z_w[:, p * pw:(p + 1) * pw] = lax.dot_general(
                q_ref[:, p * nb:(p + 1) * nb], _pair_rows(k, p), (((1,), (1,)), ((), ())),
                preferred_element_type=F32)
        z = z_r[...] + bias_row
        l = jnp.log(1.0 + jnp.exp2(jnp.abs(z) * (-LOG2E)))
        nlk = jnp.maximum(z, 0.0) + l
        lb = z - nlk
        if masked:
            tb = t - 1
            span = jnp.where((tb >= 0) & (tb <= i), tb * nb, -2 * nb)
            valid = col_minus_row < span
            nlk = nlk * jnp.concatenate([jnp.where(valid, 1.0, 0.0)] * H_A, axis=1)
            lb = lb + jnp.concatenate([jnp.where(valid, 0.0, NEG_BIG)] * H_A, axis=1)
        hi, lo = _split(nlk)
        res = [jnp.dot(jnp.concatenate([hi[:, h * nb:(h + 1) * nb], lo[:, h * nb:(h + 1) * nb]],
                                       axis=1), uo_ref[...], preferred_element_type=F32)
               for h in range(H_A)]
        right = jnp.concatenate([x[:, :nb] for x in res], axis=1)
        total = jnp.concatenate([x[:, nb:] for x in res], axis=1)
        carry = carry_ref[...]
        att_w[...] = jnp.exp(lb - right - carry).astype(BF16)
        carry_ref[...] = carry + total
        jc = jnp.clip(i - t + 2, 0, i)
        v = v_ref[pl.ds(pl.multiple_of(jc * nb, nb), nb), :]
        for p in range(2):
            acc_ref[:, p * nb:(p + 1) * nb] += jnp.dot(att_r[:, p * pw:(p + 1) * pw],
                                                       _pair_rows(v, p),
                                                       preferred_element_type=F32)

    def four_trips(masked, u, carry_unused):
        stages(4 * u, z_a, att_a, z_b, att_b, masked)
        stages(4 * u + 1, z_b, att_b, z_a, att_a, masked)
        stages(4 * u + 2, z_a, att_a, z_b, att_b, masked)
        stages(4 * u + 3, z_b, att_b, z_a, att_a, masked)
        return carry_unused

    four_trips(True, 0, 0)
    n_plain = jnp.maximum((i - 2) // 4, 0)
    lax.fori_loop(1, n_plain + 1, functools.partial(four_trips, False), 0)
    lax.fori_loop(n_plain + 1, (i + 6) // 4, functools.partial(four_trips, True), 0)
    o_ref[...] = acc_ref[...]


def _sb_prompt(q, k, v, bias, batch, seq):
    n = q.shape[0]
    nq = seq // SB_BLOCK
    hw = H_A * SB_BLOCK
    stage_bufs = [pltpu.VMEM((SB_BLOCK, hw), F32),
                  pltpu.VMEM((SB_BLOCK, hw), BF16)]
    return pl.pallas_call(
        _sb_prompt_kernel,
        grid=(batch, nq),
        in_specs=[pl.BlockSpec(memory_space=pltpu.SMEM),
                  pl.BlockSpec((SB_BLOCK, W_Q), lambda b, i: (b * nq + i, 0)),
                  pl.BlockSpec((seq, hw), lambda b, i: (b, 0)),
                  pl.BlockSpec((seq, hw), lambda b, i: (b, 0)),
                  pl.BlockSpec((2 * SB_BLOCK, 2 * SB_BLOCK), lambda b, i: (0, 0))],
        out_specs=pl.BlockSpec((SB_BLOCK, W_Q), lambda b, i: (b * nq + i, 0)),
        scratch_shapes=[pltpu.VMEM((SB_BLOCK, hw), F32),
                        pltpu.VMEM((SB_BLOCK, W_Q), F32)] + stage_bufs * 2,
        out_shape=jax.ShapeDtypeStruct((n, W_Q), F32),
        compiler_params=_cparams("parallel", "arbitrary"),
        name="sb_prompt",
    )(bias, q, k, v, _suffix_ones_hilo())


SB_ROWS = 8


def _sb_sample_kernel(n_pages, pt_ref, bias_ref, q_ref, kn_ref, vn_ref, uo_ref, *rest):
    kpages = rest[:n_pages]
    vpages = rest[n_pages:2 * n_pages]
    o_ref = rest[2 * n_pages]
    m = H_A * SB_ROWS
    q8 = q_ref[0]
    qh = [q8[:, h * DH_A:(h + 1) * DH_A] for h in range(H_A)]
    rowh = _iota((m, SB_BLOCK), 0) >> 3
    bias = jnp.zeros((m, SB_BLOCK), F32)
    for h in range(H_A):
        bias = jnp.where(rowh == h, bias_ref[h], bias)
    tri = _iota((m, SB_BLOCK), 1) < (_iota((m, SB_BLOCK), 0) & (SB_ROWS - 1))

    def log_terms(z, mask):
        z = z + bias
        l = jnp.log(1.0 + jnp.exp2(jnp.abs(z) * (-LOG2E)))
        nlk = jnp.maximum(z, 0.0) + l
        lb = z - nlk
        if mask is not None:
            nlk = jnp.where(mask, nlk, 0.0)
            lb = jnp.where(mask, lb, NEG_BIG)
        return nlk, lb

    def suffix_sums(nlk):
        hi, lo = _split(nlk)
        return jnp.dot(jnp.concatenate([hi, lo], axis=1), uo_ref[...], preferred_element_type=F32)

    def head_rows(att, h):
        return att[h * SB_ROWS:(h + 1) * SB_ROWS]

    pad = jnp.zeros((SB_BLOCK - SB_ROWS, W_Q), F32)
    kn = jnp.concatenate([kn_ref[0], pad], axis=0)
    vn = jnp.concatenate([vn_ref[0], pad], axis=0)
    zs = [jnp.concatenate([_bdot(qh[h], kpages[p][h]) for h in range(H_A)], axis=0)
          for p in range(n_pages)]
    zs.append(jnp.concatenate([_bdot_nt(qh[h], kn[:, h * DH_A:(h + 1) * DH_A])
                               for h in range(H_A)], axis=0))
    terms = [log_terms(z, tri if p == n_pages else None) for p, z in enumerate(zs)]
    sums = [suffix_sums(nlk) for nlk, _ in terms]
    carry = jnp.zeros((m, SB_BLOCK), F32)
    atts = [None] * (n_pages + 1)
    for p in range(n_pages, -1, -1):
        atts[p] = jnp.exp(terms[p][1] - sums[p][:, :SB_BLOCK] - carry)
        carry = carry + sums[p][:, SB_BLOCK:]
    acc = [_bdot(head_rows(atts[n_pages], h), vn[:, h * DH_A:(h + 1) * DH_A]) for h in range(H_A)]
    for p in range(n_pages):
        acc = [acc[h] + _bdot_nt(head_rows(atts[p], h), vpages[p][h]) for h in range(H_A)]
    o_ref[0] = jnp.concatenate(acc, axis=1)


def _sb_sample(q8, k8, v8, cache_kt, cache_vt, page_table, bias, layer):
    bsz, n_pages = page_table.shape
    page = cache_kt.shape[4]

    def page_spec(p):
        return pl.BlockSpec((None, None, H_A, DH_A, page), lambda b, pt: (layer, pt[b, p], 0, 0, 0))

    tok = pl.BlockSpec((1, SB_ROWS, W_Q), lambda b, pt: (b, 0, 0))
    return pl.pallas_call(
        functools.partial(_sb_sample_kernel, n_pages),
        grid_spec=pltpu.PrefetchScalarGridSpec(
            num_scalar_prefetch=1,
            grid=(bsz,),
            in_specs=([pl.BlockSpec(memory_space=pltpu.SMEM), tok, tok, tok,
                       pl.BlockSpec((2 * SB_BLOCK, 2 * SB_BLOCK), lambda b, pt: (0, 0))]
                      + [page_spec(p) for p in range(n_pages)] * 2),
            out_specs=tok),
        out_shape=jax.ShapeDtypeStruct((bsz, SB_ROWS, W_Q), F32),
        compiler_params=_cparams("parallel"),
        name="sb_sample",
    )(page_table, bias, q8, k8, v8, _suffix_ones_hilo(), *([cache_kt] * n_pages),
      *([cache_vt] * n_pages))


def _gmlp_prompt_kernel(u_ref, v_ref, ws_ref, bs_ref, g_ref, b_ref, o_ref):
    c = GMLP_CHUNK
    ones = _group_ones(W_Q)
    tril = _iota((c, c), 0) >= _iota((c, c), 1)
    lane_group = _iota((c, W_Q), 1) >> 6
    w = [jnp.where(tril, ws_ref[g], 0.0).astype(BF16) for g in range(G_B)]
    for n in range(u_ref.shape[0] // c):
        rows = slice(n * c, (n + 1) * c)
        u = _gelu_tanh(u_ref[rows, :])
        vn = _group_layer_norm(_gelu_tanh(v_ref[rows, :]), ones, g_ref[...], b_ref[...])
        vnb = vn.astype(BF16)
        mixed = bs_ref[...]
        for g in range(G_B):
            vg = jnp.where(lane_group == g, vnb, jnp.zeros_like(vnb))
            mixed = mixed + jnp.dot(w[g], vg, preferred_element_type=F32)
        o_ref[rows, :] = u * mixed


def _gmlp_prompt(ub, vb, ws, bs_lanes, g, b, tm):
    n = ub.shape[0]
    c = GMLP_CHUNK
    tile = pl.BlockSpec((tm, W_Q), lambda i: (i, 0))
    vec = pl.BlockSpec((1, W_Q), lambda i: (0, 0))
    return pl.pallas_call(
        _gmlp_prompt_kernel,
        grid=(n // tm,),
        in_specs=[tile, tile,
                  pl.BlockSpec((G_B, c, c), lambda i: (0, 0, 0)),
                  pl.BlockSpec((c, W_Q), lambda i: (0, 0)),
                  vec, vec],
        out_specs=tile,
        out_shape=jax.ShapeDtypeStruct((n, W_Q), F32),
        compiler_params=_cparams("parallel"),
        name="gmlp_prompt",
    )(ub, vb, ws, bs_lanes, g, b)


HALO = 16


def _pool_windows(load, t, start_pos):
    lane = _iota((t, SB_BLOCK), 1)
    pos = start_pos + _iota((t, SB_BLOCK), 0)
    halves = []
    for half, (w_small, w_big) in enumerate(((POOL_WINDOWS[0], POOL_WINDOWS[1]),
                                              (POOL_WINDOWS[2], POOL_WINDOWS[3]))):
        s = load(0, half)
        for i in range(1, w_small):
            s = s + load(i, half)
        big = s
        for i in range(w_small, w_big):
            big = big + load(i, half)
        small_lane = lane < GROUP_LANES
        win = jnp.where(small_lane, s, big)
        w = jnp.where(small_lane, w_small, w_big)
        cnt = jnp.minimum(pos + 1, w).astype(F32)
        halves.append(win / cnt)
    return jnp.concatenate(halves, axis=1)


def _pool_prompt_kernel(tiles_per_seq, x_ref, halo_ref, w_ref, gamma_ref, o_ref, xx_ref):
    i = pl.program_id(0)
    t = x_ref.shape[0]
    first = (i % tiles_per_seq) == 0
    xx_ref[0:HALO, :] = jnp.where(first, 0.0, halo_ref[...])
    xx_ref[HALO:, :] = x_ref[...]

    def load(shift, half):
        return xx_ref[pl.ds(HALO - shift, t), half * SB_BLOCK:(half + 1) * SB_BLOCK]

    start = (i % tiles_per_seq) * t
    means = _pool_windows(load, t, start)
    d = means - x_ref[...]
    o_ref[...] = _bdot(d, w_ref[...]) * gamma_ref[...]


def _pool_prompt(xc, w_bd, gamma, seq, tm):
    n = xc.shape[0]
    hb = tm // HALO
    return pl.pallas_call(
        functools.partial(_pool_prompt_kernel, seq // tm),
        grid=(n // tm,),
        in_specs=[pl.BlockSpec((tm, W_Q), lambda i: (i, 0)),
                  pl.BlockSpec((HALO, W_Q), lambda i: (jnp.maximum(i * hb - 1, 0), 0)),
                  pl.BlockSpec((W_Q, W_Q), lambda i: (0, 0)),
                  pl.BlockSpec((1, W_Q), lambda i: (0, 0))],
        out_specs=pl.BlockSpec((tm, W_Q), lambda i: (i, 0)),
        out_shape=jax.ShapeDtypeStruct((n, W_Q), F32),
        scratch_shapes=[pltpu.VMEM((tm + HALO, W_Q), F32)],
        compiler_params=_cparams("parallel"),
        name="pool_prompt",
    )(xc, xc, w_bd, gamma)


CONV_HALO = 8


def _delta_gates(bd, ad, a_log, dt_bias):
    beta = _sigmoid(bd)
    g = -jnp.exp(a_log) * _softplus(ad + dt_bias)
    return beta, g


def _delta_prep_kernel(tiles_per_seq, x_ref, halo_ref, bd_ref, ad_ref, cw_ref, alog_ref, dt_ref,
                       q_ref, k_ref, v_ref, g_ref, b_ref, xx_ref):
    i = pl.program_id(0)
    t = x_ref.shape[0]
    first = (i % tiles_per_seq) == 0
    xx_ref[0:CONV_HALO, :] = jnp.where(first, 0.0, halo_ref[...])
    xx_ref[CONV_HALO:, :] = x_ref[...]
    ones = _group_ones(W_Q)
    off = CONV_HALO - (CONV_K - 1)
    for part, out in enumerate((q_ref, k_ref, v_ref)):
        lanes = slice(part * W_Q, (part + 1) * W_Q)
        y = xx_ref[pl.ds(off, t), lanes] * cw_ref[0:1, lanes]
        for j in range(1, CONV_K):
            y = y + xx_ref[pl.ds(off + j, t), lanes] * cw_ref[j:j + 1, lanes]
        y = _silu(y)
        out[...] = y if part == 2 else _l2_normalize(y, ones)
    beta, g = _delta_gates(bd_ref[...], ad_ref[...], alog_ref[...], dt_ref[...])
    b_ref[...] = beta
    g_ref[...] = g


def _delta_prep(qkv, bd, ad, conv_w, a_log, dt_bias, seq, tm):
    n, wd = qkv.shape
    hb = tm // CONV_HALO
    tile = pl.BlockSpec((tm, W_Q), lambda i: (i, 0))
    vec = pl.BlockSpec((1, W_Q), lambda i: (0, 0))
    return pl.pallas_call(
        functools.partial(_delta_prep_kernel, seq // tm),
        grid=(n // tm,),
        in_specs=[pl.BlockSpec((tm, wd), lambda i: (i, 0)),
                  pl.BlockSpec((CONV_HALO, wd), lambda i: (jnp.maximum(i * hb - 1, 0), 0)),
                  tile, tile,
                  pl.BlockSpec((CONV_K, wd), lambda i: (0, 0)),
                  vec, vec],
        out_specs=[tile] * 5,
        out_shape=[jax.ShapeDtypeStruct((n, W_Q), F32)] * 5,
        scratch_shapes=[pltpu.VMEM((tm + CONV_HALO, wd), F32)],
        compiler_params=_cparams("parallel"),
        name="delta_prep",
    )(qkv, qkv, bd, ad, conv_w, a_log, dt_bias)


def _delta_kernel(n_sub, q_ref, k_ref, v_ref, g_ref, b_ref, zg_ref, s0_ref, gn_ref, o_ref, sf_ref,
                  s_ref):
    c = pl.program_id(1)
    n_steps = pl.num_programs(1)
    cs = DELTA_CHUNK
    hw = H_D * DK_D
    row = _iota((cs, hw), 0)
    col = _iota((cs, hw), 1) & (DK_D - 1)
    incl = row >= col
    strict = row > col
    bdmask = (_iota((hw, hw), 0) >> 6) == (_iota((hw, hw), 1) >> 6)
    ones = _group_ones(hw)
    tril = jnp.where(_iota((cs, cs), 0) >= _iota((cs, cs), 1), 1.0, 0.0).astype(BF16)

    def bd(x):
        return jnp.where(bdmask, jnp.concatenate([x] * H_D, axis=0), 0.0)

    @pl.when(c == 0)
    def _():
        s_ref[...] = jnp.where(bdmask, jnp.concatenate([s0_ref[0]] * H_D, axis=1), 0.0)

    chunks = [slice(n * cs, (n + 1) * cs) for n in range(n_sub)]
    each = lambda f, *xs: [f(*x) for x in zip(*xs)]
    q = [q_ref[0, r, :] * (DK_D ** -0.5) for r in chunks]
    k = [k_ref[0, r, :] for r in chunks]
    g = [g_ref[0, r, :] for r in chunks]
    beta = [b_ref[0, r, :] for r in chunks]
    vb = [v_ref[0, r, :] * bt for r, bt in zip(chunks, beta)]
    kb = each(lambda x, bt: x * bt, k, beta)
    gc = [_dot_exact_lhs(tril, x) for x in g]
    diff = [_dot_exact_lhs(tril, jnp.where(strict, x, 0.0)) for x in g]
    decay = [jnp.where(incl, jnp.exp(jnp.where(incl, x, 0.0)), 0.0) for x in diff]
    eg = [jnp.exp(x) for x in gc]
    kbd = [bd(x).astype(BF16) for x in k]
    neg_lower = each(lambda x, y, dc: jnp.where(strict, -(_bdot_nt(x, y) * dc), 0.0), kb, kbd, decay)
    eye = jnp.where(row == col, 1.0, 0.0)
    t_inv = [eye + x for x in neg_lower]
    p = neg_lower
    for _ in range(5):
        p = [_dot3(x, bd(x)) for x in p]
        t_inv = each(lambda t, x: t + _dot3(t, bd(x)), t_inv, p)
    u = each(lambda t, x: _bdot(t, bd(x)), t_inv, vb)
    w = each(lambda t, x, e: _bdot(t, bd(x * e)), t_inv, kb, eg)
    intra = each(lambda x, y, dc: jnp.where(incl, _bdot_nt(x, y) * dc, 0.0), q, kbd, decay)
    gc_last = [x[cs - 1:cs, :] for x in gc]
    qg = each(lambda x, e: x * e, q, eg)
    kg = each(lambda x, gl, c_: x * jnp.exp(gl - c_), k, gc_last, gc)
    solved = list(zip(u, w, intra, qg, kg, [jnp.exp(x) for x in gc_last]))
    s = s_ref[...]
    for n, (u, w, intra, qg, kg, g_last) in enumerate(solved):
        rows = slice(n * cs, (n + 1) * cs)
        v_new = u - _bdot(w, s)
        o = _bdot(qg, s) + _bdot(intra, bd(v_new))
        s = s * g_last + jnp.where(bdmask, _bdot_tn(kg, v_new), 0.0)
        o = o * lax.rsqrt(_group_sum(o * o, ones) * (1.0 / DK_D) + 1e-6) * gn_ref[...]
        o_ref[0, rows, :] = o * _silu(zg_ref[0, rows, :])
    s_ref[...] = s

    @pl.when(c == n_steps - 1)
    def _():
        sf = s[:, 0:DK_D]
        for h in range(1, H_D):
            sf = sf + s[:, h * DK_D:(h + 1) * DK_D]
        sf_ref[0] = sf


def _delta(q, k, v, g, beta, zg, s0, gn, n_sub):
    bsz, t, hw = q.shape
    rows = n_sub * DELTA_CHUNK
    tok = pl.BlockSpec((1, rows, hw), lambda b, c: (b, c, 0))
    st = pl.BlockSpec((1, hw, DK_D), lambda b, c: (b, 0, 0))
    return pl.pallas_call(
        functools.partial(_delta_kernel, n_sub),
        grid=(bsz, t // rows),
        in_specs=[tok] * 6 + [st, pl.BlockSpec((1, hw), lambda b, c: (0, 0))],
        out_specs=[tok, st],
        out_shape=[jax.ShapeDtypeStruct((bsz, t, hw), F32),
                   jax.ShapeDtypeStruct((bsz, hw, DK_D), F32)],
        scratch_shapes=[pltpu.VMEM((hw, hw), F32)],
        compiler_params=_cparams("parallel", "arbitrary"),
        name="delta",
    )(q, k, v, g, beta, zg, s0, gn)


def _delta_lanes_kernel(q_ref, k_ref, v_ref, g_ref, b_ref, zg_ref, gn_ref, s0_ref, o_ref, sf_ref,
                        qt_ref, kt_ref):
    t_new, bsz, hw = q_ref.shape
    ones = _group_ones(hw)
    sf_ref[...] = s0_ref[...]
    zero = jnp.zeros((DK_D, bsz), F32)
    for t in range(t_new):
        qt_ref[...] = jnp.transpose(q_ref[t]) * (DK_D ** -0.5)
        kt_ref[...] = jnp.transpose(k_ref[t])
        vt = jnp.transpose(v_ref[t])
        at = jnp.exp(jnp.transpose(g_ref[t]))
        bt = jnp.transpose(b_ref[t])
        outs = []
        for h in range(H_D):
            base = h * DK_D
            a = at[base:base + 1, :]
            beta = bt[base:base + 1, :]

            def k_dot_s(dk, acc, h=h, base=base):
                return acc + kt_ref[pl.ds(base + dk, 1), :] * sf_ref[h, dk]

            ks = lax.fori_loop(0, DK_D, k_dot_s, zero, unroll=8)
            delta = beta * (vt[base:base + DK_D, :] - a * ks)

            def update(dk, acc, h=h, base=base, a=a, delta=delta):
                s_new = a * sf_ref[h, dk] + kt_ref[pl.ds(base + dk, 1), :] * delta
                sf_ref[h, dk] = s_new
                return acc + qt_ref[pl.ds(base + dk, 1), :] * s_new

            outs.append(lax.fori_loop(0, DK_D, update, zero, unroll=8))
        o = jnp.transpose(jnp.concatenate(outs, axis=0))
        o = o * lax.rsqrt(_group_sum(o * o, ones) * (1.0 / DK_D) + 1e-6) * gn_ref[...]
        o_ref[t] = o * _silu(zg_ref[t])


def _delta_lanes(q, k, v, g, beta, zg, gn, s0):
    t_new, bsz, hw = q.shape
    return pl.pallas_call(
        _delta_lanes_kernel,
        out_shape=[jax.ShapeDtypeStruct((t_new, bsz, hw), F32),
                   jax.ShapeDtypeStruct(s0.shape, F32)],
        scratch_shapes=[pltpu.VMEM((hw, bsz), F32), pltpu.VMEM((hw, bsz), F32)],
        compiler_params=pltpu.CompilerParams(vmem_limit_bytes=VMEM_LIMIT),
        name="delta_lanes",
    )(q, k, v, g, beta, zg, gn, s0)


def _sample_mix_kernel(start_pos, u_ref, v_ref, xc_ref, qkv_ref, bd_ref, ad_ref, pool_prev_ref,
                       conv_prev_ref, wl_ref, bsl_ref, lng_ref, lnb_ref, pw_ref, gamma_ref, cw_ref,
                       alog_ref, dt_ref,
                       ob_ref, vn_ref, oc_ref, pool_new_ref, conv_new_ref, q_ref, k_ref, vd_ref,
                       g_ref, b_ref):
    t_new = u_ref.shape[0]
    bsz = u_ref.shape[1]
    ones = _group_ones(W_Q)
    vn = [_group_layer_norm(_gelu_tanh(v_ref[t]), ones, lng_ref[...], lnb_ref[...])
          for t in range(t_new)]
    for t in range(t_new):
        vn_ref[t] = vn[t]
        mixed = bsl_ref[t:t + 1, :] + wl_ref[t, 0:1, :] * vn[0]
        for s in range(1, t + 1):
            mixed = mixed + wl_ref[t, s:s + 1, :] * vn[s]
        ob_ref[t] = _gelu_tanh(u_ref[t]) * mixed
    n_prev = pool_prev_ref.shape[0]

    def pool_row(i):
        return pool_prev_ref[i] if i < n_prev else xc_ref[i - n_prev]

    lane = _iota((bsz, W_Q), 1) >> 6
    for t in range(t_new):
        acc = pool_row(n_prev + t)
        win = None
        done = 1
        for gi, wdw in enumerate(POOL_WINDOWS):
            for i in range(done, wdw):
                acc = acc + pool_row(n_prev + t - i)
            done = wdw
            cnt = float(min(start_pos + t + 1, wdw))
            mean = acc / cnt
            win = mean if win is None else jnp.where(lane >= gi, mean, win)
        d = win - xc_ref[t]
        oc_ref[t] = _bdot(d, pw_ref[...]) * gamma_ref[...]
    for i in range(n_prev):
        pool_new_ref[i] = pool_row(t_new + i)
    c_prev = conv_prev_ref.shape[0]

    def conv_row(i):
        return conv_prev_ref[i] if i < c_prev else qkv_ref[i - c_prev]

    for t in range(t_new):
        y = conv_row(t) * cw_ref[0:1, :]
        for j in range(1, CONV_K):
            y = y + conv_row(t + j) * cw_ref[j:j + 1, :]
        y = _silu(y)
        q_ref[t] = _l2_normalize(y[:, 0:W_Q], ones)
        k_ref[t] = _l2_normalize(y[:, W_Q:2 * W_Q], ones)
        vd_ref[t] = y[:, 2 * W_Q:3 * W_Q]
        beta, g = _delta_gates(bd_ref[t], ad_ref[t], alog_ref[...], dt_ref[...])
        b_ref[t] = beta
        g_ref[t] = g
    for i in range(c_prev):
        conv_new_ref[i] = conv_row(t_new + i)


def _sample_mix(start_pos, ub, vb, xc, qkv, bd, ad, pool_prev, conv_prev, wl, bsl, lng, lnb, pw,
                gamma, cw, a_log, dt_bias):
    t_new, bsz, _ = ub.shape
    slab = jax.ShapeDtypeStruct((t_new, bsz, W_Q), F32)
    out_shape = [slab, slab, slab,
                 jax.ShapeDtypeStruct(pool_prev.shape, F32),
                 jax.ShapeDtypeStruct(conv_prev.shape, F32),
                 slab, slab, slab, slab, slab]
    return pl.pallas_call(
        functools.partial(_sample_mix_kernel, start_pos),
        out_shape=out_shape,
        compiler_params=pltpu.CompilerParams(vmem_limit_bytes=VMEM_LIMIT),
        name="sample_mix",
    )(ub, vb, xc, qkv, bd, ad, pool_prev, conv_prev, wl, bsl, lng, lnb, pw, gamma, cw, a_log,
      dt_bias)


def _mm_ln_kernel(n_in, *refs):
    xs = refs[:n_in]
    ws = refs[n_in:2 * n_in]
    h_ref, g_ref, b_ref, o_ref = refs[2 * n_in:]
    acc = ALPHA * h_ref[...]
    for x_ref, w_ref in zip(xs, ws):
        acc = acc + _bdot(x_ref[...], w_ref[...])
    o_ref[...] = _layer_norm(acc, g_ref[...], b_ref[...])


def _mm_ln(xs, ws, h, g, b, tm):
    n, d = h.shape
    n_in = len(xs)
    return pl.pallas_call(
        functools.partial(_mm_ln_kernel, n_in),
        grid=(n // tm,),
        in_specs=([pl.BlockSpec((tm, x.shape[1]), lambda i: (i, 0)) for x in xs]
                  + [pl.BlockSpec(w.shape, lambda i: (0, 0)) for w in ws]
                  + [pl.BlockSpec((tm, d), lambda i: (i, 0)),
                     pl.BlockSpec((1, d), lambda i: (0, 0)),
                     pl.BlockSpec((1, d), lambda i: (0, 0))]),
        out_specs=pl.BlockSpec((tm, d), lambda i: (i, 0)),
        out_shape=jax.ShapeDtypeStruct((n, d), F32),
        compiler_params=_cparams("parallel"),
        name="mm_ln",
    )(*xs, *ws, h, g.reshape(1, d), b.reshape(1, d))


def _mm_kernel(scale, x_ref, w_ref, o_ref, ob_ref):
    y = _bdot(x_ref[...], w_ref[...])
    o_ref[...] = y
    ob_ref[...] = (y * scale).astype(BF16)


def _mm(x, w, tm, scale=1.0):
    n, d = x.shape
    m = w.shape[1]
    return pl.pallas_call(
        functools.partial(_mm_kernel, scale),
        grid=(n // tm,),
        in_specs=[pl.BlockSpec((tm, d), lambda i: (i, 0)),
                  pl.BlockSpec((d, m), lambda i: (0, 0))],
        out_specs=[pl.BlockSpec((tm, m), lambda i: (i, 0))] * 2,
        out_shape=[jax.ShapeDtypeStruct((n, m), F32), jax.ShapeDtypeStruct((n, m), BF16)],
        compiler_params=_cparams("parallel"),
        name="mm",
    )(x, w)


def _xattn_heads(q, k_ref, v_ref):
    dh = q.shape[1] // XA_HEADS
    outs = []
    for h in range(XA_HEADS):
        lanes = slice(h * dh, (h + 1) * dh)
        s = _bdot_nt(q[:, lanes], k_ref[0, :, lanes])
        s = s - jnp.max(s, axis=-1, keepdims=True)
        e = jnp.exp(s)
        p = e / jnp.sum(e, axis=-1, keepdims=True)
        outs.append(_bdot(p, v_ref[0, :, lanes]))
    return jnp.concatenate(outs, axis=1)


def _xattn_prompt_kernel(h_ref, wq_ref, wo_ref, k_ref, v_ref, g_ref, b_ref, o_ref):
    h = h_ref[...]
    dh = h.shape[1] // XA_HEADS
    q = (_bdot(h, wq_ref[...]) * (dh ** -0.5)).astype(BF16)
    o = _xattn_heads(q, k_ref, v_ref)
    o_ref[...] = _layer_norm(ALPHA * h + _bdot(o, wo_ref[...]), g_ref[...], b_ref[...])


def _xattn_prompt(h, wq, wo, mem_k, mem_v, g, b, seq, tm):
    n, d = h.shape
    n_mem = mem_k.shape[1]
    tps = seq // tm
    full = pl.BlockSpec((d, d), lambda i: (0, 0))
    mem = pl.BlockSpec((1, n_mem, d), lambda i: (i // tps, 0, 0))
    vec = pl.BlockSpec((1, d), lambda i: (0, 0))
    return pl.pallas_call(
        _xattn_prompt_kernel,
        grid=(n // tm,),
        in_specs=[pl.BlockSpec((tm, d), lambda i: (i, 0)), full, full, mem, mem, vec, vec],
        out_specs=pl.BlockSpec((tm, d), lambda i: (i, 0)),
        out_shape=jax.ShapeDtypeStruct((n, d), F32),
        compiler_params=_cparams("parallel"),
        name="xattn_prompt",
    )(h, wq, wo, mem_k, mem_v, g.reshape(1, d), b.reshape(1, d))


def _xattn_sample_kernel(q_ref, k_ref, v_ref, o_ref):
    o_ref[0] = _xattn_heads(q_ref[0], k_ref, v_ref)


def _xattn_sample(q8, mem_k, mem_v, layer):
    bsz, rows, d = q8.shape
    n_mem = mem_k.shape[1]
    tok = pl.BlockSpec((1, rows, d), lambda b: (b, 0, 0))
    mem = pl.BlockSpec((1, n_mem, d), lambda b: (layer * bsz + b, 0, 0))
    return pl.pallas_call(
        _xattn_sample_kernel,
        grid=(bsz,),
        in_specs=[tok, mem, mem],
        out_specs=tok,
        out_shape=jax.ShapeDtypeStruct((bsz, rows, d), F32),
        compiler_params=_cparams("parallel"),
        name="xattn_sample",
    )(q8, mem_k, mem_v)


def _top2_sum(a, b, c, d):
    m1, n1 = jnp.maximum(a, b), jnp.minimum(a, b)
    m2, n2 = jnp.maximum(c, d), jnp.minimum(c, d)
    return jnp.maximum(m1, m2) + jnp.maximum(jnp.minimum(m1, m2), jnp.maximum(n1, n2))


def _router_kernel(x_ref, wt_ref, bias_ref, o_ref):
    x = x_ref[...]
    wt = wt_ref[...]
    xh, xl = _split(x)
    wh, wl = _split(wt)
    nt = (((1,), (1,)), ((), ()))
    logits = (lax.dot_general(wh, xh, nt, preferred_element_type=F32)
              + lax.dot_general(wh, xl, nt, preferred_element_type=F32)
              + lax.dot_general(wl, xh, nt, preferred_element_type=F32))
    s_all = _sigmoid(logits)
    sel_all = s_all + bias_ref[...]
    s = [s_all[e:e + 1, :] for e in range(N_EXPERTS)]
    sel = [sel_all[e:e + 1, :] for e in range(N_EXPERTS)]
    epg = EXPERTS_PER_GROUP
    score = [_top2_sum(*sel[g * epg:(g + 1) * epg]) for g in range(N_GROUPS)]
    best = functools.reduce(jnp.maximum, score)
    taken = None
    in_group = []
    for g in range(N_GROUPS):
        hit = score[g] == best
        if taken is None:
            pick, taken = hit, hit
        else:
            pick = hit & jnp.logical_not(taken)
            taken = taken | hit
        in_group.append(pick)
    chosen = []
    for g in range(N_GROUPS):
        vals = sel[g * epg:(g + 1) * epg]
        for j in range(epg):
            ahead = jnp.zeros_like(vals[j])
            for i in range(epg):
                if i == j:
                    continue
                before = (vals[i] >= vals[j]) if i < j else (vals[i] > vals[j])
                ahead = ahead + jnp.where(before, 1.0, 0.0)
            chosen.append(in_group[g] & (ahead < 1.5))
    gates = [jnp.where(chosen[e], s[e], 0.0) for e in range(N_EXPERTS)]
    total = functools.reduce(lambda a, b: a + b, gates)
    o_ref[...] = jnp.concatenate(gates, axis=0) / total


def _router(x, w_router_t, bias, tm):
    n, d = x.shape
    e = w_router_t.shape[0]
    return pl.pallas_call(
        _router_kernel,
        grid=(n // tm,),
        in_specs=[pl.BlockSpec((tm, d), lambda i: (i, 0)),
                  pl.BlockSpec((e, d), lambda i: (0, 0)),
                  pl.BlockSpec((e, 1), lambda i: (0, 0))],
        out_specs=pl.BlockSpec((e, tm), lambda i: (0, i)),
        out_shape=jax.ShapeDtypeStruct((e, n), F32),
        compiler_params=_cparams("parallel"),
        name="router",
    )(x, w_router_t, bias.reshape(e, 1))


def _moe_kernel(x_ref, comb_ref, wg_ref, wu_ref, wd_ref, g_ref, b_ref, o_ref, xb_ref, acc_ref):
    e = pl.program_id(1)

    @pl.when(e == 0)
    def _():
        xb_ref[...] = x_ref[...].astype(BF16)
        acc_ref[...] = jnp.zeros_like(acc_ref)

    xb = xb_ref[...]
    hg = jnp.dot(xb, wg_ref[0], preferred_element_type=F32)
    hu = jnp.dot(xb, wu_ref[0], preferred_element_type=F32)
    comb = comb_ref[...]
    gate = jnp.sum(jnp.where(_iota(comb.shape, 1) == e, comb, 0.0), axis=1, keepdims=True)
    act = _silu(hg) * hu * gate
    acc_ref[...] += _bdot(act, wd_ref[0])

    @pl.when(e == pl.num_programs(1) - 1)
    def _():
        o_ref[...] = _layer_norm(ALPHA * x_ref[...] + acc_ref[...], g_ref[...], b_ref[...])


def _moe(x, comb, wg, wu, wd, g, b, tm):
    n, d = x.shape
    ne, _, f = wg.shape
    return pl.pallas_call(
        _moe_kernel,
        grid=(n // tm, ne),
        in_specs=[pl.BlockSpec((tm, d), lambda i, e: (i, 0)),
                  pl.BlockSpec((tm, ne), lambda i, e: (i, 0)),
                  pl.BlockSpec((1, d, f), lambda i, e: (e, 0, 0)),
                  pl.BlockSpec((1, d, f), lambda i, e: (e, 0, 0)),
                  pl.BlockSpec((1, f, d), lambda i, e: (e, 0, 0)),
                  pl.BlockSpec((1, d), lambda i, e: (0, 0)),
                  pl.BlockSpec((1, d), lambda i, e: (0, 0))],
        out_specs=pl.BlockSpec((tm, d), lambda i, e: (i, 0)),
        out_shape=jax.ShapeDtypeStruct((n, d), F32),
        scratch_shapes=[pltpu.VMEM((tm, d), BF16), pltpu.VMEM((tm, d), F32)],
        compiler_params=_cparams("parallel", "arbitrary"),
        name="moe",
    )(x, comb, wg, wu, wd, g.reshape(1, d), b.reshape(1, d))


def _lanes(x):
    if x.ndim == 1:
        x = jnp.repeat(x, GROUP_LANES)
    return x.reshape(1, W_Q).astype(F32)


def _block_diag(w):
    g, a, b = w.shape
    eye = jnp.eye(g, dtype=w.dtype)
    return (eye[:, None, :, None] * w[:, :, None, :]).reshape(g * a, g * b)


def _tile(n, pref):
    t = min(n, pref)
    while n % t:
        t //= 2
    return t


def kernel(x_prompt, x_sample, cache_sb_k, cache_sb_v, state_pool, state_conv, state_delta,
           cache_mem_k, cache_mem_v, page_table, mem_prompt,
           ln0_g, ln0_b, w_in, sb_bias, gmlp_ln_g, gmlp_ln_b, gmlp_ws, gmlp_bs, pool_w, pool_gamma,
           conv_w, dn_a_log, dn_dt_bias, dn_norm_g, w_out, ln1_g, ln1_b,
           xa_wq, xa_wk, xa_wv, xa_wo, ln2_g, ln2_b, w_router, router_bias,
           ex_wg, ex_wu, ex_wd, ln3_g, ln3_b):
    bp, seq, d = x_prompt.shape
    bs, t_new, _ = x_sample.shape
    n_p = bp * seq
    n_s = bs * t_new
    n_mem = mem_prompt.shape[1]
    page = cache_sb_k.shape[2]
    past_len = page_table.shape[1] * page
    xa_dh = d // XA_HEADS
    tm_p = _tile(seq, 512)
    tm_s = _tile(n_s, 512)
    tm_moe_p = _tile(n_p, 1024)
    n_sub = 8 if seq % (8 * DELTA_CHUNK) == 0 else 1

    cache_kt = cache_sb_k.transpose(0, 1, 3, 4, 2)
    cache_vt = cache_sb_v.transpose(0, 1, 3, 4, 2)
    memk_s = cache_mem_k.astype(BF16).reshape(DEPTH * bs, n_mem, d)
    memv_s = cache_mem_v.astype(BF16).reshape(DEPTH * bs, n_mem, d)
    w_router_t = w_router.T
    mem_flat = mem_prompt.reshape(bp * n_mem, d)

    def rows8(x):
        x = x.reshape(bs, t_new, -1)
        return jnp.pad(x, ((0, 0), (0, SB_ROWS - t_new), (0, 0)))

    def tmajor(x):
        return x.reshape(bs, t_new, -1).transpose(1, 0, 2)

    def bmajor(x):
        return x.transpose(1, 0, 2)

    h_p = _ln(x_prompt.reshape(n_p, d), ln0_g, ln0_b, tm_p)
    h_s = _ln(x_sample.reshape(n_s, d), ln0_g, ln0_b, tm_s)

    outs = {k: [] for k in ("kp", "vp", "ks", "vs", "gv", "pp", "ps", "cp", "cs", "dp", "ds",
                            "mk", "mv")}
    for l in range(DEPTH):
        w_in_l = _prep_w_in(w_in[l])
        w_out_l = w_out[l].astype(BF16)
        w_out_parts = [w_out_l[i * W_Q:(i + 1) * W_Q] for i in range(4)]
        ws_l = gmlp_ws[l]
        bs_lanes = jnp.repeat(gmlp_bs[l].T, GROUP_LANES, axis=1)
        lng, lnb = _lanes(gmlp_ln_g[l]), _lanes(gmlp_ln_b[l])
        pw = _block_diag(pool_w[l]).astype(BF16)
        gamma = pool_gamma[l].reshape(1, W_Q)
        cw = conv_w[l]
        a_log, dt_b = _lanes(dn_a_log[l]), _lanes(dn_dt_bias[l])
        gn = jnp.tile(dn_norm_g[l], H_D).reshape(1, W_Q)
        wq, wo = xa_wq[l].astype(BF16), xa_wo[l].astype(BF16)
        wg, wu, wd = ex_wg[l].astype(BF16), ex_wu[l].astype(BF16), ex_wd[l].astype(BF16)

        mk, mk_b = _mm(mem_flat, xa_wk[l].astype(BF16), _tile(bp * n_mem, 256))
        mv, mv_b = _mm(mem_flat, xa_wv[l].astype(BF16), _tile(bp * n_mem, 256))
        outs["mk"].append(mk.reshape(bp, n_mem, XA_HEADS, xa_dh))
        outs["mv"].append(mv.reshape(bp, n_mem, XA_HEADS, xa_dh))

        (q, k, v, k_b, v_b, ub, vb, xc, qkv, zg, bd, ad) = _proj_in(h_p, w_in_l, tm_p)
        outs["kp"].append(k.reshape(bp, seq, H_A, DH_A))
        outs["vp"].append(v.reshape(bp, seq, H_A, DH_A))
        outs["pp"].append(xc.reshape(bp, seq, W_Q)[:, seq - POOL_BUF:])
        outs["cp"].append(qkv.reshape(bp, seq, 3 * W_Q)[:, seq - (CONV_K - 1):])
        oa = _sb_prompt(q, k_b, v_b, sb_bias[l], bp, seq)
        ob = _gmlp_prompt(ub, vb, ws_l, bs_lanes, lng, lnb, tm_p)
        oc = _pool_prompt(xc, pw, gamma, seq, tm_p)
        qd, kd, vd, gd, betad = _delta_prep(qkv, bd, ad, cw, a_log, dt_b, seq, tm_p)
        r3 = lambda x: x.reshape(bp, seq, W_Q)
        od, s_fin = _delta(r3(qd), r3(kd), r3(vd), r3(gd), r3(betad), r3(zg),
                           jnp.zeros((bp, H_D * DK_D, DK_D), F32), gn, n_sub)
        outs["dp"].append(s_fin.reshape(bp, H_D, DK_D, DK_D))
        h_p = _mm_ln([oa, ob, oc, od.reshape(n_p, W_Q)], w_out_parts, h_p, ln1_g[l], ln1_b[l], tm_p)
        h_p = _xattn_prompt(h_p, wq, wo, mk_b.reshape(bp, n_mem, d), mv_b.reshape(bp, n_mem, d),
                            ln2_g[l], ln2_b[l], seq, tm_p)
        comb = _router(h_p, w_router_t, router_bias, tm_p).T
        h_p = _moe(h_p, comb, wg, wu, wd, ln3_g[l], ln3_b[l], tm_moe_p)

        (q, k, v, _, _, ub, vb, xc, qkv, zg, bd, ad) = _proj_in(h_s, w_in_l, tm_s)
        outs["ks"].append(k.reshape(bs, t_new, H_A, DH_A))
        outs["vs"].append(v.reshape(bs, t_new, H_A, DH_A))
        oa = _sb_sample(rows8(q), rows8(k), rows8(v), cache_kt, cache_vt, page_table, sb_bias[l],
                        l)[:, :t_new].reshape(n_s, W_Q)
        wl = jnp.repeat(jnp.where(jnp.tril(jnp.ones((t_new, t_new), bool)),
                                  ws_l[:, :t_new, :t_new], 0.0).transpose(1, 2, 0),
                        GROUP_LANES, axis=2)
        (ob, vn, oc, pool_new, conv_new, qd, kd, vd, gd, betad) = _sample_mix(
            past_len, tmajor(ub), tmajor(vb), tmajor(xc), tmajor(qkv), tmajor(bd), tmajor(ad),
            bmajor(state_pool[l]), bmajor(state_conv[l]), wl, bs_lanes[:t_new], lng, lnb, pw,
            gamma, cw, a_log, dt_b)
        outs["gv"].append(bmajor(vn))
        outs["ps"].append(bmajor(pool_new))
        outs["cs"].append(bmajor(conv_new))
        od, s_fin = _delta_lanes(qd, kd, vd, gd, betad, tmajor(zg), gn,
                                 state_delta[l].transpose(1, 2, 3, 0))
        outs["ds"].append(s_fin.transpose(3, 0, 1, 2))
        flat = lambda x: bmajor(x).reshape(n_s, W_Q)
        h_s = _mm_ln([oa, flat(ob), flat(oc), flat(od)], w_out_parts, h_s, ln1_g[l], ln1_b[l],
                     tm_s)
        _, qx = _mm(h_s, wq, tm_s, scale=xa_dh ** -0.5)
        ox = _xattn_sample(rows8(qx), memk_s, memv_s, l)[:, :t_new].reshape(n_s, d)
        h_s = _mm_ln([ox], [wo], h_s, ln2_g[l], ln2_b[l], tm_s)
        comb = _router(h_s, w_router_t, router_bias, tm_s).T
        h_s = _moe(h_s, comb, wg, wu, wd, ln3_g[l], ln3_b[l], tm_s)

    st = lambda key: jnp.stack(outs[key])
    return (h_p.reshape(bp, seq, d), h_s.reshape(bs, t_new, d),
            st("kp"), st("vp"), st("ks"), st("vs"), st("gv"),
            st("pp"), st("ps"), st("cp"), st("cs"), st("dp"), st("ds"),
            st("mk"), st("mv"))
```

```python
import functools

import jax
import jax.numpy as jnp
from jax import lax
from jax.experimental import pallas as pl
from jax.experimental.pallas import tpu as pltpu

F32 = jnp.float32
BF16 = jnp.bfloat16

DEPTH = 2
H_A = 4
DH_A = 64
G_B = 4
GMLP_CHUNK = 128
POOL_WINDOWS = (2, 4, 8, 16)
POOL_BUF = 15
H_D = 4
DK_D = 64
CONV_K = 4
DELTA_CHUNK = 64
XA_HEADS = 4
N_EXPERTS = 16
N_GROUPS = 4
EXPERTS_PER_GROUP = 4
ALPHA = (2 * DEPTH) ** 0.25
LN_EPS = 1e-5
GROUP_LANES = 64
W_Q = 256

SB_BLOCK = 128
VMEM_LIMIT = 48 * 1024 * 1024


def _cparams(*sem):
    return pltpu.CompilerParams(dimension_semantics=sem, vmem_limit_bytes=VMEM_LIMIT)


def _bdot(a, b):
    return jnp.dot(a.astype(BF16), b.astype(BF16), preferred_element_type=F32)


def _bdot_nt(a, b):
    return lax.dot_general(a.astype(BF16), b.astype(BF16), (((1,), (1,)), ((), ())),
                           preferred_element_type=F32)


def _bdot_tn(a, b):
    return lax.dot_general(a.astype(BF16), b.astype(BF16), (((0,), (0,)), ((), ())),
                           preferred_element_type=F32)


def _split(x):
    hi = x.astype(BF16)
    lo = (x - hi.astype(F32)).astype(BF16)
    return hi, lo


def _dot_exact_rhs(x, c):
    hi, lo = _split(x)
    return (jnp.dot(hi, c, preferred_element_type=F32)
            + jnp.dot(lo, c, preferred_element_type=F32))


def _dot_exact_lhs(c, x):
    hi, lo = _split(x)
    return (jnp.dot(c, hi, preferred_element_type=F32)
            + jnp.dot(c, lo, preferred_element_type=F32))


def _dot3(a, b):
    ah, al = _split(a)
    bh, bl = _split(b)
    return (jnp.dot(ah, bh, preferred_element_type=F32)
            + jnp.dot(ah, bl, preferred_element_type=F32)
            + jnp.dot(al, bh, preferred_element_type=F32))


def _iota(shape, dim):
    return lax.broadcasted_iota(jnp.int32, shape, dim)


def _group_ones(n):
    return jnp.where((_iota((n, n), 0) >> 6) == (_iota((n, n), 1) >> 6), 1.0, 0.0).astype(BF16)


def _group_sum(x, ones):
    return _dot_exact_rhs(x, ones)


def _layer_norm(x, g, b):
    mu = jnp.mean(x, axis=-1, keepdims=True)
    xc = x - mu
    var = jnp.mean(xc * xc, axis=-1, keepdims=True)
    return xc * lax.rsqrt(var + LN_EPS) * g + b


def _sigmoid(x):
    return 1.0 / (1.0 + jnp.exp(-x))


def _silu(x):
    return x * _sigmoid(x)


def _softplus(x):
    return jnp.maximum(x, 0.0) + jnp.log(1.0 + jnp.exp(-jnp.abs(x)))


def _gelu_tanh(x):
    return 0.5 * x * (1.0 + jnp.tanh(0.7978845608028654 * (x + 0.044715 * (x * x * x))))


def _group_layer_norm(x, ones, g, b):
    mu = _group_sum(x, ones) * (1.0 / GROUP_LANES)
    xc = x - mu
    var = _group_sum(xc * xc, ones) * (1.0 / GROUP_LANES)
    return xc * lax.rsqrt(var + LN_EPS) * g + b


def _l2_normalize(x, ones):
    return x * lax.rsqrt(_group_sum(x * x, ones) + 1e-6)


def _ln_kernel(x_ref, g_ref, b_ref, o_ref):
    o_ref[...] = _layer_norm(x_ref[...], g_ref[...], b_ref[...])


def _ln(x, g, b, tm):
    n, d = x.shape
    return pl.pallas_call(
        _ln_kernel,
        grid=(n // tm,),
        in_specs=[pl.BlockSpec((tm, d), lambda i: (i, 0)),
                  pl.BlockSpec((1, d), lambda i: (0, 0)),
                  pl.BlockSpec((1, d), lambda i: (0, 0))],
        out_specs=pl.BlockSpec((tm, d), lambda i: (i, 0)),
        out_shape=jax.ShapeDtypeStruct((n, d), F32),
        compiler_params=_cparams("parallel"),
        name="ln0",
    )(x, g.reshape(1, d), b.reshape(1, d))


_SEG = {"q": (0, 256), "k": (256, 512), "v": (512, 768), "ub": (768, 1024), "vb": (1024, 1280),
        "xc": (1280, 1536), "qkv": (1536, 2304), "zg": (2304, 2560), "bd": (2560, 2816),
        "ad": (2816, 3072)}
N_IN_PAD = 3072


def _proj_in_kernel(x_ref, w_ref, q_ref, k_ref, v_ref, kb_ref, vb_ref, ub_ref, vbb_ref, xc_ref,
                    qkv_ref, zg_ref, bd_ref, ad_ref):
    x = x_ref[...].astype(BF16)

    def seg(name):
        a, b = _SEG[name]
        return jnp.dot(x, w_ref[:, a:b], preferred_element_type=F32)

    first_head = _iota((x.shape[0], SB_BLOCK), 1) < GROUP_LANES

    def head_stacked(y):
        yb = y.astype(BF16)
        zero = jnp.zeros((x.shape[0], SB_BLOCK), BF16)
        parts = []
        for p in range(H_A // 2):
            yp = yb[:, p * SB_BLOCK:(p + 1) * SB_BLOCK]
            parts += [jnp.where(first_head, yp, zero), jnp.where(first_head, zero, yp)]
        return jnp.concatenate(parts, axis=1)

    q_ref[...] = (seg("q") * (DH_A ** -0.5)).astype(BF16)
    k = seg("k")
    k_ref[...] = k
    kb_ref[...] = head_stacked(k)
    v = seg("v")
    v_ref[...] = v
    vb_ref[...] = head_stacked(v)
    ub_ref[...] = seg("ub")
    vbb_ref[...] = seg("vb")
    xc_ref[...] = seg("xc")
    qkv_ref[...] = seg("qkv")
    zg_ref[...] = seg("zg")
    bd_ref[...] = seg("bd")
    ad_ref[...] = seg("ad")


def _proj_in(x, w, tm):
    n, d = x.shape
    widths = [(256, BF16), (256, F32), (256, F32), (512, BF16), (512, BF16), (256, F32), (256, F32),
              (256, F32), (768, F32), (256, F32), (256, F32), (256, F32)]
    return pl.pallas_call(
        _proj_in_kernel,
        grid=(n // tm,),
        in_specs=[pl.BlockSpec((tm, d), lambda i: (i, 0)),
                  pl.BlockSpec((d, N_IN_PAD), lambda i: (0, 0))],
        out_specs=[pl.BlockSpec((tm, wd), lambda i: (i, 0)) for wd, _ in widths],
        out_shape=[jax.ShapeDtypeStruct((n, wd), dt) for wd, dt in widths],
        compiler_params=_cparams("parallel"),
        name="proj_in",
    )(x, w)


def _prep_w_in(w_in_l):
    main = w_in_l[:, :2560]
    bd = jnp.repeat(w_in_l[:, 2560:2564], GROUP_LANES, axis=1)
    ad = jnp.repeat(w_in_l[:, 2564:2568], GROUP_LANES, axis=1)
    return jnp.concatenate([main, bd, ad], axis=1).astype(BF16)


NEG_BIG = -1e30


LOG2E = 1.4426950408889634


def _pair_rows(x, p):
    w = 2 * SB_BLOCK
    return jnp.concatenate([x[:, p * w:p * w + SB_BLOCK], x[:, p * w + SB_BLOCK:(p + 1) * w]], axis=0)


def _suffix_ones_hilo():
    nb = SB_BLOCK
    r = lax.broadcasted_iota(jnp.int32, (2 * nb, 2 * nb), 0) & (nb - 1)
    c = lax.broadcasted_iota(jnp.int32, (2 * nb, 2 * nb), 1)
    return jnp.where((c >= nb) | (r > c), 1.0, 0.0).astype(BF16)


SB_QROWS = 2 * SB_BLOCK


def _sb_prompt_kernel(bias_ref, q_ref, k_ref, v_ref, uo_ref, o_ref, carry_ref, acc_ref,
                      z_a, att_a, z_b, att_b):
    i = pl.program_id(1)
    nb = SB_BLOCK
    nq = SB_QROWS
    pw = 2 * nb
    last = (nq // nb) * (i + 1) - 1
    head_of_lane = _iota((1, H_A * nb), 1) >> 7
    bias_row = jnp.zeros((1, H_A * nb), F32)
    for h in range(H_A):
        bias_row = jnp.where(head_of_lane == h, bias_ref[h], bias_row)
    col_minus_row = _iota((nq, nb), 1) - _iota((nq, nb), 0)

    acc_ref[...] = jnp.zeros_like(acc_ref)
    carry_ref[...] = jnp.zeros_like(carry_ref)
    z_b[...] = jnp.zeros_like(z_b)
    att_b[...] = jnp.zeros_like(att_b)

    def stages(t, z_w, att_w, z_r, att_r, masked):
        ja = jnp.maximum(last - t, 0)
        k = k_ref[pl.ds(pl.multiple_of(ja * nb, nb), nb), :]
        for p in range(2):
            z_w[:, p * pw:(p + 1) * pw] = lax.dot_general(
                q_ref[:, p * nb:(p + 1) * nb], _pair_rows(k, p), (((1,), (1,)), ((), ())),
                preferred_element_type=F32)
        z = z_r[...] + bias_row
        l = jnp.log(1.0 + jnp.exp2(jnp.abs(z) * (-LOG2E)))
        nlk = jnp.maximum(z, 0.0) + l
        lb = z - nlk
        if masked:
            tb = t - 1
            span = jnp.where((tb >= 0) & (tb <= last), (tb - (nq // nb - 1)) * nb, -2 * nq)
            valid = col_minus_row < span
            nlk = nlk * jnp.concatenate([jnp.where(valid, 1.0, 0.0)] * H_A, axis=1)
            lb = lb + jnp.concatenate([jnp.where(valid, 0.0, NEG_BIG)] * H_A, axis=1)
        hi, lo = _split(nlk)
        res = [jnp.dot(jnp.concatenate([hi[:, h * nb:(h + 1) * nb], lo[:, h * nb:(h + 1) * nb]],
                                       axis=1), uo_ref[...], preferred_element_type=F32)
               for h in range(H_A)]
        right = jnp.concatenate([x[:, :nb] for x in res], axis=1)
        total = jnp.concatenate([x[:, nb:] for x in res], axis=1)
        carry = carry_ref[...]
        att_w[...] = jnp.exp(lb - right - carry).astype(BF16)
        carry_ref[...] = carry + total
        jc = jnp.clip(last - t + 2, 0, last)
        v = v_ref[pl.ds(pl.multiple_of(jc * nb, nb), nb), :]
        for p in range(2):
            acc_ref[:, p * nb:(p + 1) * nb] += jnp.dot(att_r[:, p * pw:(p + 1) * pw],
                                                       _pair_rows(v, p),
                                                       preferred_element_type=F32)

    def four_trips(masked, u, carry_unused):
        stages(4 * u, z_a, att_a, z_b, att_b, masked)
        stages(4 * u + 1, z_b, att_b, z_a, att_a, masked)
        stages(4 * u + 2, z_a, att_a, z_b, att_b, masked)
        stages(4 * u + 3, z_b, att_b, z_a, att_a, masked)
        return carry_unused

    four_trips(True, 0, 0)
    n_plain = jnp.maximum((last - 2) // 4, 0)
    lax.fori_loop(1, n_plain + 1, functools.partial(four_trips, False), 0)
    lax.fori_loop(n_plain + 1, (last + 6) // 4, functools.partial(four_trips, True), 0)
    o_ref[...] = acc_ref[...]


def _sb_prompt(q, k, v, bias, batch, seq):
    n = q.shape[0]
    nq = seq // SB_QROWS
    hw = H_A * SB_BLOCK
    stage_bufs = [pltpu.VMEM((SB_QROWS, hw), F32),
                  pltpu.VMEM((SB_QROWS, hw), BF16)]
    return pl.pallas_call(
        _sb_prompt_kernel,
        grid=(batch, nq),
        in_specs=[pl.BlockSpec(memory_space=pltpu.SMEM),
                  pl.BlockSpec((SB_QROWS, W_Q), lambda b, i: (b * nq + i, 0)),
                  pl.BlockSpec((seq, hw), lambda b, i: (b, 0)),
                  pl.BlockSpec((seq, hw), lambda b, i: (b, 0)),
                  pl.BlockSpec((2 * SB_BLOCK, 2 * SB_BLOCK), lambda b, i: (0, 0))],
        out_specs=pl.BlockSpec((SB_QROWS, W_Q), lambda b, i: (b * nq + i, 0)),
        scratch_shapes=[pltpu.VMEM((SB_QROWS, hw), F32),
                        pltpu.VMEM((SB_QROWS, W_Q), F32)] + stage_bufs * 2,
        out_shape=jax.ShapeDtypeStruct((n, W_Q), F32),
        compiler_params=_cparams("parallel", "arbitrary"),
        name="sb_prompt",
    )(bias, q, k, v, _suffix_ones_hilo())


SB_ROWS = 8


def _sb_sample_kernel(n_pages, pt_ref, bias_ref, q_ref, kn_ref, vn_ref, uo_ref, *rest):
    kpages = rest[:n_pages]
    vpages = rest[n_pages:2 * n_pages]
    o_ref = rest[2 * n_pages]
    m = H_A * SB_ROWS
    q8 = q_ref[0]
    qh = [q8[:, h * DH_A:(h + 1) * DH_A] for h in range(H_A)]
    rowh = _iota((m, SB_BLOCK), 0) >> 3
    bias = jnp.zeros((m, SB_BLOCK), F32)
    for h in range(H_A):
        bias = jnp.where(rowh == h, bias_ref[h], bias)
    tri = _iota((m, SB_BLOCK), 1) < (_iota((m, SB_BLOCK), 0) & (SB_ROWS - 1))

    def log_terms(z, mask):
        z = z + bias
        l = jnp.log(1.0 + jnp.exp2(jnp.abs(z) * (-LOG2E)))
        nlk = jnp.maximum(z, 0.0) + l
        lb = z - nlk
        if mask is not None:
            nlk = jnp.where(mask, nlk, 0.0)
            lb = jnp.where(mask, lb, NEG_BIG)
        return nlk, lb

    def suffix_sums(nlk):
        hi, lo = _split(nlk)
        return jnp.dot(jnp.concatenate([hi, lo], axis=1), uo_ref[...], preferred_element_type=F32)

    def head_rows(att, h):
        return att[h * SB_ROWS:(h + 1) * SB_ROWS]

    pad = jnp.zeros((SB_BLOCK - SB_ROWS, W_Q), F32)
    kn = jnp.concatenate([kn_ref[0], pad], axis=0)
    vn = jnp.concatenate([vn_ref[0], pad], axis=0)
    zs = [jnp.concatenate([_bdot(qh[h], kpages[p][h]) for h in range(H_A)], axis=0)
          for p in range(n_pages)]
    zs.append(jnp.concatenate([_bdot_nt(qh[h], kn[:, h * DH_A:(h + 1) * DH_A])
                               for h in range(H_A)], axis=0))
    terms = [log_terms(z, tri if p == n_pages else None) for p, z in enumerate(zs)]
    sums = [suffix_sums(nlk) for nlk, _ in terms]
    carry = jnp.zeros((m, SB_BLOCK), F32)
    atts = [None] * (n_pages + 1)
    for p in range(n_pages, -1, -1):
        atts[p] = jnp.exp(terms[p][1] - sums[p][:, :SB_BLOCK] - carry)
        carry = carry + sums[p][:, SB_BLOCK:]
    acc = [_bdot(head_rows(atts[n_pages], h), vn[:, h * DH_A:(h + 1) * DH_A]) for h in range(H_A)]
    for p in range(n_pages):
        acc = [acc[h] + _bdot_nt(head_rows(atts[p], h), vpages[p][h]) for h in range(H_A)]
    o_ref[0] = jnp.concatenate(acc, axis=1)


def _sb_sample(q8, k8, v8, cache_kt, cache_vt, page_table, bias, layer):
    bsz, n_pages = page_table.shape
    page = cache_kt.shape[4]

    def page_spec(p):
        return pl.BlockSpec((None, None, H_A, DH_A, page), lambda b, pt: (layer, pt[b, p], 0, 0, 0))

    tok = pl.BlockSpec((1, SB_ROWS, W_Q), lambda b, pt: (b, 0, 0))
    return pl.pallas_call(
        functools.partial(_sb_sample_kernel, n_pages),
        grid_spec=pltpu.PrefetchScalarGridSpec(
            num_scalar_prefetch=1,
            grid=(bsz,),
            in_specs=([pl.BlockSpec(memory_space=pltpu.SMEM), tok, tok, tok,
                       pl.BlockSpec((2 * SB_BLOCK, 2 * SB_BLOCK), lambda b, pt: (0, 0))]
                      + [page_spec(p) for p in range(n_pages)] * 2),
            out_specs=tok),
        out_shape=jax.ShapeDtypeStruct((bsz, SB_ROWS, W_Q), F32),
        compiler_params=_cparams("parallel"),
        name="sb_sample",
    )(page_table, bias, q8, k8, v8, _suffix_ones_hilo(), *([cache_kt] * n_pages),
      *([cache_vt] * n_pages))


def _gmlp_prompt_kernel(u_ref, v_ref, ws_ref, bs_ref, g_ref, b_ref, o_ref):
    c = GMLP_CHUNK
    ones = _group_ones(W_Q)
    tril = _iota((c, c), 0) >= _iota((c, c), 1)
    lane_group = _iota((c, W_Q), 1) >> 6
    w = [jnp.where(tril, ws_ref[g], 0.0).astype(BF16) for g in range(G_B)]
    for n in range(u_ref.shape[0] // c):
        rows = slice(n * c, (n + 1) * c)
        u = _gelu_tanh(u_ref[rows, :])
        vn = _group_layer_norm(_gelu_tanh(v_ref[rows, :]), ones, g_ref[...], b_ref[...])
        vnb = vn.astype(BF16)
        mixed = bs_ref[...]
        for g in range(G_B):
            vg = jnp.where(lane_group == g, vnb, jnp.zeros_like(vnb))
            mixed = mixed + jnp.dot(w[g], vg, preferred_element_type=F32)
        o_ref[rows, :] = u * mixed


def _gmlp_prompt(ub, vb, ws, bs_lanes, g, b, tm):
    n = ub.shape[0]
    c = GMLP_CHUNK
    tile = pl.BlockSpec((tm, W_Q), lambda i: (i, 0))
    vec = pl.BlockSpec((1, W_Q), lambda i: (0, 0))
    return pl.pallas_call(
        _gmlp_prompt_kernel,
        grid=(n // tm,),
        in_specs=[tile, tile,
                  pl.BlockSpec((G_B, c, c), lambda i: (0, 0, 0)),
                  pl.BlockSpec((c, W_Q), lambda i: (0, 0)),
                  vec, vec],
        out_specs=tile,
        out_shape=jax.ShapeDtypeStruct((n, W_Q), F32),
        compiler_params=_cparams("parallel"),
        name="gmlp_prompt",
    )(ub, vb, ws, bs_lanes, g, b)


HALO = 16


def _pool_windows(load, t, start_pos):
    lane = _iota((t, SB_BLOCK), 1)
    pos = start_pos + _iota((t, SB_BLOCK), 0)
    halves = []
    for half, (w_small, w_big) in enumerate(((POOL_WINDOWS[0], POOL_WINDOWS[1]),
                                              (POOL_WINDOWS[2], POOL_WINDOWS[3]))):
        s = load(0, half)
        for i in range(1, w_small):
            s = s + load(i, half)
        big = s
        for i in range(w_small, w_big):
            big = big + load(i, half)
        small_lane = lane < GROUP_LANES
        win = jnp.where(small_lane, s, big)
        w = jnp.where(small_lane, w_small, w_big)
        cnt = jnp.minimum(pos + 1, w).astype(F32)
        halves.append(win / cnt)
    return jnp.concatenate(halves, axis=1)


def _pool_prompt_kernel(tiles_per_seq, x_ref, halo_ref, w_ref, gamma_ref, o_ref, xx_ref):
    i = pl.program_id(0)
    t = x_ref.shape[0]
    first = (i % tiles_per_seq) == 0
    xx_ref[0:HALO, :] = jnp.where(first, 0.0, halo_ref[...])
    xx_ref[HALO:, :] = x_ref[...]

    def load(shift, half):
        return xx_ref[pl.ds(HALO - shift, t), half * SB_BLOCK:(half + 1) * SB_BLOCK]

    start = (i % tiles_per_seq) * t
    means = _pool_windows(load, t, start)
    d = means - x_ref[...]
    o_ref[...] = _bdot(d, w_ref[...]) * gamma_ref[...]


def _pool_prompt(xc, w_bd, gamma, seq, tm):
    n = xc.shape[0]
    hb = tm // HALO
    return pl.pallas_call(
        functools.partial(_pool_prompt_kernel, seq // tm),
        grid=(n // tm,),
        in_specs=[pl.BlockSpec((tm, W_Q), lambda i: (i, 0)),
                  pl.BlockSpec((HALO, W_Q), lambda i: (jnp.maximum(i * hb - 1, 0), 0)),
                  pl.BlockSpec((W_Q, W_Q), lambda i: (0, 0)),
                  pl.BlockSpec((1, W_Q), lambda i: (0, 0))],
        out_specs=pl.BlockSpec((tm, W_Q), lambda i: (i, 0)),
        out_shape=jax.ShapeDtypeStruct((n, W_Q), F32),
        scratch_shapes=[pltpu.VMEM((tm + HALO, W_Q), F32)],
        compiler_params=_cparams("parallel"),
        name="pool_prompt",
    )(xc, xc, w_bd, gamma)


CONV_HALO = 8


def _delta_gates(bd, ad, a_log, dt_bias):
    beta = _sigmoid(bd)
    g = -jnp.exp(a_log) * _softplus(ad + dt_bias)
    return beta, g


def _delta_prep_kernel(tiles_per_seq, x_ref, halo_ref, bd_ref, ad_ref, cw_ref, alog_ref, dt_ref,
                       q_ref, k_ref, v_ref, g_ref, b_ref, xx_ref):
    i = pl.program_id(0)
    t = x_ref.shape[0]
    first = (i % tiles_per_seq) == 0
    xx_ref[0:CONV_HALO, :] = jnp.where(first, 0.0, halo_ref[...])
    xx_ref[CONV_HALO:, :] = x_ref[...]
    ones = _group_ones(W_Q)
    off = CONV_HALO - (CONV_K - 1)
    for part, out in enumerate((q_ref, k_ref, v_ref)):
        lanes = slice(part * W_Q, (part + 1) * W_Q)
        y = xx_ref[pl.ds(off, t), lanes] * cw_ref[0:1, lanes]
        for j in range(1, CONV_K):
            y = y + xx_ref[pl.ds(off + j, t), lanes] * cw_ref[j:j + 1, lanes]
        y = _silu(y)
        out[...] = y if part == 2 else _l2_normalize(y, ones)
    beta, g = _delta_gates(bd_ref[...], ad_ref[...], alog_ref[...], dt_ref[...])
    b_ref[...] = beta
    g_ref[...] = g


def _delta_prep(qkv, bd, ad, conv_w, a_log, dt_bias, seq, tm):
    n, wd = qkv.shape
    hb = tm // CONV_HALO
    tile = pl.BlockSpec((tm, W_Q), lambda i: (i, 0))
    vec = pl.BlockSpec((1, W_Q), lambda i: (0, 0))
    return pl.pallas_call(
        functools.partial(_delta_prep_kernel, seq // tm),
        grid=(n // tm,),
        in_specs=[pl.BlockSpec((tm, wd), lambda i: (i, 0)),
                  pl.BlockSpec((CONV_HALO, wd), lambda i: (jnp.maximum(i * hb - 1, 0), 0)),
                  tile, tile,
                  pl.BlockSpec((CONV_K, wd), lambda i: (0, 0)),
                  vec, vec],
        out_specs=[tile] * 5,
        out_shape=[jax.ShapeDtypeStruct((n, W_Q), F32)] * 5,
        scratch_shapes=[pltpu.VMEM((tm + CONV_HALO, wd), F32)],
        compiler_params=_cparams("parallel"),
        name="delta_prep",
    )(qkv, qkv, bd, ad, conv_w, a_log, dt_bias)


def _delta_kernel(n_sub, q_ref, k_ref, v_ref, g_ref, b_ref, zg_ref, s0_ref, gn_ref, o_ref, sf_ref,
                  s_ref):
    c = pl.program_id(1)
    n_steps = pl.num_programs(1)
    cs = DELTA_CHUNK
    hw = H_D * DK_D
    row = _iota((cs, hw), 0)
    col = _iota((cs, hw), 1) & (DK_D - 1)
    incl = row >= col
    strict = row > col
    bdmask = (_iota((hw, hw), 0) >> 6) == (_iota((hw, hw), 1) >> 6)
    ones = _group_ones(hw)
    tril = jnp.where(_iota((cs, cs), 0) >= _iota((cs, cs), 1), 1.0, 0.0).astype(BF16)

    def bd(x):
        return jnp.where(bdmask, jnp.concatenate([x] * H_D, axis=0), 0.0)

    @pl.when(c == 0)
    def _():
        s_ref[...] = jnp.where(bdmask, jnp.concatenate([s0_ref[0]] * H_D, axis=1), 0.0)

    chunks = [slice(n * cs, (n + 1) * cs) for n in range(n_sub)]
    each = lambda f, *xs: [f(*x) for x in zip(*xs)]
    q = [q_ref[0, r, :] * (DK_D ** -0.5) for r in chunks]
    k = [k_ref[0, r, :] for r in chunks]
    g = [g_ref[0, r, :] for r in chunks]
    beta = [b_ref[0, r, :] for r in chunks]
    vb = [v_ref[0, r, :] * bt for r, bt in zip(chunks, beta)]
    kb = each(lambda x, bt: x * bt, k, beta)
    gc = [_dot_exact_lhs(tril, x) for x in g]
    diff = [_dot_exact_lhs(tril, jnp.where(strict, x, 0.0)) for x in g]
    decay = [jnp.where(incl, jnp.exp(jnp.where(incl, x, 0.0)), 0.0) for x in diff]
    eg = [jnp.exp(x) for x in gc]
    kbd = [bd(x).astype(BF16) for x in k]
    neg_lower = each(lambda x, y, dc: jnp.where(strict, -(_bdot_nt(x, y) * dc), 0.0), kb, kbd, decay)
    eye = jnp.where(row == col, 1.0, 0.0)
    t_inv = [eye + x for x in neg_lower]
    p = neg_lower
    for _ in range(5):
        p = [_dot3(x, bd(x)) for x in p]
        t_inv = each(lambda t, x: t + _dot3(t, bd(x)), t_inv, p)
    u = each(lambda t, x: _bdot(t, bd(x)), t_inv, vb)
    w = each(lambda t, x, e: _bdot(t, bd(x * e)), t_inv, kb, eg)
    intra = each(lambda x, y, dc: jnp.where(incl, _bdot_nt(x, y) * dc, 0.0), q, kbd, decay)
    gc_last = [x[cs - 1:cs, :] for x in gc]
    qg = each(lambda x, e: x * e, q, eg)
    kg = each(lambda x, gl, c_: x * jnp.exp(gl - c_), k, gc_last, gc)
    solved = list(zip(u, w, intra, qg, kg, [jnp.exp(x) for x in gc_last]))
    s = s_ref[...]
    for n, (u, w, intra, qg, kg, g_last) in enumerate(solved):
        rows = slice(n * cs, (n + 1) * cs)
        v_new = u - _bdot(w, s)
        o = _bdot(qg, s) + _bdot(intra, bd(v_new))
        s = s * g_last + jnp.where(bdmask, _bdot_tn(kg, v_new), 0.0)
        o = o * lax.rsqrt(_group_sum(o * o, ones) * (1.0 / DK_D) + 1e-6) * gn_ref[...]
        o_ref[0, rows, :] = o * _silu(zg_ref[0, rows, :])
    s_ref[...] = s

    @pl.when(c == n_steps - 1)
    def _():
        sf = s[:, 0:DK_D]
        for h in range(1, H_D):
            sf = sf + s[:, h * DK_D:(h + 1) * DK_D]
        sf_ref[0] = sf


def _delta(q, k, v, g, beta, zg, s0, gn, n_sub):
    bsz, t, hw = q.shape
    rows = n_sub * DELTA_CHUNK
    tok = pl.BlockSpec((1, rows, hw), lambda b, c: (b, c, 0))
    st = pl.BlockSpec((1, hw, DK_D), lambda b, c: (b, 0, 0))
    return pl.pallas_call(
        functools.partial(_delta_kernel, n_sub),
        grid=(bsz, t // rows),
        in_specs=[tok] * 6 + [st, pl.BlockSpec((1, hw), lambda b, c: (0, 0))],
        out_specs=[tok, st],
        out_shape=[jax.ShapeDtypeStruct((bsz, t, hw), F32),
                   jax.ShapeDtypeStruct((bsz, hw, DK_D), F32)],
        scratch_shapes=[pltpu.VMEM((hw, hw), F32)],
        compiler_params=_cparams("parallel", "arbitrary"),
        name="delta",
    )(q, k, v, g, beta, zg, s0, gn)


def _delta_lanes_kernel(q_ref, k_ref, v_ref, g_ref, b_ref, zg_ref, gn_ref, s0_ref, o_ref, sf_ref,
                        qt_ref, kt_ref):
    t_new, bsz, hw = q_ref.shape
    ones = _group_ones(hw)
    sf_ref[...] = s0_ref[...]
    zero = jnp.zeros((DK_D, bsz), F32)
    for t in range(t_new):
        qt_ref[...] = jnp.transpose(q_ref[t]) * (DK_D ** -0.5)
        kt_ref[...] = jnp.transpose(k_ref[t])
        vt = jnp.transpose(v_ref[t])
        at = jnp.exp(jnp.transpose(g_ref[t]))
        bt = jnp.transpose(b_ref[t])
        outs = []
        for h in range(H_D):
            base = h * DK_D
            a = at[base:base + 1, :]
            beta = bt[base:base + 1, :]

            def k_dot_s(dk, acc, h=h, base=base):
                return acc + kt_ref[pl.ds(base + dk, 1), :] * sf_ref[h, dk]

            ks = lax.fori_loop(0, DK_D, k_dot_s, zero, unroll=8)
            delta = beta * (vt[base:base + DK_D, :] - a * ks)

            def update(dk, acc, h=h, base=base, a=a, delta=delta):
                s_new = a * sf_ref[h, dk] + kt_ref[pl.ds(base + dk, 1), :] * delta
                sf_ref[h, dk] = s_new
                return acc + qt_ref[pl.ds(base + dk, 1), :] * s_new

            outs.append(lax.fori_loop(0, DK_D, update, zero, unroll=8))
        o = jnp.transpose(jnp.concatenate(outs, axis=0))
        o = o * lax.rsqrt(_group_sum(o * o, ones) * (1.0 / DK_D) + 1e-6) * gn_ref[...]
        o_ref[t] = o * _silu(zg_ref[t])


def _delta_lanes(q, k, v, g, beta, zg, gn, s0):
    t_new, bsz, hw = q.shape
    return pl.pallas_call(
        _delta_lanes_kernel,
        out_shape=[jax.ShapeDtypeStruct((t_new, bsz, hw), F32),
                   jax.ShapeDtypeStruct(s0.shape, F32)],
        scratch_shapes=[pltpu.VMEM((hw, bsz), F32), pltpu.VMEM((hw, bsz), F32)],
        compiler_params=pltpu.CompilerParams(vmem_limit_bytes=VMEM_LIMIT),
        name="delta_lanes",
    )(q, k, v, g, beta, zg, gn, s0)


def _sample_mix_kernel(start_pos, u_ref, v_ref, xc_ref, qkv_ref, bd_ref, ad_ref, pool_prev_ref,
                       conv_prev_ref, wl_ref, bsl_ref, lng_ref, lnb_ref, pw_ref, gamma_ref, cw_ref,
                       alog_ref, dt_ref,
                       ob_ref, vn_ref, oc_ref, pool_new_ref, conv_new_ref, q_ref, k_ref, vd_ref,
                       g_ref, b_ref):
    t_new = u_ref.shape[0]
    bsz = u_ref.shape[1]
    ones = _group_ones(W_Q)
    vn = [_group_layer_norm(_gelu_tanh(v_ref[t]), ones, lng_ref[...], lnb_ref[...])
          for t in range(t_new)]
    for t in range(t_new):
        vn_ref[t] = vn[t]
        mixed = bsl_ref[t:t + 1, :] + wl_ref[t, 0:1, :] * vn[0]
        for s in range(1, t + 1):
            mixed = mixed + wl_ref[t, s:s + 1, :] * vn[s]
        ob_ref[t] = _gelu_tanh(u_ref[t]) * mixed
    n_prev = pool_prev_ref.shape[0]

    def pool_row(i):
        return pool_prev_ref[i] if i < n_prev else xc_ref[i - n_prev]

    lane = _iota((bsz, W_Q), 1) >> 6
    for t in range(t_new):
        acc = pool_row(n_prev + t)
        win = None
        done = 1
        for gi, wdw in enumerate(POOL_WINDOWS):
            for i in range(done, wdw):
                acc = acc + pool_row(n_prev + t - i)
            done = wdw
            cnt = float(min(start_pos + t + 1, wdw))
            mean = acc / cnt
            win = mean if win is None else jnp.where(lane >= gi, mean, win)
        d = win - xc_ref[t]
        oc_ref[t] = _bdot(d, pw_ref[...]) * gamma_ref[...]
    for i in range(n_prev):
        pool_new_ref[i] = pool_row(t_new + i)
    c_prev = conv_prev_ref.shape[0]

    def conv_row(i):
        return conv_prev_ref[i] if i < c_prev else qkv_ref[i - c_prev]

    for t in range(t_new):
        y = conv_row(t) * cw_ref[0:1, :]
        for j in range(1, CONV_K):
            y = y + conv_row(t + j) * cw_ref[j:j + 1, :]
        y = _silu(y)
        q_ref[t] = _l2_normalize(y[:, 0:W_Q], ones)
        k_ref[t] = _l2_normalize(y[:, W_Q:2 * W_Q], ones)
        vd_ref[t] = y[:, 2 * W_Q:3 * W_Q]
        beta, g = _delta_gates(bd_ref[t], ad_ref[t], alog_ref[...], dt_ref[...])
        b_ref[t] = beta
        g_ref[t] = g
    for i in range(c_prev):
        conv_new_ref[i] = conv_row(t_new + i)


def _sample_mix(start_pos, ub, vb, xc, qkv, bd, ad, pool_prev, conv_prev, wl, bsl, lng, lnb, pw,
                gamma, cw, a_log, dt_bias):
    t_new, bsz, _ = ub.shape
    slab = jax.ShapeDtypeStruct((t_new, bsz, W_Q), F32)
    out_shape = [slab, slab, slab,
                 jax.ShapeDtypeStruct(pool_prev.shape, F32),
                 jax.ShapeDtypeStruct(conv_prev.shape, F32),
                 slab, slab, slab, slab, slab]
    return pl.pallas_call(
        functools.partial(_sample_mix_kernel, start_pos),
        out_shape=out_shape,
        compiler_params=pltpu.CompilerParams(vmem_limit_bytes=VMEM_LIMIT),
        name="sample_mix",
    )(ub, vb, xc, qkv, bd, ad, pool_prev, conv_prev, wl, bsl, lng, lnb, pw, gamma, cw, a_log,
      dt_bias)


def _mm_ln_kernel(n_in, *refs):
    xs = refs[:n_in]
    ws = refs[n_in:2 * n_in]
    h_ref, g_ref, b_ref, o_ref = refs[2 * n_in:]
    acc = ALPHA * h_ref[...]
    for x_ref, w_ref in zip(xs, ws):
        acc = acc + _bdot(x_ref[...], w_ref[...])
    o_ref[...] = _layer_norm(acc, g_ref[...], b_ref[...])


def _mm_ln(xs, ws, h, g, b, tm):
    n, d = h.shape
    n_in = len(xs)
    return pl.pallas_call(
        functools.partial(_mm_ln_kernel, n_in),
        grid=(n // tm,),
        in_specs=([pl.BlockSpec((tm, x.shape[1]), lambda i: (i, 0)) for x in xs]
                  + [pl.BlockSpec(w.shape, lambda i: (0, 0)) for w in ws]
                  + [pl.BlockSpec((tm, d), lambda i: (i, 0)),
                     pl.BlockSpec((1, d), lambda i: (0, 0)),
                     pl.BlockSpec((1, d), lambda i: (0, 0))]),
        out_specs=pl.BlockSpec((tm, d), lambda i: (i, 0)),
        out_shape=jax.ShapeDtypeStruct((n, d), F32),
        compiler_params=_cparams("parallel"),
        name="mm_ln",
    )(*xs, *ws, h, g.reshape(1, d), b.reshape(1, d))


def _mm_kernel(scale, x_ref, w_ref, o_ref, ob_ref):
    y = _bdot(x_ref[...], w_ref[...])
    o_ref[...] = y
    ob_ref[...] = (y * scale).astype(BF16)


def _mm(x, w, tm, scale=1.0):
    n, d = x.shape
    m = w.shape[1]
    return pl.pallas_call(
        functools.partial(_mm_kernel, scale),
        grid=(n // tm,),
        in_specs=[pl.BlockSpec((tm, d), lambda i: (i, 0)),
                  pl.BlockSpec((d, m), lambda i: (0, 0))],
        out_specs=[pl.BlockSpec((tm, m), lambda i: (i, 0))] * 2,
        out_shape=[jax.ShapeDtypeStruct((n, m), F32), jax.ShapeDtypeStruct((n, m), BF16)],
        compiler_params=_cparams("parallel"),
        name="mm",
    )(x, w)


def _xattn_heads(q, k_ref, v_ref):
    dh = q.shape[1] // XA_HEADS
    outs = []
    for h in range(XA_HEADS):
        lanes = slice(h * dh, (h + 1) * dh)
        s = _bdot_nt(q[:, lanes], k_ref[0, :, lanes])
        s = s - jnp.max(s, axis=-1, keepdims=True)
        e = jnp.exp(s)
        p = e / jnp.sum(e, axis=-1, keepdims=True)
        outs.append(_bdot(p, v_ref[0, :, lanes]))
    return jnp.concatenate(outs, axis=1)


def _xattn_prompt_kernel(h_ref, wq_ref, wo_ref, k_ref, v_ref, g_ref, b_ref, o_ref):
    h = h_ref[...]
    dh = h.shape[1] // XA_HEADS
    q = (_bdot(h, wq_ref[...]) * (dh ** -0.5)).astype(BF16)
    o = _xattn_heads(q, k_ref, v_ref)
    o_ref[...] = _layer_norm(ALPHA * h + _bdot(o, wo_ref[...]), g_ref[...], b_ref[...])


def _xattn_prompt(h, wq, wo, mem_k, mem_v, g, b, seq, tm):
    n, d = h.shape
    n_mem = mem_k.shape[1]
    tps = seq // tm
    full = pl.BlockSpec((d, d), lambda i: (0, 0))
    mem = pl.BlockSpec((1, n_mem, d), lambda i: (i // tps, 0, 0))
    vec = pl.BlockSpec((1, d), lambda i: (0, 0))
    return pl.pallas_call(
        _xattn_prompt_kernel,
        grid=(n // tm,),
        in_specs=[pl.BlockSpec((tm, d), lambda i: (i, 0)), full, full, mem, mem, vec, vec],
        out_specs=pl.BlockSpec((tm, d), lambda i: (i, 0)),
        out_shape=jax.ShapeDtypeStruct((n, d), F32),
        compiler_params=_cparams("parallel"),
        name="xattn_prompt",
    )(h, wq, wo, mem_k, mem_v, g.reshape(1, d), b.reshape(1, d))


def _xattn_sample_kernel(q_ref, k_ref, v_ref, o_ref):
    q = q_ref[0]
    n_mem, n_heads, dh = k_ref.shape
    k2 = k_ref[...].reshape(n_mem * n_heads, dh).astype(BF16)
    v2 = v_ref[...].reshape(n_mem * n_heads, dh).astype(BF16)
    head_of_lane = _iota((q.shape[0], n_mem * n_heads), 1) & (n_heads - 1)
    outs = []
    for h in range(n_heads):
        s = _bdot_nt(q[:, h * dh:(h + 1) * dh], k2)
        s = jnp.where(head_of_lane == h, s, NEG_BIG)
        s = s - jnp.max(s, axis=-1, keepdims=True)
        e = jnp.exp(s)
        p = e / jnp.sum(e, axis=-1, keepdims=True)
        outs.append(_bdot(p, v2))
    o_ref[0] = jnp.concatenate(outs, axis=1)


def _xattn_sample(q8, mem_k, mem_v, layer):
    bsz, rows, d = q8.shape
    _, _, n_mem, n_heads, dh = mem_k.shape
    tok = pl.BlockSpec((1, rows, d), lambda b: (b, 0, 0))
    mem = pl.BlockSpec((None, None, n_mem, n_heads, dh), lambda b: (layer, b, 0, 0, 0))
    return pl.pallas_call(
        _xattn_sample_kernel,
        grid=(bsz,),
        in_specs=[tok, mem, mem],
        out_specs=tok,
        out_shape=jax.ShapeDtypeStruct((bsz, rows, d), F32),
        compiler_params=_cparams("parallel"),
        name="xattn_sample",
    )(q8, mem_k, mem_v)


def _top2_sum(a, b, c, d):
    m1, n1 = jnp.maximum(a, b), jnp.minimum(a, b)
    m2, n2 = jnp.maximum(c, d), jnp.minimum(c, d)
    return jnp.maximum(m1, m2) + jnp.maximum(jnp.minimum(m1, m2), jnp.maximum(n1, n2))


def _router_kernel(x_ref, wt_ref, bias_ref, o_ref):
    x = x_ref[...]
    wt = wt_ref[...]
    xh, xl = _split(x)
    wh, wl = _split(wt)
    nt = (((1,), (1,)), ((), ()))
    logits = (lax.dot_general(wh, xh, nt, preferred_element_type=F32)
              + lax.dot_general(wh, xl, nt, preferred_element_type=F32)
              + lax.dot_general(wl, xh, nt, preferred_element_type=F32))
    s_all = _sigmoid(logits)
    sel_all = s_all + bias_ref[...]
    s = [s_all[e:e + 1, :] for e in range(N_EXPERTS)]
    sel = [sel_all[e:e + 1, :] for e in range(N_EXPERTS)]
    epg = EXPERTS_PER_GROUP
    score = [_top2_sum(*sel[g * epg:(g + 1) * epg]) for g in range(N_GROUPS)]
    best = functools.reduce(jnp.maximum, score)
    taken = None
    in_group = []
    for g in range(N_GROUPS):
        hit = score[g] == best
        if taken is None:
            pick, taken = hit, hit
        else:
            pick = hit & jnp.logical_not(taken)
            taken = taken | hit
        in_group.append(pick)
    chosen = []
    for g in range(N_GROUPS):
        vals = sel[g * epg:(g + 1) * epg]
        for j in range(epg):
            ahead = jnp.zeros_like(vals[j])
            for i in range(epg):
                if i == j:
                    continue
                before = (vals[i] >= vals[j]) if i < j else (vals[i] > vals[j])
                ahead = ahead + jnp.where(before, 1.0, 0.0)
            chosen.append(in_group[g] & (ahead < 1.5))
    gates = [jnp.where(chosen[e], s[e], 0.0) for e in range(N_EXPERTS)]
    total = functools.reduce(lambda a, b: a + b, gates)
    o_ref[...] = jnp.concatenate(gates, axis=0) / total


def _router(x, w_router_t, bias, tm):
    n, d = x.shape
    e = w_router_t.shape[0]
    return pl.pallas_call(
        _router_kernel,
        grid=(n // tm,),
        in_specs=[pl.BlockSpec((tm, d), lambda i: (i, 0)),
                  pl.BlockSpec((e, d), lambda i: (0, 0)),
                  pl.BlockSpec((e, 1), lambda i: (0, 0))],
        out_specs=pl.BlockSpec((e, tm), lambda i: (0, i)),
        out_shape=jax.ShapeDtypeStruct((e, n), F32),
        compiler_params=_cparams("parallel"),
        name="router",
    )(x, w_router_t, bias.reshape(e, 1))


def _moe_kernel(x_ref, comb_ref, wg_ref, wu_ref, wd_ref, g_ref, b_ref, o_ref, xb_ref, acc_ref):
    e = pl.program_id(1)

    @pl.when(e == 0)
    def _():
        xb_ref[...] = x_ref[...].astype(BF16)
        acc_ref[...] = jnp.zeros_like(acc_ref)

    xb = xb_ref[...]
    hg = jnp.dot(xb, wg_ref[0], preferred_element_type=F32)
    hu = jnp.dot(xb, wu_ref[0], preferred_element_type=F32)
    comb = comb_ref[...]
    gate = jnp.sum(jnp.where(_iota(comb.shape, 1) == e, comb, 0.0), axis=1, keepdims=True)
    act = _silu(hg) * hu * gate
    acc_ref[...] += _bdot(act, wd_ref[0])

    @pl.when(e == pl.num_programs(1) - 1)
    def _():
        o_ref[...] = _layer_norm(ALPHA * x_ref[...] + acc_ref[...], g_ref[...], b_ref[...])


def _moe(x, comb, wg, wu, wd, g, b, tm):
    n, d = x.shape
    ne, _, f = wg.shape
    return pl.pallas_call(
        _moe_kernel,
        grid=(n // tm, ne),
        in_specs=[pl.BlockSpec((tm, d), lambda i, e: (i, 0)),
                  pl.BlockSpec((tm, ne), lambda i, e: (i, 0)),
                  pl.BlockSpec((1, d, f), lambda i, e: (e, 0, 0)),
                  pl.BlockSpec((1, d, f), lambda i, e: (e, 0, 0)),
                  pl.BlockSpec((1, f, d), lambda i, e: (e, 0, 0)),
                  pl.BlockSpec((1, d), lambda i, e: (0, 0)),
                  pl.BlockSpec((1, d), lambda i, e: (0, 0))],
        out_specs=pl.BlockSpec((tm, d), lambda i, e: (i, 0)),
        out_shape=jax.ShapeDtypeStruct((n, d), F32),
        scratch_shapes=[pltpu.VMEM((tm, d), BF16), pltpu.VMEM((tm, d), F32)],
        compiler_params=_cparams("parallel", "arbitrary"),
        name="moe",
    )(x, comb, wg, wu, wd, g.reshape(1, d), b.reshape(1, d))


def _lanes(x):
    if x.ndim == 1:
        x = jnp.repeat(x, GROUP_LANES)
    return x.reshape(1, W_Q).astype(F32)


def _block_diag(w):
    g, a, b = w.shape
    eye = jnp.eye(g, dtype=w.dtype)
    return (eye[:, None, :, None] * w[:, :, None, :]).reshape(g * a, g * b)


def _tile(n, pref):
    t = min(n, pref)
    while n % t:
        t //= 2
    return t


def kernel(x_prompt, x_sample, cache_sb_k, cache_sb_v, state_pool, state_conv, state_delta,
           cache_mem_k, cache_mem_v, page_table, mem_prompt,
           ln0_g, ln0_b, w_in, sb_bias, gmlp_ln_g, gmlp_ln_b, gmlp_ws, gmlp_bs, pool_w, pool_gamma,
           conv_w, dn_a_log, dn_dt_bias, dn_norm_g, w_out, ln1_g, ln1_b,
           xa_wq, xa_wk, xa_wv, xa_wo, ln2_g, ln2_b, w_router, router_bias,
           ex_wg, ex_wu, ex_wd, ln3_g, ln3_b):
    bp, seq, d = x_prompt.shape
    bs, t_new, _ = x_sample.shape
    n_p = bp * seq
    n_s = bs * t_new
    n_mem = mem_prompt.shape[1]
    page = cache_sb_k.shape[2]
    past_len = page_table.shape[1] * page
    xa_dh = d // XA_HEADS
    tm_p = _tile(seq, 512)
    tm_s = _tile(n_s, 512)
    tm_moe_p = _tile(n_p, 1024)
    n_sub = 8 if seq % (8 * DELTA_CHUNK) == 0 else 1

    cache_kt = cache_sb_k.transpose(0, 1, 3, 4, 2)
    cache_vt = cache_sb_v.transpose(0, 1, 3, 4, 2)
    w_router_t = w_router.T
    mem_flat = mem_prompt.reshape(bp * n_mem, d)

    def rows8(x):
        x = x.reshape(bs, t_new, -1)
        return jnp.pad(x, ((0, 0), (0, SB_ROWS - t_new), (0, 0)))

    def tmajor(x):
        return x.reshape(bs, t_new, -1).transpose(1, 0, 2)

    def bmajor(x):
        return x.transpose(1, 0, 2)

    h_p = _ln(x_prompt.reshape(n_p, d), ln0_g, ln0_b, tm_p)
    h_s = _ln(x_sample.reshape(n_s, d), ln0_g, ln0_b, tm_s)

    outs = {k: [] for k in ("kp", "vp", "ks", "vs", "gv", "pp", "ps", "cp", "cs", "dp", "ds",
                            "mk", "mv")}
    for l in range(DEPTH):
        w_in_l = _prep_w_in(w_in[l])
        w_out_l = w_out[l].astype(BF16)
        w_out_parts = [w_out_l[i * W_Q:(i + 1) * W_Q] for i in range(4)]
        ws_l = gmlp_ws[l]
        bs_lanes = jnp.repeat(gmlp_bs[l].T, GROUP_LANES, axis=1)
        lng, lnb = _lanes(gmlp_ln_g[l]), _lanes(gmlp_ln_b[l])
        pw = _block_diag(pool_w[l]).astype(BF16)
        gamma = pool_gamma[l].reshape(1, W_Q)
        cw = conv_w[l]
        a_log, dt_b = _lanes(dn_a_log[l]), _lanes(dn_dt_bias[l])
        gn = jnp.tile(dn_norm_g[l], H_D).reshape(1, W_Q)
        wq, wo = xa_wq[l].astype(BF16), xa_wo[l].astype(BF16)
        wg, wu, wd = ex_wg[l].astype(BF16), ex_wu[l].astype(BF16), ex_wd[l].astype(BF16)

        mk, mk_b = _mm(mem_flat, xa_wk[l].astype(BF16), _tile(bp * n_mem, 256))
        mv, mv_b = _mm(mem_flat, xa_wv[l].astype(BF16), _tile(bp * n_mem, 256))
        outs["mk"].append(mk.reshape(bp, n_mem, XA_HEADS, xa_dh))
        outs["mv"].append(mv.reshape(bp, n_mem, XA_HEADS, xa_dh))

        (q, k, v, k_b, v_b, ub, vb, xc, qkv, zg, bd, ad) = _proj_in(h_p, w_in_l, tm_p)
        outs["kp"].append(k.reshape(bp, seq, H_A, DH_A))
        outs["vp"].append(v.reshape(bp, seq, H_A, DH_A))
        outs["pp"].append(xc.reshape(bp, seq, W_Q)[:, seq - POOL_BUF:])
        outs["cp"].append(qkv.reshape(bp, seq, 3 * W_Q)[:, seq - (CONV_K - 1):])
        oa = _sb_prompt(q, k_b, v_b, sb_bias[l], bp, seq)
        ob = _gmlp_prompt(ub, vb, ws_l, bs_lanes, lng, lnb, tm_p)
        oc = _pool_prompt(xc, pw, gamma, seq, tm_p)
        qd, kd, vd, gd, betad = _delta_prep(qkv, bd, ad, cw, a_log, dt_b, seq, tm_p)
        r3 = lambda x: x.reshape(bp, seq, W_Q)
        od, s_fin = _delta(r3(qd), r3(kd), r3(vd), r3(gd), r3(betad), r3(zg),
                           jnp.zeros((bp, H_D * DK_D, DK_D), F32), gn, n_sub)
        outs["dp"].append(s_fin.reshape(bp, H_D, DK_D, DK_D))
        h_p = _mm_ln([oa, ob, oc, od.reshape(n_p, W_Q)], w_out_parts, h_p, ln1_g[l], ln1_b[l], tm_p)
        h_p = _xattn_prompt(h_p, wq, wo, mk_b.reshape(bp, n_mem, d), mv_b.reshape(bp, n_mem, d),
                            ln2_g[l], ln2_b[l], seq, tm_p)
        comb = _router(h_p, w_router_t, router_bias, tm_p).T
        h_p = _moe(h_p, comb, wg, wu, wd, ln3_g[l], ln3_b[l], tm_moe_p)

        (q, k, v, _, _, ub, vb, xc, qkv, zg, bd, ad) = _proj_in(h_s, w_in_l, tm_s)
        outs["ks"].append(k.reshape(bs, t_new, H_A, DH_A))
        outs["vs"].append(v.reshape(bs, t_new, H_A, DH_A))
        oa = _sb_sample(rows8(q), rows8(k), rows8(v), cache_kt, cache_vt, page_table, sb_bias[l],
                        l)[:, :t_new].reshape(n_s, W_Q)
        wl = jnp.repeat(jnp.where(jnp.tril(jnp.ones((t_new, t_new), bool)),
                                  ws_l[:, :t_new, :t_new], 0.0).transpose(1, 2, 0),
                        GROUP_LANES, axis=2)
        (ob, vn, oc, pool_new, conv_new, qd, kd, vd, gd, betad) = _sample_mix(
            past_len, tmajor(ub), tmajor(vb), tmajor(xc), tmajor(qkv), tmajor(bd), tmajor(ad),
            bmajor(state_pool[l]), bmajor(state_conv[l]), wl, bs_lanes[:t_new], lng, lnb, pw,
            gamma, cw, a_log, dt_b)
        outs["gv"].append(bmajor(vn))
        outs["ps"].append(bmajor(pool_new))
        outs["cs"].append(bmajor(conv_new))
        od, s_fin = _delta_lanes(qd, kd, vd, gd, betad, tmajor(zg), gn,
                                 state_delta[l].transpose(1, 2, 3, 0))
        outs["ds"].append(s_fin.transpose(3, 0, 1, 2))
        flat = lambda x: bmajor(x).reshape(n_s, W_Q)
        h_s = _mm_ln([oa, flat(ob), flat(oc), flat(od)], w_out_parts, h_s, ln1_g[l], ln1_b[l],
                     tm_s)
        _, qx = _mm(h_s, wq, tm_s, scale=xa_dh ** -0.5)
        ox = _xattn_sample(rows8(qx), cache_mem_k, cache_mem_v, l)[:, :t_new].reshape(n_s, d)
        h_s = _mm_ln([ox], [wo], h_s, ln2_g[l], ln2_b[l], tm_s)
        comb = _router(h_s, w_router_t, router_bias, tm_s).T
        h_s = _moe(h_s, comb, wg, wu, wd, ln3_g[l], ln3_b[l], tm_s)

    st = lambda key: jnp.stack(outs[key])
    return (h_p.reshape(bp, seq, d), h_s.reshape(bs, t_new, d),
            st("kp"), st("vp"), st("ks"), st("vs"), st("gv"),
            st("pp"), st("ps"), st("cp"), st("cs"), st("dp"), st("ds"),
            st("mk"), st("mv"))
```

```python
import functools

import jax
import jax.numpy as jnp
from jax import lax
from jax.experimental import pallas as pl
from jax.experimental.pallas import tpu as pltpu

F32 = jnp.float32
BF16 = jnp.bfloat16

DEPTH = 2
H_A = 4
DH_A = 64
G_B = 4
GMLP_CHUNK = 128
POOL_WINDOWS = (2, 4, 8, 16)
POOL_BUF = 15
H_D = 4
DK_D = 64
CONV_K = 4
DELTA_CHUNK = 64
XA_HEADS = 4
N_EXPERTS = 16
N_GROUPS = 4
EXPERTS_PER_GROUP = 4
ALPHA = (2 * DEPTH) ** 0.25
LN_EPS = 1e-5
GROUP_LANES = 64
W_Q = 256

SB_BLOCK = 128
VMEM_LIMIT = 48 * 1024 * 1024


def _cparams(*sem):
    return pltpu.CompilerParams(dimension_semantics=sem, vmem_limit_bytes=VMEM_LIMIT)


def _bdot(a, b):
    return jnp.dot(a.astype(BF16), b.astype(BF16), preferred_element_type=F32)


def _bdot_nt(a, b):
    return lax.dot_general(a.astype(BF16), b.astype(BF16), (((1,), (1,)), ((), ())),
                           preferred_element_type=F32)


def _bdot_tn(a, b):
    return lax.dot_general(a.astype(BF16), b.astype(BF16), (((0,), (0,)), ((), ())),
                           preferred_element_type=F32)


def _split(x):
    hi = x.astype(BF16)
    lo = (x - hi.astype(F32)).astype(BF16)
    return hi, lo


def _dot_exact_rhs(x, c):
    hi, lo = _split(x)
    return (jnp.dot(hi, c, preferred_element_type=F32)
            + jnp.dot(lo, c, preferred_element_type=F32))


def _dot_exact_lhs(c, x):
    hi, lo = _split(x)
    return (jnp.dot(c, hi, preferred_element_type=F32)
            + jnp.dot(c, lo, preferred_element_type=F32))


def _dot3(a, b):
    ah, al = _split(a)
    bh, bl = _split(b)
    return (jnp.dot(ah, bh, preferred_element_type=F32)
            + jnp.dot(ah, bl, preferred_element_type=F32)
            + jnp.dot(al, bh, preferred_element_type=F32))


def _iota(shape, dim):
    return lax.broadcasted_iota(jnp.int32, shape, dim)


def _group_ones(n):
    return jnp.where((_iota((n, n), 0) >> 6) == (_iota((n, n), 1) >> 6), 1.0, 0.0).astype(BF16)


def _group_sum(x, ones):
    return _dot_exact_rhs(x, ones)


def _layer_norm(x, g, b):
    mu = jnp.mean(x, axis=-1, keepdims=True)
    xc = x - mu
    var = jnp.mean(xc * xc, axis=-1, keepdims=True)
    return xc * lax.rsqrt(var + LN_EPS) * g + b


def _sigmoid(x):
    return 1.0 / (1.0 + jnp.exp(-x))


def _silu(x):
    return x * _sigmoid(x)


def _softplus(x):
    return jnp.maximum(x, 0.0) + jnp.log(1.0 + jnp.exp(-jnp.abs(x)))


def _gelu_tanh(x):
    return 0.5 * x * (1.0 + jnp.tanh(0.7978845608028654 * (x + 0.044715 * (x * x * x))))


def _group_layer_norm(x, ones, g, b):
    mu = _group_sum(x, ones) * (1.0 / GROUP_LANES)
    xc = x - mu
    var = _group_sum(xc * xc, ones) * (1.0 / GROUP_LANES)
    return xc * lax.rsqrt(var + LN_EPS) * g + b


def _l2_normalize(x, ones):
    return x * lax.rsqrt(_group_sum(x * x, ones) + 1e-6)


def _ln_kernel(x_ref, g_ref, b_ref, o_ref):
    o_ref[...] = _layer_norm(x_ref[...], g_ref[...], b_ref[...])


def _ln(x, g, b, tm):
    n, d = x.shape
    return pl.pallas_call(
        _ln_kernel,
        grid=(n // tm,),
        in_specs=[pl.BlockSpec((tm, d), lambda i: (i, 0)),
                  pl.BlockSpec((1, d), lambda i: (0, 0)),
                  pl.BlockSpec((1, d), lambda i: (0, 0))],
        out_specs=pl.BlockSpec((tm, d), lambda i: (i, 0)),
        out_shape=jax.ShapeDtypeStruct((n, d), F32),
        compiler_params=_cparams("parallel"),
        name="ln0",
    )(x, g.reshape(1, d), b.reshape(1, d))


_SEG = {"q": (0, 256), "k": (256, 512), "v": (512, 768), "ub": (768, 1024), "vb": (1024, 1280),
        "xc": (1280, 1536), "qkv": (1536, 2304), "zg": (2304, 2560), "bd": (2560, 2816),
        "ad": (2816, 3072)}
N_IN_PAD = 3072


def _proj_in_kernel(x_ref, w_ref, q_ref, k_ref, v_ref, kb_ref, vb_ref, ub_ref, vbb_ref, xc_ref,
                    qkv_ref, zg_ref, bd_ref, ad_ref):
    x = x_ref[...].astype(BF16)

    def seg(name):
        a, b = _SEG[name]
        return jnp.dot(x, w_ref[:, a:b], preferred_element_type=F32)

    first_head = _iota((x.shape[0], SB_BLOCK), 1) < GROUP_LANES

    def head_stacked(y):
        yb = y.astype(BF16)
        zero = jnp.zeros((x.shape[0], SB_BLOCK), BF16)
        parts = []
        for p in range(H_A // 2):
            yp = yb[:, p * SB_BLOCK:(p + 1) * SB_BLOCK]
            parts += [jnp.where(first_head, yp, zero), jnp.where(first_head, zero, yp)]
        return jnp.concatenate(parts, axis=1)

    q_ref[...] = (seg("q") * (DH_A ** -0.5)).astype(BF16)
    k = seg("k")
    k_ref[...] = k
    kb_ref[...] = head_stacked(k)
    v = seg("v")
    v_ref[...] = v
    vb_ref[...] = head_stacked(v)
    ub_ref[...] = seg("ub")
    vbb_ref[...] = seg("vb")
    xc_ref[...] = seg("xc")
    qkv_ref[...] = seg("qkv")
    zg_ref[...] = seg("zg")
    bd_ref[...] = seg("bd")
    ad_ref[...] = seg("ad")


def _proj_in(x, w, tm):
    n, d = x.shape
    widths = [(256, BF16), (256, F32), (256, F32), (512, BF16), (512, BF16), (256, F32), (256, F32),
              (256, F32), (768, F32), (256, F32), (256, F32), (256, F32)]
    return pl.pallas_call(
        _proj_in_kernel,
        grid=(n // tm,),
        in_specs=[pl.BlockSpec((tm, d), lambda i: (i, 0)),
                  pl.BlockSpec((d, N_IN_PAD), lambda i: (0, 0))],
        out_specs=[pl.BlockSpec((tm, wd), lambda i: (i, 0)) for wd, _ in widths],
        out_shape=[jax.ShapeDtypeStruct((n, wd), dt) for wd, dt in widths],
        compiler_params=_cparams("parallel"),
        name="proj_in",
    )(x, w)


def _prep_w_in(w_in_l):
    main = w_in_l[:, :2560]
    bd = jnp.repeat(w_in_l[:, 2560:2564], GROUP_LANES, axis=1)
    ad = jnp.repeat(w_in_l[:, 2564:2568], GROUP_LANES, axis=1)
    return jnp.concatenate([main, bd, ad], axis=1).astype(BF16)


NEG_BIG = -1e30


LOG2E = 1.4426950408889634


def _pair_rows(x, p):
    w = 2 * SB_BLOCK
    return jnp.concatenate([x[:, p * w:p * w + SB_BLOCK], x[:, p * w + SB_BLOCK:(p + 1) * w]], axis=0)


def _suffix_ones_hilo():
    nb = SB_BLOCK
    r = lax.broadcasted_iota(jnp.int32, (2 * nb, 2 * nb), 0) & (nb - 1)
    c = lax.broadcasted_iota(jnp.int32, (2 * nb, 2 * nb), 1)
    return jnp.where((c >= nb) | (r > c), 1.0, 0.0).astype(BF16)


SB_QROWS = 2 * SB_BLOCK


def _sb_prompt_kernel(bias_ref, q_ref, k_ref, v_ref, uo_ref, o_ref, carry_ref, acc_ref,
                      z_a, att_a, z_b, att_b):
    i = pl.program_id(1)
    nb = SB_BLOCK
    nq = SB_QROWS
    pw = 2 * nb
    last = (nq // nb) * (i + 1) - 1
    head_of_lane = _iota((1, H_A * nb), 1) >> 7
    bias_row = jnp.zeros((1, H_A * nb), F32)
    for h in range(H_A):
        bias_row = jnp.where(head_of_lane == h, bias_ref[h], bias_row)
    col_minus_row = _iota((nq, nb), 1) - _iota((nq, nb), 0)

    acc_ref[...] = jnp.zeros_like(acc_ref)
    carry_ref[...] = jnp.zeros_like(carry_ref)
    z_b[...] = jnp.zeros_like(z_b)
    att_b[...] = jnp.zeros_like(att_b)

    def stages(t, z_w, att_w, z_r, att_r, masked):
        ja = jnp.maximum(last - t, 0)
        k = k_ref[pl.ds(pl.multiple_of(ja * nb, nb), nb), :]
        for p in range(2):
            z_w[:, p * pw:(p + 1) * pw] = lax.dot_general(
                q_ref[:, p * nb:(p + 1) * nb], _pair_rows(k, p), (((1,), (1,)), ((), ())),
                preferred_element_type=F32)
        z = z_r[...] + bias_row
        l = jnp.log(1.0 + jnp.exp2(jnp.abs(z) * (-LOG2E)))
        nlk = jnp.maximum(z, 0.0) + l
        lb = z - nlk
        if masked:
            tb = t - 1
            span = jnp.where((tb >= 0) & (tb <= last), (tb - (nq // nb - 1)) * nb, -2 * nq)
            valid = col_minus_row < span
            nlk = nlk * jnp.concatenate([jnp.where(valid, 1.0, 0.0)] * H_A, axis=1)
            lb = lb + jnp.concatenate([jnp.where(valid, 0.0, NEG_BIG)] * H_A, axis=1)
        hi, lo = _split(nlk)
        res = [jnp.dot(jnp.concatenate([hi[:, h * nb:(h + 1) * nb], lo[:, h * nb:(h + 1) * nb]],
                                       axis=1), uo_ref[...], preferred_element_type=F32)
               for h in range(H_A)]
        right = jnp.concatenate([x[:, :nb] for x in res], axis=1)
        total = jnp.concatenate([x[:, nb:] for x in res], axis=1)
        carry = carry_ref[...]
        att_w[...] = jnp.exp(lb - right - carry).astype(BF16)
        carry_ref[...] = carry + total
        jc = jnp.clip(last - t + 2, 0, last)
        v = v_ref[pl.ds(pl.multiple_of(jc * nb, nb), nb), :]
        for p in range(2):
            acc_ref[:, p * nb:(p + 1) * nb] += jnp.dot(att_r[:, p * pw:(p + 1) * pw],
                                                       _pair_rows(v, p),
                                                       preferred_element_type=F32)

    def four_trips(masked, u, carry_unused):
        stages(4 * u, z_a, att_a, z_b, att_b, masked)
        stages(4 * u + 1, z_b, att_b, z_a, att_a, masked)
        stages(4 * u + 2, z_a, att_a, z_b, att_b, masked)
        stages(4 * u + 3, z_b, att_b, z_a, att_a, masked)
        return carry_unused

    four_trips(True, 0, 0)
    n_plain = jnp.maximum((last - 2) // 4, 0)
    lax.fori_loop(1, n_plain + 1, functools.partial(four_trips, False), 0)
    lax.fori_loop(n_plain + 1, (last + 6) // 4, functools.partial(four_trips, True), 0)
    o_ref[...] = acc_ref[...]


def _sb_prompt(q, k, v, bias, batch, seq):
    n = q.shape[0]
    nq = seq // SB_QROWS
    hw = H_A * SB_BLOCK
    stage_bufs = [pltpu.VMEM((SB_QROWS, hw), F32),
                  pltpu.VMEM((SB_QROWS, hw), BF16)]
    return pl.pallas_call(
        _sb_prompt_kernel,
        grid=(batch, nq),
        in_specs=[pl.BlockSpec(memory_space=pltpu.SMEM),
                  pl.BlockSpec((SB_QROWS, W_Q), lambda b, i: (b * nq + i, 0)),
                  pl.BlockSpec((seq, hw), lambda b, i: (b, 0)),
                  pl.BlockSpec((seq, hw), lambda b, i: (b, 0)),
                  pl.BlockSpec((2 * SB_BLOCK, 2 * SB_BLOCK), lambda b, i: (0, 0))],
        out_specs=pl.BlockSpec((SB_QROWS, W_Q), lambda b, i: (b * nq + i, 0)),
        scratch_shapes=[pltpu.VMEM((SB_QROWS, hw), F32),
                        pltpu.VMEM((SB_QROWS, W_Q), F32)] + stage_bufs * 2,
        out_shape=jax.ShapeDtypeStruct((n, W_Q), F32),
        compiler_params=_cparams("parallel", "arbitrary"),
        name="sb_prompt",
    )(bias, q, k, v, _suffix_ones_hilo())


SB_ROWS = 8


def _sb_sample_kernel(n_pages, pt_ref, bias_ref, q_ref, kn_ref, vn_ref, uo_ref, *rest):
    kpages = rest[:n_pages]
    vpages = rest[n_pages:2 * n_pages]
    o_ref = rest[2 * n_pages]
    m = H_A * SB_ROWS
    row = _iota((m, W_Q), 0)
    lane = _iota((m, W_Q), 1)
    own = (row >> 3) == (lane >> 6)
    qbd = jnp.where(own, jnp.concatenate([q_ref[0]] * H_A, axis=0), jnp.zeros((m, W_Q), BF16))
    nblk = n_pages + 1
    width = nblk * SB_BLOCK
    rowh = _iota((m, width), 0) >> 3
    key = _iota((m, width), 1)
    bias = jnp.zeros((m, width), F32)
    for h in range(H_A):
        bias = jnp.where(rowh == h, bias_ref[h], bias)
    visible = (key < n_pages * SB_BLOCK) | ((key - n_pages * SB_BLOCK) < (_iota((m, width), 0) & (SB_ROWS - 1)))

    def page_t(ref):
        return ref[...].reshape(W_Q, SB_BLOCK).astype(BF16)

    pad = jnp.zeros((SB_BLOCK - SB_ROWS, W_Q), F32)
    kn = jnp.concatenate([kn_ref[0], pad], axis=0)
    vn = jnp.concatenate([vn_ref[0], pad], axis=0)
    kt_all = jnp.concatenate([page_t(r) for r in kpages], axis=1)
    z = jnp.concatenate([jnp.dot(qbd, kt_all, preferred_element_type=F32), _bdot_nt(qbd, kn)],
                        axis=1) + bias
    l = jnp.log(1.0 + jnp.exp2(jnp.abs(z) * (-LOG2E)))
    nlk = jnp.maximum(z, 0.0) + l
    lb = jnp.where(visible, z - nlk, NEG_BIG)
    hi, lo = _split(jnp.where(visible, nlk, 0.0))
    blocks = [slice(p * SB_BLOCK, (p + 1) * SB_BLOCK) for p in range(nblk)]
    sums = jnp.dot(jnp.concatenate([jnp.concatenate([hi[:, s], lo[:, s]], axis=1) for s in blocks],
                                   axis=0), uo_ref[...], preferred_element_type=F32)
    right = jnp.concatenate([sums[p * m:(p + 1) * m, :SB_BLOCK] for p in range(nblk)], axis=1)
    carry = jnp.zeros((m, SB_BLOCK), F32)
    carries = [None] * nblk
    for p in range(nblk - 1, -1, -1):
        carries[p] = carry
        carry = carry + sums[p * m:(p + 1) * m, SB_BLOCK:]
    att = jnp.exp(lb - right - jnp.concatenate(carries, axis=1)).astype(BF16)
    vt_all = jnp.concatenate([page_t(r) for r in vpages], axis=1)
    acc = (lax.dot_general(att[:, :n_pages * SB_BLOCK], vt_all, (((1,), (1,)), ((), ())),
                           preferred_element_type=F32)
           + _bdot(att[:, n_pages * SB_BLOCK:], vn))
    acc = jnp.where(own, acc, 0.0)
    out = acc[0:SB_ROWS]
    for h in range(1, H_A):
        out = out + acc[h * SB_ROWS:(h + 1) * SB_ROWS]
    o_ref[0] = out


def _sb_sample(q8, k8, v8, cache_kt, cache_vt, page_table, bias, layer):
    bsz, n_pages = page_table.shape
    page = cache_kt.shape[4]

    def page_spec(p):
        return pl.BlockSpec((None, None, H_A, DH_A, page), lambda b, pt: (layer, pt[b, p], 0, 0, 0))

    tok = pl.BlockSpec((1, SB_ROWS, W_Q), lambda b, pt: (b, 0, 0))
    return pl.pallas_call(
        functools.partial(_sb_sample_kernel, n_pages),
        grid_spec=pltpu.PrefetchScalarGridSpec(
            num_scalar_prefetch=1,
            grid=(bsz,),
            in_specs=([pl.BlockSpec(memory_space=pltpu.SMEM), tok, tok, tok,
                       pl.BlockSpec((2 * SB_BLOCK, 2 * SB_BLOCK), lambda b, pt: (0, 0))]
                      + [page_spec(p) for p in range(n_pages)] * 2),
            out_specs=tok),
        out_shape=jax.ShapeDtypeStruct((bsz, SB_ROWS, W_Q), F32),
        compiler_params=_cparams("parallel"),
        name="sb_sample",
    )(page_table, bias, q8, k8, v8, _suffix_ones_hilo(), *([cache_kt] * n_pages),
      *([cache_vt] * n_pages))


def _gmlp_prompt_kernel(u_ref, v_ref, ws_ref, bs_ref, g_ref, b_ref, o_ref):
    c = GMLP_CHUNK
    ones = _group_ones(W_Q)
    tril = _iota((c, c), 0) >= _iota((c, c), 1)
    lane_group = _iota((c, W_Q), 1) >> 6
    w = [jnp.where(tril, ws_ref[g], 0.0).astype(BF16) for g in range(G_B)]
    for n in range(u_ref.shape[0] // c):
        rows = slice(n * c, (n + 1) * c)
        u = _gelu_tanh(u_ref[rows, :])
        vn = _group_layer_norm(_gelu_tanh(v_ref[rows, :]), ones, g_ref[...], b_ref[...])
        vnb = vn.astype(BF16)
        mixed = bs_ref[...]
        for g in range(G_B):
            vg = jnp.where(lane_group == g, vnb, jnp.zeros_like(vnb))
            mixed = mixed + jnp.dot(w[g], vg, preferred_element_type=F32)
        o_ref[rows, :] = u * mixed


def _gmlp_prompt(ub, vb, ws, bs_lanes, g, b, tm):
    n = ub.shape[0]
    c = GMLP_CHUNK
    tile = pl.BlockSpec((tm, W_Q), lambda i: (i, 0))
    vec = pl.BlockSpec((1, W_Q), lambda i: (0, 0))
    return pl.pallas_call(
        _gmlp_prompt_kernel,
        grid=(n // tm,),
        in_specs=[tile, tile,
                  pl.BlockSpec((G_B, c, c), lambda i: (0, 0, 0)),
                  pl.BlockSpec((c, W_Q), lambda i: (0, 0)),
                  vec, vec],
        out_specs=tile,
        out_shape=jax.ShapeDtypeStruct((n, W_Q), F32),
        compiler_params=_cparams("parallel"),
        name="gmlp_prompt",
    )(ub, vb, ws, bs_lanes, g, b)


HALO = 16


def _pool_windows(load, t, start_pos):
    lane = _iota((t, SB_BLOCK), 1)
    pos = start_pos + _iota((t, SB_BLOCK), 0)
    halves = []
    for half, (w_small, w_big) in enumerate(((POOL_WINDOWS[0], POOL_WINDOWS[1]),
                                              (POOL_WINDOWS[2], POOL_WINDOWS[3]))):
        s = load(0, half)
        for i in range(1, w_small):
            s = s + load(i, half)
        big = s
        for i in range(w_small, w_big):
            big = big + load(i, half)
        small_lane = lane < GROUP_LANES
        win = jnp.where(small_lane, s, big)
        w = jnp.where(small_lane, w_small, w_big)
        cnt = jnp.minimum(pos + 1, w).astype(F32)
        halves.append(win / cnt)
    return jnp.concatenate(halves, axis=1)


def _pool_prompt_kernel(tiles_per_seq, x_ref, halo_ref, w_ref, gamma_ref, o_ref, xx_ref):
    i = pl.program_id(0)
    t = x_ref.shape[0]
    first = (i % tiles_per_seq) == 0
    xx_ref[0:HALO, :] = jnp.where(first, 0.0, halo_ref[...])
    xx_ref[HALO:, :] = x_ref[...]

    def load(shift, half):
        return xx_ref[pl.ds(HALO - shift, t), half * SB_BLOCK:(half + 1) * SB_BLOCK]

    start = (i % tiles_per_seq) * t
    means = _pool_windows(load, t, start)
    d = means - x_ref[...]
    o_ref[...] = _bdot(d, w_ref[...]) * gamma_ref[...]


def _pool_prompt(xc, w_bd, gamma, seq, tm):
    n = xc.shape[0]
    hb = tm // HALO
    return pl.pallas_call(
        functools.partial(_pool_prompt_kernel, seq // tm),
        grid=(n // tm,),
        in_specs=[pl.BlockSpec((tm, W_Q), lambda i: (i, 0)),
                  pl.BlockSpec((HALO, W_Q), lambda i: (jnp.maximum(i * hb - 1, 0), 0)),
                  pl.BlockSpec((W_Q, W_Q), lambda i: (0, 0)),
                  pl.BlockSpec((1, W_Q), lambda i: (0, 0))],
        out_specs=pl.BlockSpec((tm, W_Q), lambda i: (i, 0)),
        out_shape=jax.ShapeDtypeStruct((n, W_Q), F32),
        scratch_shapes=[pltpu.VMEM((tm + HALO, W_Q), F32)],
        compiler_params=_cparams("parallel"),
        name="pool_prompt",
    )(xc, xc, w_bd, gamma)


CONV_HALO = 8


def _delta_gates(bd, ad, a_log, dt_bias):
    beta = _sigmoid(bd)
    g = -jnp.exp(a_log) * _softplus(ad + dt_bias)
    return beta, g


def _delta_prep_kernel(tiles_per_seq, x_ref, halo_ref, bd_ref, ad_ref, cw_ref, alog_ref, dt_ref,
                       q_ref, k_ref, v_ref, g_ref, b_ref, xx_ref):
    i = pl.program_id(0)
    t = x_ref.shape[0]
    first = (i % tiles_per_seq) == 0
    xx_ref[0:CONV_HALO, :] = jnp.where(first, 0.0, halo_ref[...])
    xx_ref[CONV_HALO:, :] = x_ref[...]
    ones = _group_ones(W_Q)
    off = CONV_HALO - (CONV_K - 1)
    for part, out in enumerate((q_ref, k_ref, v_ref)):
        lanes = slice(part * W_Q, (part + 1) * W_Q)
        y = xx_ref[pl.ds(off, t), lanes] * cw_ref[0:1, lanes]
        for j in range(1, CONV_K):
            y = y + xx_ref[pl.ds(off + j, t), lanes] * cw_ref[j:j + 1, lanes]
        y = _silu(y)
        out[...] = y if part == 2 else _l2_normalize(y, ones)
    beta, g = _delta_gates(bd_ref[...], ad_ref[...], alog_ref[...], dt_ref[...])
    b_ref[...] = beta
    g_ref[...] = g


def _delta_prep(qkv, bd, ad, conv_w, a_log, dt_bias, seq, tm):
    n, wd = qkv.shape
    hb = tm // CONV_HALO
    tile = pl.BlockSpec((tm, W_Q), lambda i: (i, 0))
    vec = pl.BlockSpec((1, W_Q), lambda i: (0, 0))
    return pl.pallas_call(
        functools.partial(_delta_prep_kernel, seq // tm),
        grid=(n // tm,),
        in_specs=[pl.BlockSpec((tm, wd), lambda i: (i, 0)),
                  pl.BlockSpec((CONV_HALO, wd), lambda i: (jnp.maximum(i * hb - 1, 0), 0)),
                  tile, tile,
                  pl.BlockSpec((CONV_K, wd), lambda i: (0, 0)),
                  vec, vec],
        out_specs=[tile] * 5,
        out_shape=[jax.ShapeDtypeStruct((n, W_Q), F32)] * 5,
        scratch_shapes=[pltpu.VMEM((tm + CONV_HALO, wd), F32)],
        compiler_params=_cparams("parallel"),
        name="delta_prep",
    )(qkv, qkv, bd, ad, conv_w, a_log, dt_bias)


def _delta_kernel(n_sub, q_ref, k_ref, v_ref, g_ref, b_ref, zg_ref, s0_ref, gn_ref, o_ref, sf_ref,
                  s_ref):
    c = pl.program_id(1)
    n_steps = pl.num_programs(1)
    cs = DELTA_CHUNK
    hw = H_D * DK_D
    row = _iota((cs, hw), 0)
    col = _iota((cs, hw), 1) & (DK_D - 1)
    incl = row >= col
    strict = row > col
    bdmask = (_iota((hw, hw), 0) >> 6) == (_iota((hw, hw), 1) >> 6)
    ones = _group_ones(hw)
    tril = jnp.where(_iota((cs, cs), 0) >= _iota((cs, cs), 1), 1.0, 0.0).astype(BF16)

    def bd(x):
        return jnp.where(bdmask, jnp.concatenate([x] * H_D, axis=0), 0.0)

    @pl.when(c == 0)
    def _():
        s_ref[...] = jnp.where(bdmask, jnp.concatenate([s0_ref[0]] * H_D, axis=1), 0.0)

    chunks = [slice(n * cs, (n + 1) * cs) for n in range(n_sub)]
    each = lambda f, *xs: [f(*x) for x in zip(*xs)]
    q = [q_ref[0, r, :] * (DK_D ** -0.5) for r in chunks]
    k = [k_ref[0, r, :] for r in chunks]
    g = [g_ref[0, r, :] for r in chunks]
    beta = [b_ref[0, r, :] for r in chunks]
    vb = [v_ref[0, r, :] * bt for r, bt in zip(chunks, beta)]
    kb = each(lambda x, bt: x * bt, k, beta)
    gc = [_dot_exact_lhs(tril, x) for x in g]
    diff = [_dot_exact_lhs(tril, jnp.where(strict, x, 0.0)) for x in g]
    decay = [jnp.where(incl, jnp.exp(jnp.where(incl, x, 0.0)), 0.0) for x in diff]
    eg = [jnp.exp(x) for x in gc]
    kbd = [bd(x).astype(BF16) for x in k]
    neg_lower = each(lambda x, y, dc: jnp.where(strict, -(_bdot_nt(x, y) * dc), 0.0), kb, kbd, decay)
    eye = jnp.where(row == col, 1.0, 0.0)
    t_inv = [eye + x for x in neg_lower]
    p = neg_lower
    for _ in range(5):
        p = [_dot3(x, bd(x)) for x in p]
        t_inv = each(lambda t, x: t + _dot3(t, bd(x)), t_inv, p)
    u = each(lambda t, x: _bdot(t, bd(x)), t_inv, vb)
    w = each(lambda t, x, e: _bdot(t, bd(x * e)), t_inv, kb, eg)
    intra = each(lambda x, y, dc: jnp.where(incl, _bdot_nt(x, y) * dc, 0.0), q, kbd, decay)
    gc_last = [x[cs - 1:cs, :] for x in gc]
    qg = each(lambda x, e: x * e, q, eg)
    kg = each(lambda x, gl, c_: x * jnp.exp(gl - c_), k, gc_last, gc)
    solved = list(zip(u, w, intra, qg, kg, [jnp.exp(x) for x in gc_last]))
    s = s_ref[...]
    for n, (u, w, intra, qg, kg, g_last) in enumerate(solved):
        rows = slice(n * cs, (n + 1) * cs)
        v_new = u - _bdot(w, s)
        o = _bdot(qg, s) + _bdot(intra, bd(v_new))
        s = s * g_last + jnp.where(bdmask, _bdot_tn(kg, v_new), 0.0)
        o = o * lax.rsqrt(_group_sum(o * o, ones) * (1.0 / DK_D) + 1e-6) * gn_ref[...]
        o_ref[0, rows, :] = o * _silu(zg_ref[0, rows, :])
    s_ref[...] = s

    @pl.when(c == n_steps - 1)
    def _():
        sf = s[:, 0:DK_D]
        for h in range(1, H_D):
            sf = sf + s[:, h * DK_D:(h + 1) * DK_D]
        sf_ref[0] = sf


def _delta(q, k, v, g, beta, zg, s0, gn, n_sub):
    bsz, t, hw = q.shape
    rows = n_sub * DELTA_CHUNK
    tok = pl.BlockSpec((1, rows, hw), lambda b, c: (b, c, 0))
    st = pl.BlockSpec((1, hw, DK_D), lambda b, c: (b, 0, 0))
    return pl.pallas_call(
        functools.partial(_delta_kernel, n_sub),
        grid=(bsz, t // rows),
        in_specs=[tok] * 6 + [st, pl.BlockSpec((1, hw), lambda b, c: (0, 0))],
        out_specs=[tok, st],
        out_shape=[jax.ShapeDtypeStruct((bsz, t, hw), F32),
                   jax.ShapeDtypeStruct((bsz, hw, DK_D), F32)],
        scratch_shapes=[pltpu.VMEM((hw, hw), F32)],
        compiler_params=_cparams("parallel", "arbitrary"),
        name="delta",
    )(q, k, v, g, beta, zg, s0, gn)


def _delta_lanes_kernel(q_ref, k_ref, v_ref, g_ref, b_ref, zg_ref, gn_ref, s0_ref, o_ref, sf_ref,
                        qt_ref, kt_ref):
    t_new, bsz, hw = q_ref.shape
    ones = _group_ones(hw)
    sf_ref[...] = s0_ref[...]
    zero = jnp.zeros((DK_D, bsz), F32)
    for t in range(t_new):
        qt_ref[...] = jnp.transpose(q_ref[t]) * (DK_D ** -0.5)
        kt_ref[...] = jnp.transpose(k_ref[t])
        vt = jnp.transpose(v_ref[t])
        at = jnp.exp(jnp.transpose(g_ref[t]))
        bt = jnp.transpose(b_ref[t])
        outs = []
        for h in range(H_D):
            base = h * DK_D
            a = at[base:base + 1, :]
            beta = bt[base:base + 1, :]

            def k_dot_s(dk, acc, h=h, base=base):
                return acc + kt_ref[pl.ds(base + dk, 1), :] * sf_ref[h, dk]

            ks = lax.fori_loop(0, DK_D, k_dot_s, zero, unroll=8)
            delta = beta * (vt[base:base + DK_D, :] - a * ks)

            def update(dk, acc, h=h, base=base, a=a, delta=delta):
                s_new = a * sf_ref[h, dk] + kt_ref[pl.ds(base + dk, 1), :] * delta
                sf_ref[h, dk] = s_new
                return acc + qt_ref[pl.ds(base + dk, 1), :] * s_new

            outs.append(lax.fori_loop(0, DK_D, update, zero, unroll=8))
        o = jnp.transpose(jnp.concatenate(outs, axis=0))
        o = o * lax.rsqrt(_group_sum(o * o, ones) * (1.0 / DK_D) + 1e-6) * gn_ref[...]
        o_ref[t] = o * _silu(zg_ref[t])


def _delta_lanes(q, k, v, g, beta, zg, gn, s0):
    t_new, bsz, hw = q.shape
    return pl.pallas_call(
        _delta_lanes_kernel,
        out_shape=[jax.ShapeDtypeStruct((t_new, bsz, hw), F32),
                   jax.ShapeDtypeStruct(s0.shape, F32)],
        scratch_shapes=[pltpu.VMEM((hw, bsz), F32), pltpu.VMEM((hw, bsz), F32)],
        compiler_params=pltpu.CompilerParams(vmem_limit_bytes=VMEM_LIMIT),
        name="delta_lanes",
    )(q, k, v, g, beta, zg, gn, s0)


def _sample_mix_kernel(start_pos, u_ref, v_ref, xc_ref, qkv_ref, bd_ref, ad_ref, pool_prev_ref,
                       conv_prev_ref, wl_ref, bsl_ref, lng_ref, lnb_ref, pw_ref, gamma_ref, cw_ref,
                       alog_ref, dt_ref,
                       ob_ref, vn_ref, oc_ref, pool_new_ref, conv_new_ref, q_ref, k_ref, vd_ref,
                       g_ref, b_ref):
    t_new = u_ref.shape[0]
    bsz = u_ref.shape[1]
    ones = _group_ones(W_Q)
    vn = [_group_layer_norm(_gelu_tanh(v_ref[t]), ones, lng_ref[...], lnb_ref[...])
          for t in range(t_new)]
    for t in range(t_new):
        vn_ref[t] = vn[t]
        mixed = bsl_ref[t:t + 1, :] + wl_ref[t, 0:1, :] * vn[0]
        for s in range(1, t + 1):
            mixed = mixed + wl_ref[t, s:s + 1, :] * vn[s]
        ob_ref[t] = _gelu_tanh(u_ref[t]) * mixed
    n_prev = pool_prev_ref.shape[0]

    def pool_row(i):
        return pool_prev_ref[i] if i < n_prev else xc_ref[i - n_prev]

    lane = _iota((bsz, W_Q), 1) >> 6
    for t in range(t_new):
        acc = pool_row(n_prev + t)
        win = None
        done = 1
        for gi, wdw in enumerate(POOL_WINDOWS):
            for i in range(done, wdw):
                acc = acc + pool_row(n_prev + t - i)
            done = wdw
            cnt = float(min(start_pos + t + 1, wdw))
            mean = acc / cnt
            win = mean if win is None else jnp.where(lane >= gi, mean, win)
        d = win - xc_ref[t]
        oc_ref[t] = _bdot(d, pw_ref[...]) * gamma_ref[...]
    for i in range(n_prev):
        pool_new_ref[i] = pool_row(t_new + i)
    c_prev = conv_prev_ref.shape[0]

    def conv_row(i):
        return conv_prev_ref[i] if i < c_prev else qkv_ref[i - c_prev]

    for t in range(t_new):
        y = conv_row(t) * cw_ref[0:1, :]
        for j in range(1, CONV_K):
            y = y + conv_row(t + j) * cw_ref[j:j + 1, :]
        y = _silu(y)
        q_ref[t] = _l2_normalize(y[:, 0:W_Q], ones)
        k_ref[t] = _l2_normalize(y[:, W_Q:2 * W_Q], ones)
        vd_ref[t] = y[:, 2 * W_Q:3 * W_Q]
        beta, g = _delta_gates(bd_ref[t], ad_ref[t], alog_ref[...], dt_ref[...])
        b_ref[t] = beta
        g_ref[t] = g
    for i in range(c_prev):
        conv_new_ref[i] = conv_row(t_new + i)


def _sample_mix(start_pos, ub, vb, xc, qkv, bd, ad, pool_prev, conv_prev, wl, bsl, lng, lnb, pw,
                gamma, cw, a_log, dt_bias):
    t_new, bsz, _ = ub.shape
    slab = jax.ShapeDtypeStruct((t_new, bsz, W_Q), F32)
    out_shape = [slab, slab, slab,
                 jax.ShapeDtypeStruct(pool_prev.shape, F32),
                 jax.ShapeDtypeStruct(conv_prev.shape, F32),
                 slab, slab, slab, slab, slab]
    return pl.pallas_call(
        functools.partial(_sample_mix_kernel, start_pos),
        out_shape=out_shape,
        compiler_params=pltpu.CompilerParams(vmem_limit_bytes=VMEM_LIMIT),
        name="sample_mix",
    )(ub, vb, xc, qkv, bd, ad, pool_prev, conv_prev, wl, bsl, lng, lnb, pw, gamma, cw, a_log,
      dt_bias)


def _mm_ln_kernel(n_in, *refs):
    xs = refs[:n_in]
    ws = refs[n_in:2 * n_in]
    h_ref, g_ref, b_ref, o_ref = refs[2 * n_in:]
    acc = ALPHA * h_ref[...]
    for x_ref, w_ref in zip(xs, ws):
        acc = acc + _bdot(x_ref[...], w_ref[...])
    o_ref[...] = _layer_norm(acc, g_ref[...], b_ref[...])


def _mm_ln(xs, ws, h, g, b, tm):
    n, d = h.shape
    n_in = len(xs)
    return pl.pallas_call(
        functools.partial(_mm_ln_kernel, n_in),
        grid=(n // tm,),
        in_specs=([pl.BlockSpec((tm, x.shape[1]), lambda i: (i, 0)) for x in xs]
                  + [pl.BlockSpec(w.shape, lambda i: (0, 0)) for w in ws]
                  + [pl.BlockSpec((tm, d), lambda i: (i, 0)),
                     pl.BlockSpec((1, d), lambda i: (0, 0)),
                     pl.BlockSpec((1, d), lambda i: (0, 0))]),
        out_specs=pl.BlockSpec((tm, d), lambda i: (i, 0)),
        out_shape=jax.ShapeDtypeStruct((n, d), F32),
        compiler_params=_cparams("parallel"),
        name="mm_ln",
    )(*xs, *ws, h, g.reshape(1, d), b.reshape(1, d))


def _mm_kernel(scale, x_ref, w_ref, o_ref, ob_ref):
    y = _bdot(x_ref[...], w_ref[...])
    o_ref[...] = y
    ob_ref[...] = (y * scale).astype(BF16)


def _mm(x, w, tm, scale=1.0):
    n, d = x.shape
    m = w.shape[1]
    return pl.pallas_call(
        functools.partial(_mm_kernel, scale),
        grid=(n // tm,),
        in_specs=[pl.BlockSpec((tm, d), lambda i: (i, 0)),
                  pl.BlockSpec((d, m), lambda i: (0, 0))],
        out_specs=[pl.BlockSpec((tm, m), lambda i: (i, 0))] * 2,
        out_shape=[jax.ShapeDtypeStruct((n, m), F32), jax.ShapeDtypeStruct((n, m), BF16)],
        compiler_params=_cparams("parallel"),
        name="mm",
    )(x, w)


def _xattn_heads(q, k_ref, v_ref):
    dh = q.shape[1] // XA_HEADS
    outs = []
    for h in range(XA_HEADS):
        lanes = slice(h * dh, (h + 1) * dh)
        s = _bdot_nt(q[:, lanes], k_ref[0, :, lanes])
        s = s - jnp.max(s, axis=-1, keepdims=True)
        e = jnp.exp(s)
        p = e / jnp.sum(e, axis=-1, keepdims=True)
        outs.append(_bdot(p, v_ref[0, :, lanes]))
    return jnp.concatenate(outs, axis=1)


def _xattn_prompt_kernel(h_ref, wq_ref, wo_ref, k_ref, v_ref, g_ref, b_ref, o_ref):
    h = h_ref[...]
    dh = h.shape[1] // XA_HEADS
    q = (_bdot(h, wq_ref[...]) * (dh ** -0.5)).astype(BF16)
    o = _xattn_heads(q, k_ref, v_ref)
    o_ref[...] = _layer_norm(ALPHA * h + _bdot(o, wo_ref[...]), g_ref[...], b_ref[...])


def _xattn_prompt(h, wq, wo, mem_k, mem_v, g, b, seq, tm):
    n, d = h.shape
    n_mem = mem_k.shape[1]
    tps = seq // tm
    full = pl.BlockSpec((d, d), lambda i: (0, 0))
    mem = pl.BlockSpec((1, n_mem, d), lambda i: (i // tps, 0, 0))
    vec = pl.BlockSpec((1, d), lambda i: (0, 0))
    return pl.pallas_call(
        _xattn_prompt_kernel,
        grid=(n // tm,),
        in_specs=[pl.BlockSpec((tm, d), lambda i: (i, 0)), full, full, mem, mem, vec, vec],
        out_specs=pl.BlockSpec((tm, d), lambda i: (i, 0)),
        out_shape=jax.ShapeDtypeStruct((n, d), F32),
        compiler_params=_cparams("parallel"),
        name="xattn_prompt",
    )(h, wq, wo, mem_k, mem_v, g.reshape(1, d), b.reshape(1, d))


def _xattn_sample_kernel(q_ref, k_ref, v_ref, o_ref):
    q = q_ref[0]
    n_mem, n_heads, dh = k_ref.shape
    rows = q.shape[0]
    k2 = k_ref[...].reshape(n_mem * n_heads, dh).astype(BF16)
    v2 = v_ref[...].reshape(n_mem * n_heads, dh).astype(BF16)
    qs = jnp.concatenate([q[:, h * dh:(h + 1) * dh] for h in range(n_heads)], axis=0)
    shape = (n_heads * rows, n_mem * n_heads)
    own = (_iota(shape, 0) // rows) == (_iota(shape, 1) & (n_heads - 1))
    s = jnp.where(own, _bdot_nt(qs, k2), NEG_BIG)
    s = s - jnp.max(s, axis=-1, keepdims=True)
    e = jnp.exp(s)
    p = e / jnp.sum(e, axis=-1, keepdims=True)
    o = _bdot(p, v2)
    o_ref[0] = jnp.concatenate([o[h * rows:(h + 1) * rows] for h in range(n_heads)], axis=1)


def _xattn_sample(q8, mem_k, mem_v, layer):
    bsz, rows, d = q8.shape
    _, _, n_mem, n_heads, dh = mem_k.shape
    tok = pl.BlockSpec((1, rows, d), lambda b: (b, 0, 0))
    mem = pl.BlockSpec((None, None, n_mem, n_heads, dh), lambda b: (layer, b, 0, 0, 0))
    return pl.pallas_call(
        _xattn_sample_kernel,
        grid=(bsz,),
        in_specs=[tok, mem, mem],
        out_specs=tok,
        out_shape=jax.ShapeDtypeStruct((bsz, rows, d), F32),
        compiler_params=_cparams("parallel"),
        name="xattn_sample",
    )(q8, mem_k, mem_v)


def _top2_sum(a, b, c, d):
    m1, n1 = jnp.maximum(a, b), jnp.minimum(a, b)
    m2, n2 = jnp.maximum(c, d), jnp.minimum(c, d)
    return jnp.maximum(m1, m2) + jnp.maximum(jnp.minimum(m1, m2), jnp.maximum(n1, n2))


def _router_kernel(x_ref, wt_ref, bias_ref, o_ref):
    x = x_ref[...]
    wt = wt_ref[...]
    xh, xl = _split(x)
    wh, wl = _split(wt)
    nt = (((1,), (1,)), ((), ()))
    logits = (lax.dot_general(wh, xh, nt, preferred_element_type=F32)
              + lax.dot_general(wh, xl, nt, preferred_element_type=F32)
              + lax.dot_general(wl, xh, nt, preferred_element_type=F32))
    s_all = _sigmoid(logits)
    sel_all = s_all + bias_ref[...]
    s = [s_all[e:e + 1, :] for e in range(N_EXPERTS)]
    sel = [sel_all[e:e + 1, :] for e in range(N_EXPERTS)]
    epg = EXPERTS_PER_GROUP
    score = [_top2_sum(*sel[g * epg:(g + 1) * epg]) for g in range(N_GROUPS)]
    best = functools.reduce(jnp.maximum, score)
    taken = None
    in_group = []
    for g in range(N_GROUPS):
        hit = score[g] == best
        if taken is None:
            pick, taken = hit, hit
        else:
            pick = hit & jnp.logical_not(taken)
            taken = taken | hit
        in_group.append(pick)
    chosen = []
    for g in range(N_GROUPS):
        vals = sel[g * epg:(g + 1) * epg]
        for j in range(epg):
            ahead = jnp.zeros_like(vals[j])
            for i in range(epg):
                if i == j:
                    continue
                before = (vals[i] >= vals[j]) if i < j else (vals[i] > vals[j])
                ahead = ahead + jnp.where(before, 1.0, 0.0)
            chosen.append(in_group[g] & (ahead < 1.5))
    gates = [jnp.where(chosen[e], s[e], 0.0) for e in range(N_EXPERTS)]
    total = functools.reduce(lambda a, b: a + b, gates)
    o_ref[...] = jnp.concatenate(gates, axis=0) / total


def _router(x, w_router_t, bias, tm):
    n, d = x.shape
    e = w_router_t.shape[0]
    return pl.pallas_call(
        _router_kernel,
        grid=(n // tm,),
        in_specs=[pl.BlockSpec((tm, d), lambda i: (i, 0)),
                  pl.BlockSpec((e, d), lambda i: (0, 0)),
                  pl.BlockSpec((e, 1), lambda i: (0, 0))],
        out_specs=pl.BlockSpec((e, tm), lambda i: (0, i)),
        out_shape=jax.ShapeDtypeStruct((e, n), F32),
        compiler_params=_cparams("parallel"),
        name="router",
    )(x, w_router_t, bias.reshape(e, 1))


def _moe_kernel(x_ref, comb_ref, wg_ref, wu_ref, wd_ref, g_ref, b_ref, o_ref, xb_ref, acc_ref):
    e = pl.program_id(1)

    @pl.when(e == 0)
    def _():
        xb_ref[...] = x_ref[...].astype(BF16)
        acc_ref[...] = jnp.zeros_like(acc_ref)

    xb = xb_ref[...]
    hg = jnp.dot(xb, wg_ref[0], preferred_element_type=F32)
    hu = jnp.dot(xb, wu_ref[0], preferred_element_type=F32)
    comb = comb_ref[...]
    gate = jnp.sum(jnp.where(_iota(comb.shape, 1) == e, comb, 0.0), axis=1, keepdims=True)
    act = _silu(hg) * hu * gate
    acc_ref[...] += _bdot(act, wd_ref[0])

    @pl.when(e == pl.num_programs(1) - 1)
    def _():
        o_ref[...] = _layer_norm(ALPHA * x_ref[...] + acc_ref[...], g_ref[...], b_ref[...])


def _moe(x, comb, wg, wu, wd, g, b, tm):
    n, d = x.shape
    ne, _, f = wg.shape
    return pl.pallas_call(
        _moe_kernel,
        grid=(n // tm, ne),
        in_specs=[pl.BlockSpec((tm, d), lambda i, e: (i, 0)),
                  pl.BlockSpec((tm, ne), lambda i, e: (i, 0)),
                  pl.BlockSpec((1, d, f), lambda i, e: (e, 0, 0)),
                  pl.BlockSpec((1, d, f), lambda i, e: (e, 0, 0)),
                  pl.BlockSpec((1, f, d), lambda i, e: (e, 0, 0)),
                  pl.BlockSpec((1, d), lambda i, e: (0, 0)),
                  pl.BlockSpec((1, d), lambda i, e: (0, 0))],
        out_specs=pl.BlockSpec((tm, d), lambda i, e: (i, 0)),
        out_shape=jax.ShapeDtypeStruct((n, d), F32),
        scratch_shapes=[pltpu.VMEM((tm, d), BF16), pltpu.VMEM((tm, d), F32)],
        compiler_params=_cparams("parallel", "arbitrary"),
        name="moe",
    )(x, comb, wg, wu, wd, g.reshape(1, d), b.reshape(1, d))


def _lanes(x):
    if x.ndim == 1:
        x = jnp.repeat(x, GROUP_LANES)
    return x.reshape(1, W_Q).astype(F32)


def _block_diag(w):
    g, a, b = w.shape
    eye = jnp.eye(g, dtype=w.dtype)
    return (eye[:, None, :, None] * w[:, :, None, :]).reshape(g * a, g * b)


def _tile(n, pref):
    t = min(n, pref)
    while n % t:
        t //= 2
    return t


def kernel(x_prompt, x_sample, cache_sb_k, cache_sb_v, state_pool, state_conv, state_delta,
           cache_mem_k, cache_mem_v, page_table, mem_prompt,
           ln0_g, ln0_b, w_in, sb_bias, gmlp_ln_g, gmlp_ln_b, gmlp_ws, gmlp_bs, pool_w, pool_gamma,
           conv_w, dn_a_log, dn_dt_bias, dn_norm_g, w_out, ln1_g, ln1_b,
           xa_wq, xa_wk, xa_wv, xa_wo, ln2_g, ln2_b, w_router, router_bias,
           ex_wg, ex_wu, ex_wd, ln3_g, ln3_b):
    bp, seq, d = x_prompt.shape
    bs, t_new, _ = x_sample.shape
    n_p = bp * seq
    n_s = bs * t_new
    n_mem = mem_prompt.shape[1]
    page = cache_sb_k.shape[2]
    past_len = page_table.shape[1] * page
    xa_dh = d // XA_HEADS
    tm_p = _tile(seq, 512)
    tm_s = _tile(n_s, 512)
    tm_moe_p = _tile(n_p, 1024)
    n_sub = 8 if seq % (8 * DELTA_CHUNK) == 0 else 1

    cache_kt = cache_sb_k.transpose(0, 1, 3, 4, 2)
    cache_vt = cache_sb_v.transpose(0, 1, 3, 4, 2)
    w_router_t = w_router.T
    mem_flat = mem_prompt.reshape(bp * n_mem, d)

    def rows8(x):
        x = x.reshape(bs, t_new, -1)
        return jnp.pad(x, ((0, 0), (0, SB_ROWS - t_new), (0, 0)))

    def tmajor(x):
        return x.reshape(bs, t_new, -1).transpose(1, 0, 2)

    def bmajor(x):
        return x.transpose(1, 0, 2)

    h_p = _ln(x_prompt.reshape(n_p, d), ln0_g, ln0_b, tm_p)
    h_s = _ln(x_sample.reshape(n_s, d), ln0_g, ln0_b, tm_s)

    outs = {k: [] for k in ("kp", "vp", "ks", "vs", "gv", "pp", "ps", "cp", "cs", "dp", "ds",
                            "mk", "mv")}
    for l in range(DEPTH):
        w_in_l = _prep_w_in(w_in[l])
        w_out_l = w_out[l].astype(BF16)
        w_out_parts = [w_out_l[i * W_Q:(i + 1) * W_Q] for i in range(4)]
        ws_l = gmlp_ws[l]
        bs_lanes = jnp.repeat(gmlp_bs[l].T, GROUP_LANES, axis=1)
        lng, lnb = _lanes(gmlp_ln_g[l]), _lanes(gmlp_ln_b[l])
        pw = _block_diag(pool_w[l]).astype(BF16)
        gamma = pool_gamma[l].reshape(1, W_Q)
        cw = conv_w[l]
        a_log, dt_b = _lanes(dn_a_log[l]), _lanes(dn_dt_bias[l])
        gn = jnp.tile(dn_norm_g[l], H_D).reshape(1, W_Q)
        wq, wo = xa_wq[l].astype(BF16), xa_wo[l].astype(BF16)
        wg, wu, wd = ex_wg[l].astype(BF16), ex_wu[l].astype(BF16), ex_wd[l].astype(BF16)

        mk, mk_b = _mm(mem_flat, xa_wk[l].astype(BF16), _tile(bp * n_mem, 256))
        mv, mv_b = _mm(mem_flat, xa_wv[l].astype(BF16), _tile(bp * n_mem, 256))
        outs["mk"].append(mk.reshape(bp, n_mem, XA_HEADS, xa_dh))
        outs["mv"].append(mv.reshape(bp, n_mem, XA_HEADS, xa_dh))

        (q, k, v, k_b, v_b, ub, vb, xc, qkv, zg, bd, ad) = _proj_in(h_p, w_in_l, tm_p)
        outs["kp"].append(k.reshape(bp, seq, H_A, DH_A))
        outs["vp"].append(v.reshape(bp, seq, H_A, DH_A))
        outs["pp"].append(xc.reshape(bp, seq, W_Q)[:, seq - POOL_BUF:])
        outs["cp"].append(qkv.reshape(bp, seq, 3 * W_Q)[:, seq - (CONV_K - 1):])
        oa = _sb_prompt(q, k_b, v_b, sb_bias[l], bp, seq)
        ob = _gmlp_prompt(ub, vb, ws_l, bs_lanes, lng, lnb, tm_p)
        oc = _pool_prompt(xc, pw, gamma, seq, tm_p)
        qd, kd, vd, gd, betad = _delta_prep(qkv, bd, ad, cw, a_log, dt_b, seq, tm_p)
        r3 = lambda x: x.reshape(bp, seq, W_Q)
        od, s_fin = _delta(r3(qd), r3(kd), r3(vd), r3(gd), r3(betad), r3(zg),
                           jnp.zeros((bp, H_D * DK_D, DK_D), F32), gn, n_sub)
        outs["dp"].append(s_fin.reshape(bp, H_D, DK_D, DK_D))
        h_p = _mm_ln([oa, ob, oc, od.reshape(n_p, W_Q)], w_out_parts, h_p, ln1_g[l], ln1_b[l], tm_p)
        h_p = _xattn_prompt(h_p, wq, wo, mk_b.reshape(bp, n_mem, d), mv_b.reshape(bp, n_mem, d),
                            ln2_g[l], ln2_b[l], seq, tm_p)
        comb = _router(h_p, w_router_t, router_bias, tm_p).T
        h_p = _moe(h_p, comb, wg, wu, wd, ln3_g[l], ln3_b[l], tm_moe_p)

        (q, k, v, _, _, ub, vb, xc, qkv, zg, bd, ad) = _proj_in(h_s, w_in_l, tm_s)
        outs["ks"].append(k.reshape(bs, t_new, H_A, DH_A))
        outs["vs"].append(v.reshape(bs, t_new, H_A, DH_A))
        oa = _sb_sample(rows8(q), rows8(k), rows8(v), cache_kt, cache_vt, page_table, sb_bias[l],
                        l)[:, :t_new].reshape(n_s, W_Q)
        wl = jnp.repeat(jnp.where(jnp.tril(jnp.ones((t_new, t_new), bool)),
                                  ws_l[:, :t_new, :t_new], 0.0).transpose(1, 2, 0),
                        GROUP_LANES, axis=2)
        (ob, vn, oc, pool_new, conv_new, qd, kd, vd, gd, betad) = _sample_mix(
            past_len, tmajor(ub), tmajor(vb), tmajor(xc), tmajor(qkv), tmajor(bd), tmajor(ad),
            bmajor(state_pool[l]), bmajor(state_conv[l]), wl, bs_lanes[:t_new], lng, lnb, pw,
            gamma, cw, a_log, dt_b)
        outs["gv"].append(bmajor(vn))
        outs["ps"].append(bmajor(pool_new))
        outs["cs"].append(bmajor(conv_new))
        od, s_fin = _delta_lanes(qd, kd, vd, gd, betad, tmajor(zg), gn,
                                 state_delta[l].transpose(1, 2, 3, 0))
        outs["ds"].append(s_fin.transpose(3, 0, 1, 2))
        flat = lambda x: bmajor(x).reshape(n_s, W_Q)
        h_s = _mm_ln([oa, flat(ob), flat(oc), flat(od)], w_out_parts, h_s, ln1_g[l], ln1_b[l],
                     tm_s)
        _, qx = _mm(h_s, wq, tm_s, scale=xa_dh ** -0.5)
        ox = _xattn_sample(rows8(qx), cache_mem_k, cache_mem_v, l)[:, :t_new].reshape(n_s, d)
        h_s = _mm_ln([ox], [wo], h_s, ln2_g[l], ln2_b[l], tm_s)
        comb = _router(h_s, w_router_t, router_bias, tm_s).T
        h_s = _moe(h_s, comb, wg, wu, wd, ln3_g[l], ln3_b[l], tm_s)

    st = lambda key: jnp.stack(outs[key])
    return (h_p.reshape(bp, seq, d), h_s.reshape(bs, t_new, d),
            st("kp"), st("vp"), st("ks"), st("vs"), st("gv"),
            st("pp"), st("ps"), st("cp"), st("cs"), st("dp"), st("ds"),
            st("mk"), st("mv"))
```

```python
import functools

import jax
import jax.numpy as jnp
from jax import lax
from jax.experimental import pallas as pl
from jax.experimental.pallas import tpu as pltpu

F32 = jnp.float32
BF16 = jnp.bfloat16

DEPTH = 2
H_A = 4
DH_A = 64
G_B = 4
GMLP_CHUNK = 128
POOL_WINDOWS = (2, 4, 8, 16)
POOL_BUF = 15
H_D = 4
DK_D = 64
CONV_K = 4
DELTA_CHUNK = 64
XA_HEADS = 4
N_EXPERTS = 16
N_GROUPS = 4
EXPERTS_PER_GROUP = 4
ALPHA = (2 * DEPTH) ** 0.25
LN_EPS = 1e-5
GROUP_LANES = 64
W_Q = 256

SB_BLOCK = 128
VMEM_LIMIT = 48 * 1024 * 1024


def _cparams(*sem):
    return pltpu.CompilerParams(dimension_semantics=sem, vmem_limit_bytes=VMEM_LIMIT)


def _bdot(a, b):
    return jnp.dot(a.astype(BF16), b.astype(BF16), preferred_element_type=F32)


def _bdot_nt(a, b):
    return lax.dot_general(a.astype(BF16), b.astype(BF16), (((1,), (1,)), ((), ())),
                           preferred_element_type=F32)


def _bdot_tn(a, b):
    return lax.dot_general(a.astype(BF16), b.astype(BF16), (((0,), (0,)), ((), ())),
                           preferred_element_type=F32)


def _split(x):
    hi = x.astype(BF16)
    lo = (x - hi.astype(F32)).astype(BF16)
    return hi, lo


def _dot_exact_rhs(x, c):
    hi, lo = _split(x)
    return (jnp.dot(hi, c, preferred_element_type=F32)
            + jnp.dot(lo, c, preferred_element_type=F32))


def _dot_exact_lhs(c, x):
    hi, lo = _split(x)
    return (jnp.dot(c, hi, preferred_element_type=F32)
            + jnp.dot(c, lo, preferred_element_type=F32))


def _iota(shape, dim):
    return lax.broadcasted_iota(jnp.int32, shape, dim)


def _group_ones(n):
    return jnp.where((_iota((n, n), 0) >> 6) == (_iota((n, n), 1) >> 6), 1.0, 0.0).astype(BF16)


def _group_sum(x, ones):
    return _dot_exact_rhs(x, ones)


def _layer_norm(x, g, b):
    mu = jnp.mean(x, axis=-1, keepdims=True)
    xc = x - mu
    var = jnp.mean(xc * xc, axis=-1, keepdims=True)
    return xc * lax.rsqrt(var + LN_EPS) * g + b


def _sigmoid(x):
    return 1.0 / (1.0 + jnp.exp(-x))


def _silu(x):
    return x * _sigmoid(x)


def _softplus(x):
    return jnp.maximum(x, 0.0) + jnp.log(1.0 + jnp.exp(-jnp.abs(x)))


def _gelu_tanh(x):
    return 0.5 * x * (1.0 + jnp.tanh(0.7978845608028654 * (x + 0.044715 * (x * x * x))))


def _group_layer_norm(x, ones, g, b):
    mu = _group_sum(x, ones) * (1.0 / GROUP_LANES)
    xc = x - mu
    var = _group_sum(xc * xc, ones) * (1.0 / GROUP_LANES)
    return xc * lax.rsqrt(var + LN_EPS) * g + b


def _l2_normalize(x, ones):
    return x * lax.rsqrt(_group_sum(x * x, ones) + 1e-6)


def _ln_kernel(x_ref, g_ref, b_ref, o_ref):
    o_ref[...] = _layer_norm(x_ref[...], g_ref[...], b_ref[...])


def _ln(x, g, b, tm):
    n, d = x.shape
    return pl.pallas_call(
        _ln_kernel,
        grid=(n // tm,),
        in_specs=[pl.BlockSpec((tm, d), lambda i: (i, 0)),
                  pl.BlockSpec((1, d), lambda i: (0, 0)),
                  pl.BlockSpec((1, d), lambda i: (0, 0))],
        out_specs=pl.BlockSpec((tm, d), lambda i: (i, 0)),
        out_shape=jax.ShapeDtypeStruct((n, d), F32),
        compiler_params=_cparams("parallel"),
        name="ln0",
    )(x, g.reshape(1, d), b.reshape(1, d))


_SEG = {"q": (0, 256), "k": (256, 512), "v": (512, 768), "ub": (768, 1024), "vb": (1024, 1280),
        "xc": (1280, 1536), "qkv": (1536, 2304), "zg": (2304, 2560), "bd": (2560, 2816),
        "ad": (2816, 3072)}
N_IN_PAD = 3072


def _proj_in_kernel(x_ref, w_ref, q_ref, k_ref, v_ref, kb_ref, vb_ref, ub_ref, vbb_ref, xc_ref,
                    qkv_ref, zg_ref, bd_ref, ad_ref):
    x = x_ref[...].astype(BF16)

    def seg(name):
        a, b = _SEG[name]
        return jnp.dot(x, w_ref[:, a:b], preferred_element_type=F32)

    first_head = _iota((x.shape[0], SB_BLOCK), 1) < GROUP_LANES

    def head_stacked(y):
        yb = y.astype(BF16)
        zero = jnp.zeros((x.shape[0], SB_BLOCK), BF16)
        parts = []
        for p in range(H_A // 2):
            yp = yb[:, p * SB_BLOCK:(p + 1) * SB_BLOCK]
            parts += [jnp.where(first_head, yp, zero), jnp.where(first_head, zero, yp)]
        return jnp.concatenate(parts, axis=1)

    q_ref[...] = (seg("q") * (DH_A ** -0.5)).astype(BF16)
    k = seg("k")
    k_ref[...] = k
    kb_ref[...] = head_stacked(k)
    v = seg("v")
    v_ref[...] = v
    vb_ref[...] = head_stacked(v)
    ub_ref[...] = seg("ub")
    vbb_ref[...] = seg("vb")
    xc_ref[...] = seg("xc")
    qkv_ref[...] = seg("qkv")
    zg_ref[...] = seg("zg")
    bd_ref[...] = seg("bd")
    ad_ref[...] = seg("ad")


def _proj_in(x, w, tm):
    n, d = x.shape
    widths = [(256, BF16), (256, F32), (256, F32), (512, BF16), (512, BF16), (256, F32), (256, F32),
              (256, F32), (768, F32), (256, F32), (256, F32), (256, F32)]
    return pl.pallas_call(
        _proj_in_kernel,
        grid=(n // tm,),
        in_specs=[pl.BlockSpec((tm, d), lambda i: (i, 0)),
                  pl.BlockSpec((d, N_IN_PAD), lambda i: (0, 0))],
        out_specs=[pl.BlockSpec((tm, wd), lambda i: (i, 0)) for wd, _ in widths],
        out_shape=[jax.ShapeDtypeStruct((n, wd), dt) for wd, dt in widths],
        compiler_params=_cparams("parallel"),
        name="proj_in",
    )(x, w)


def _prep_w_in(w_in_l):
    main = w_in_l[:, :2560]
    bd = jnp.repeat(w_in_l[:, 2560:2564], GROUP_LANES, axis=1)
    ad = jnp.repeat(w_in_l[:, 2564:2568], GROUP_LANES, axis=1)
    return jnp.concatenate([main, bd, ad], axis=1).astype(BF16)


NEG_BIG = -1e30


LOG2E = 1.4426950408889634


def _pair_rows(x, p):
    w = 2 * SB_BLOCK
    return jnp.concatenate([x[:, p * w:p * w + SB_BLOCK], x[:, p * w + SB_BLOCK:(p + 1) * w]], axis=0)


def _suffix_ones_hilo():
    nb = SB_BLOCK
    r = lax.broadcasted_iota(jnp.int32, (2 * nb, 2 * nb), 0) & (nb - 1)
    c = lax.broadcasted_iota(jnp.int32, (2 * nb, 2 * nb), 1)
    return jnp.where((c >= nb) | (r > c), 1.0, 0.0).astype(BF16)


SB_QROWS = 2 * SB_BLOCK


def _sb_prompt_kernel(bias_ref, q_ref, k_ref, v_ref, uo_ref, o_ref, carry_ref, acc_ref,
                      z_a, att_a, z_b, att_b):
    i = pl.program_id(1)
    nb = SB_BLOCK
    nq = SB_QROWS
    pw = 2 * nb
    last = (nq // nb) * (i + 1) - 1
    head_of_lane = _iota((1, H_A * nb), 1) >> 7
    bias_row = jnp.zeros((1, H_A * nb), F32)
    for h in range(H_A):
        bias_row = jnp.where(head_of_lane == h, bias_ref[h], bias_row)
    col_minus_row = _iota((nq, nb), 1) - _iota((nq, nb), 0)

    acc_ref[...] = jnp.zeros_like(acc_ref)
    carry_ref[...] = jnp.zeros_like(carry_ref)
    z_b[...] = jnp.zeros_like(z_b)
    att_b[...] = jnp.zeros_like(att_b)

    def stages(t, z_w, att_w, z_r, att_r, masked):
        ja = jnp.maximum(last - t, 0)
        k = k_ref[pl.ds(pl.multiple_of(ja * nb, nb), nb), :]
        for p in range(2):
            z_w[:, p * pw:(p + 1) * pw] = lax.dot_general(
                q_ref[:, p * nb:(p + 1) * nb], _pair_rows(k, p), (((1,), (1,)), ((), ())),
                preferred_element_type=F32)
        z = z_r[...] + bias_row
        l = jnp.log(1.0 + jnp.exp2(jnp.abs(z) * (-LOG2E)))
        nlk = jnp.maximum(z, 0.0) + l
        lb = z - nlk
        if masked:
            tb = t - 1
            span = jnp.where((tb >= 0) & (tb <= last), (tb - (nq // nb - 1)) * nb, -2 * nq)
            valid = col_minus_row < span
            nlk = nlk * jnp.concatenate([jnp.where(valid, 1.0, 0.0)] * H_A, axis=1)
            lb = lb + jnp.concatenate([jnp.where(valid, 0.0, NEG_BIG)] * H_A, axis=1)
        hi, lo = _split(nlk)
        res = [jnp.dot(jnp.concatenate([hi[:, h * nb:(h + 1) * nb], lo[:, h * nb:(h + 1) * nb]],
                                       axis=1), uo_ref[...], preferred_element_type=F32)
               for h in range(H_A)]
        right = jnp.concatenate([x[:, :nb] for x in res], axis=1)
        total = jnp.concatenate([x[:, nb:] for x in res], axis=1)
        carry = carry_ref[...]
        att_w[...] = jnp.exp(lb - right - carry).astype(BF16)
        carry_ref[...] = carry + total
        jc = jnp.clip(last - t + 2, 0, last)
        v = v_ref[pl.ds(pl.multiple_of(jc * nb, nb), nb), :]
        for p in range(2):
            acc_ref[:, p * nb:(p + 1) * nb] += jnp.dot(att_r[:, p * pw:(p + 1) * pw],
                                                       _pair_rows(v, p),
                                                       preferred_element_type=F32)

    def four_trips(masked, u, carry_unused):
        stages(4 * u, z_a, att_a, z_b, att_b, masked)
        stages(4 * u + 1, z_b, att_b, z_a, att_a, masked)
        stages(4 * u + 2, z_a, att_a, z_b, att_b, masked)
        stages(4 * u + 3, z_b, att_b, z_a, att_a, masked)
        return carry_unused

    four_trips(True, 0, 0)
    n_plain = jnp.maximum((last - 2) // 4, 0)
    lax.fori_loop(1, n_plain + 1, functools.partial(four_trips, False), 0)
    lax.fori_loop(n_plain + 1, (last + 6) // 4, functools.partial(four_trips, True), 0)
    o_ref[...] = acc_ref[...]


def _sb_prompt(q, k, v, bias, batch, seq):
    n = q.shape[0]
    nq = seq // SB_QROWS
    hw = H_A * SB_BLOCK
    stage_bufs = [pltpu.VMEM((SB_QROWS, hw), F32),
                  pltpu.VMEM((SB_QROWS, hw), BF16)]
    return pl.pallas_call(
        _sb_prompt_kernel,
        grid=(batch, nq),
        in_specs=[pl.BlockSpec(memory_space=pltpu.SMEM),
                  pl.BlockSpec((SB_QROWS, W_Q), lambda b, i: (b * nq + i, 0)),
                  pl.BlockSpec((seq, hw), lambda b, i: (b, 0)),
                  pl.BlockSpec((seq, hw), lambda b, i: (b, 0)),
                  pl.BlockSpec((2 * SB_BLOCK, 2 * SB_BLOCK), lambda b, i: (0, 0))],
        out_specs=pl.BlockSpec((SB_QROWS, W_Q), lambda b, i: (b * nq + i, 0)),
        scratch_shapes=[pltpu.VMEM((SB_QROWS, hw), F32),
                        pltpu.VMEM((SB_QROWS, W_Q), F32)] + stage_bufs * 2,
        out_shape=jax.ShapeDtypeStruct((n, W_Q), F32),
        compiler_params=_cparams("parallel", "arbitrary"),
        name="sb_prompt",
    )(bias, q, k, v, _suffix_ones_hilo())


SB_ROWS = 8


def _sb_sample_kernel(n_pages, pt_ref, bias_ref, q_ref, kn_ref, vn_ref, uo_ref, *rest):
    kpages = rest[:n_pages]
    vpages = rest[n_pages:2 * n_pages]
    o_ref = rest[2 * n_pages]
    m = H_A * SB_ROWS
    row = _iota((m, W_Q), 0)
    lane = _iota((m, W_Q), 1)
    own = (row >> 3) == (lane >> 6)
    qbd = jnp.where(own, jnp.concatenate([q_ref[0]] * H_A, axis=0), jnp.zeros((m, W_Q), BF16))
    nblk = n_pages + 1
    width = nblk * SB_BLOCK
    rowh = _iota((m, width), 0) >> 3
    key = _iota((m, width), 1)
    bias = jnp.zeros((m, width), F32)
    for h in range(H_A):
        bias = jnp.where(rowh == h, bias_ref[h], bias)
    visible = (key < n_pages * SB_BLOCK) | ((key - n_pages * SB_BLOCK) < (_iota((m, width), 0) & (SB_ROWS - 1)))

    def page_t(ref):
        return ref[...].reshape(W_Q, SB_BLOCK).astype(BF16)

    pad = jnp.zeros((SB_BLOCK - SB_ROWS, W_Q), F32)
    kn = jnp.concatenate([kn_ref[0], pad], axis=0)
    vn = jnp.concatenate([vn_ref[0], pad], axis=0)
    kt_all = jnp.concatenate([page_t(r) for r in kpages], axis=1)
    z = jnp.concatenate([jnp.dot(qbd, kt_all, preferred_element_type=F32), _bdot_nt(qbd, kn)],
                        axis=1) + bias
    l = jnp.log(1.0 + jnp.exp2(jnp.abs(z) * (-LOG2E)))
    nlk = jnp.maximum(z, 0.0) + l
    lb = jnp.where(visible, z - nlk, NEG_BIG)
    hi, lo = _split(jnp.where(visible, nlk, 0.0))
    blocks = [slice(p * SB_BLOCK, (p + 1) * SB_BLOCK) for p in range(nblk)]
    sums = jnp.dot(jnp.concatenate([jnp.concatenate([hi[:, s], lo[:, s]], axis=1) for s in blocks],
                                   axis=0), uo_ref[...], preferred_element_type=F32)
    right = jnp.concatenate([sums[p * m:(p + 1) * m, :SB_BLOCK] for p in range(nblk)], axis=1)
    carry = jnp.zeros((m, SB_BLOCK), F32)
    carries = [None] * nblk
    for p in range(nblk - 1, -1, -1):
        carries[p] = carry
        carry = carry + sums[p * m:(p + 1) * m, SB_BLOCK:]
    att = jnp.exp(lb - right - jnp.concatenate(carries, axis=1)).astype(BF16)
    vt_all = jnp.concatenate([page_t(r) for r in vpages], axis=1)
    acc = (lax.dot_general(att[:, :n_pages * SB_BLOCK], vt_all, (((1,), (1,)), ((), ())),
                           preferred_element_type=F32)
           + _bdot(att[:, n_pages * SB_BLOCK:], vn))
    acc = jnp.where(own, acc, 0.0)
    out = acc[0:SB_ROWS]
    for h in range(1, H_A):
        out = out + acc[h * SB_ROWS:(h + 1) * SB_ROWS]
    o_ref[0] = out


def _sb_sample(q8, k8, v8, cache_kt, cache_vt, page_table, bias, layer):
    bsz, n_pages = page_table.shape
    page = cache_kt.shape[4]

    def page_spec(p):
        return pl.BlockSpec((None, None, H_A, DH_A, page), lambda b, pt: (layer, pt[b, p], 0, 0, 0))

    tok = pl.BlockSpec((1, SB_ROWS, W_Q), lambda b, pt: (b, 0, 0))
    return pl.pallas_call(
        functools.partial(_sb_sample_kernel, n_pages),
        grid_spec=pltpu.PrefetchScalarGridSpec(
            num_scalar_prefetch=1,
            grid=(bsz,),
            in_specs=([pl.BlockSpec(memory_space=pltpu.SMEM), tok, tok, tok,
                       pl.BlockSpec((2 * SB_BLOCK, 2 * SB_BLOCK), lambda b, pt: (0, 0))]
                      + [page_spec(p) for p in range(n_pages)] * 2),
            out_specs=tok),
        out_shape=jax.ShapeDtypeStruct((bsz, SB_ROWS, W_Q), F32),
        compiler_params=_cparams("parallel"),
        name="sb_sample",
    )(page_table, bias, q8, k8, v8, _suffix_ones_hilo(), *([cache_kt] * n_pages),
      *([cache_vt] * n_pages))


def _gmlp_prompt_kernel(u_ref, v_ref, ws_ref, bs_ref, g_ref, b_ref, o_ref):
    c = GMLP_CHUNK
    ones = _group_ones(W_Q)
    tril = _iota((c, c), 0) >= _iota((c, c), 1)
    lane_group = _iota((c, W_Q), 1) >> 6
    w = [jnp.where(tril, ws_ref[g], 0.0).astype(BF16) for g in range(G_B)]
    for n in range(u_ref.shape[0] // c):
        rows = slice(n * c, (n + 1) * c)
        u = _gelu_tanh(u_ref[rows, :])
        vn = _group_layer_norm(_gelu_tanh(v_ref[rows, :]), ones, g_ref[...], b_ref[...])
        vnb = vn.astype(BF16)
        mixed = bs_ref[...]
        for g in range(G_B):
            vg = jnp.where(lane_group == g, vnb, jnp.zeros_like(vnb))
            mixed = mixed + jnp.dot(w[g], vg, preferred_element_type=F32)
        o_ref[rows, :] = u * mixed


def _gmlp_prompt(ub, vb, ws, bs_lanes, g, b, tm):
    n = ub.shape[0]
    c = GMLP_CHUNK
    tile = pl.BlockSpec((tm, W_Q), lambda i: (i, 0))
    vec = pl.BlockSpec((1, W_Q), lambda i: (0, 0))
    return pl.pallas_call(
        _gmlp_prompt_kernel,
        grid=(n // tm,),
        in_specs=[tile, tile,
                  pl.BlockSpec((G_B, c, c), lambda i: (0, 0, 0)),
                  pl.BlockSpec((c, W_Q), lambda i: (0, 0)),
                  vec, vec],
        out_specs=tile,
        out_shape=jax.ShapeDtypeStruct((n, W_Q), F32),
        compiler_params=_cparams("parallel"),
        name="gmlp_prompt",
    )(ub, vb, ws, bs_lanes, g, b)


HALO = 16


def _pool_windows(load, t, start_pos):
    lane = _iota((t, SB_BLOCK), 1)
    pos = start_pos + _iota((t, SB_BLOCK), 0)
    halves = []
    for half, (w_small, w_big) in enumerate(((POOL_WINDOWS[0], POOL_WINDOWS[1]),
                                              (POOL_WINDOWS[2], POOL_WINDOWS[3]))):
        s = load(0, half)
        for i in range(1, w_small):
            s = s + load(i, half)
        big = s
        for i in range(w_small, w_big):
            big = big + load(i, half)
        small_lane = lane < GROUP_LANES
        win = jnp.where(small_lane, s, big)
        w = jnp.where(small_lane, w_small, w_big)
        cnt = jnp.minimum(pos + 1, w).astype(F32)
        halves.append(win / cnt)
    return jnp.concatenate(halves, axis=1)


def _pool_prompt_kernel(tiles_per_seq, x_ref, halo_ref, w_ref, gamma_ref, o_ref, xx_ref):
    i = pl.program_id(0)
    t = x_ref.shape[0]
    first = (i % tiles_per_seq) == 0
    xx_ref[0:HALO, :] = jnp.where(first, 0.0, halo_ref[...])
    xx_ref[HALO:, :] = x_ref[...]

    def load(shift, half):
        return xx_ref[pl.ds(HALO - shift, t), half * SB_BLOCK:(half + 1) * SB_BLOCK]

    start = (i % tiles_per_seq) * t
    means = _pool_windows(load, t, start)
    d = means - x_ref[...]
    o_ref[...] = _bdot(d, w_ref[...]) * gamma_ref[...]


def _pool_prompt(xc, w_bd, gamma, seq, tm):
    n = xc.shape[0]
    hb = tm // HALO
    return pl.pallas_call(
        functools.partial(_pool_prompt_kernel, seq // tm),
        grid=(n // tm,),
        in_specs=[pl.BlockSpec((tm, W_Q), lambda i: (i, 0)),
                  pl.BlockSpec((HALO, W_Q), lambda i: (jnp.maximum(i * hb - 1, 0), 0)),
                  pl.BlockSpec((W_Q, W_Q), lambda i: (0, 0)),
                  pl.BlockSpec((1, W_Q), lambda i: (0, 0))],
        out_specs=pl.BlockSpec((tm, W_Q), lambda i: (i, 0)),
        out_shape=jax.ShapeDtypeStruct((n, W_Q), F32),
        scratch_shapes=[pltpu.VMEM((tm + HALO, W_Q), F32)],
        compiler_params=_cparams("parallel"),
        name="pool_prompt",
    )(xc, xc, w_bd, gamma)


CONV_HALO = 8


def _delta_gates(bd, ad, a_log, dt_bias):
    beta = _sigmoid(bd)
    g = -jnp.exp(a_log) * _softplus(ad + dt_bias)
    return beta, g


def _delta_prep_kernel(tiles_per_seq, x_ref, halo_ref, bd_ref, ad_ref, cw_ref, alog_ref, dt_ref,
                       q_ref, k_ref, v_ref, g_ref, b_ref, xx_ref):
    i = pl.program_id(0)
    t = x_ref.shape[0]
    first = (i % tiles_per_seq) == 0
    xx_ref[0:CONV_HALO, :] = jnp.where(first, 0.0, halo_ref[...])
    xx_ref[CONV_HALO:, :] = x_ref[...]
    ones = _group_ones(W_Q)
    off = CONV_HALO - (CONV_K - 1)
    for part, out in enumerate((q_ref, k_ref, v_ref)):
        lanes = slice(part * W_Q, (part + 1) * W_Q)
        y = xx_ref[pl.ds(off, t), lanes] * cw_ref[0:1, lanes]
        for j in range(1, CONV_K):
            y = y + xx_ref[pl.ds(off + j, t), lanes] * cw_ref[j:j + 1, lanes]
        y = _silu(y)
        out[...] = y if part == 2 else _l2_normalize(y, ones)
    beta, g = _delta_gates(bd_ref[...], ad_ref[...], alog_ref[...], dt_ref[...])
    b_ref[...] = beta
    g_ref[...] = g


def _delta_prep(qkv, bd, ad, conv_w, a_log, dt_bias, seq, tm):
    n, wd = qkv.shape
    hb = tm // CONV_HALO
    tile = pl.BlockSpec((tm, W_Q), lambda i: (i, 0))
    vec = pl.BlockSpec((1, W_Q), lambda i: (0, 0))
    return pl.pallas_call(
        functools.partial(_delta_prep_kernel, seq // tm),
        grid=(n // tm,),
        in_specs=[pl.BlockSpec((tm, wd), lambda i: (i, 0)),
                  pl.BlockSpec((CONV_HALO, wd), lambda i: (jnp.maximum(i * hb - 1, 0), 0)),
                  tile, tile,
                  pl.BlockSpec((CONV_K, wd), lambda i: (0, 0)),
                  vec, vec],
        out_specs=[tile] * 5,
        out_shape=[jax.ShapeDtypeStruct((n, W_Q), F32)] * 5,
        scratch_shapes=[pltpu.VMEM((tm + CONV_HALO, wd), F32)],
        compiler_params=_cparams("parallel"),
        name="delta_prep",
    )(qkv, qkv, bd, ad, conv_w, a_log, dt_bias)


def _dot3_shared(lhs, b):
    bh, bl = _split(b)
    parts = [_split(a) for a in lhs]
    his = [h for h, _ in parts]
    los = [l for _, l in parts]
    n, rows = len(lhs), lhs[0].shape[0]
    r_h = jnp.dot(jnp.concatenate(his + los, axis=0), bh, preferred_element_type=F32)
    r_l = jnp.dot(jnp.concatenate(his, axis=0), bl, preferred_element_type=F32)
    piece = lambda r, i: r[i * rows:(i + 1) * rows]
    return [piece(r_h, i) + piece(r_h, n + i) + piece(r_l, i) for i in range(n)]


def _delta_kernel(n_seq, n_sub, q_ref, k_ref, v_ref, g_ref, b_ref, zg_ref, s0_ref, gn_ref, o_ref,
                  sf_ref, s_ref):
    c = pl.program_id(0)
    n_steps = pl.num_programs(0)
    cs = DELTA_CHUNK
    hw = H_D * DK_D
    row = _iota((cs, hw), 0)
    col = _iota((cs, hw), 1) & (DK_D - 1)
    incl = row >= col
    strict = row > col
    bdmask = (_iota((hw, hw), 0) >> 6) == (_iota((hw, hw), 1) >> 6)
    ones = _group_ones(hw)
    tril = jnp.where(_iota((cs, cs), 0) >= _iota((cs, cs), 1), 1.0, 0.0).astype(BF16)

    def bd(x):
        return jnp.where(bdmask, jnp.concatenate([x] * H_D, axis=0), 0.0)

    @pl.when(c == 0)
    def _():
        for s in range(n_seq):
            s_ref[s] = jnp.where(bdmask, jnp.concatenate([s0_ref[s]] * H_D, axis=1), 0.0)

    chunks = [(s, slice(n * cs, (n + 1) * cs)) for n in range(n_sub) for s in range(n_seq)]
    each = lambda f, *xs: [f(*x) for x in zip(*xs)]
    q = [q_ref[s, r, :] * (DK_D ** -0.5) for s, r in chunks]
    k = [k_ref[s, r, :] for s, r in chunks]
    g = [g_ref[s, r, :] for s, r in chunks]
    beta = [b_ref[s, r, :] for s, r in chunks]
    vb = [v_ref[s, r, :] * bt for (s, r), bt in zip(chunks, beta)]
    kb = each(lambda x, bt: x * bt, k, beta)
    gc = [_dot_exact_lhs(tril, x) for x in g]
    diff = [_dot_exact_lhs(tril, jnp.where(strict, x, 0.0)) for x in g]
    decay = [jnp.where(incl, jnp.exp(jnp.where(incl, x, 0.0)), 0.0) for x in diff]
    eg = [jnp.exp(x) for x in gc]
    kbd = [bd(x).astype(BF16) for x in k]
    neg_lower = each(lambda x, y, dc: jnp.where(strict, -(_bdot_nt(x, y) * dc), 0.0), kb, kbd, decay)
    eye = jnp.where(row == col, 1.0, 0.0)
    t_inv = [eye + x for x in neg_lower]
    p = [_dot3_shared([x], bd(x))[0] for x in neg_lower]
    for _ in range(4):
        both = each(lambda x, t: _dot3_shared([x, t], bd(x)), p, t_inv)
        t_inv = each(lambda t, r: t + r[1], t_inv, both)
        p = [r[0] for r in both]
    t_inv = each(lambda t, x: t + _dot3_shared([t], bd(x))[0], t_inv, p)
    u = each(lambda t, x: _bdot(t, bd(x)), t_inv, vb)
    w = each(lambda t, x, e: _bdot(t, bd(x * e)), t_inv, kb, eg)
    intra = each(lambda x, y, dc: jnp.where(incl, _bdot_nt(x, y) * dc, 0.0), q, kbd, decay)
    gc_last = [x[cs - 1:cs, :] for x in gc]
    qg = each(lambda x, e: x * e, q, eg)
    kg = each(lambda x, gl, c_: x * jnp.exp(gl - c_), k, gc_last, gc)
    g_last = [jnp.exp(x) for x in gc_last]
    state = [s_ref[s] for s in range(n_seq)]
    for i, (s, rows) in enumerate(chunks):
        v_new = u[i] - _bdot(w[i], state[s])
        o = _bdot(qg[i], state[s]) + _bdot(intra[i], bd(v_new))
        state[s] = state[s] * g_last[i] + jnp.where(bdmask, _bdot_tn(kg[i], v_new), 0.0)
        o = o * lax.rsqrt(_group_sum(o * o, ones) * (1.0 / DK_D) + 1e-6) * gn_ref[...]
        o_ref[s, rows, :] = o * _silu(zg_ref[s, rows, :])
    for s in range(n_seq):
        s_ref[s] = state[s]

    @pl.when(c == n_steps - 1)
    def _():
        for s in range(n_seq):
            sf = state[s][:, 0:DK_D]
            for h in range(1, H_D):
                sf = sf + state[s][:, h * DK_D:(h + 1) * DK_D]
            sf_ref[s] = sf


def _delta(q, k, v, g, beta, zg, s0, gn, n_sub):
    bsz, t, hw = q.shape
    rows = n_sub * DELTA_CHUNK
    tok = pl.BlockSpec((bsz, rows, hw), lambda c: (0, c, 0))
    st = pl.BlockSpec((bsz, hw, DK_D), lambda c: (0, 0, 0))
    return pl.pallas_call(
        functools.partial(_delta_kernel, bsz, n_sub),
        grid=(t // rows,),
        in_specs=[tok] * 6 + [st, pl.BlockSpec((1, hw), lambda c: (0, 0))],
        out_specs=[tok, st],
        out_shape=[jax.ShapeDtypeStruct((bsz, t, hw), F32),
                   jax.ShapeDtypeStruct((bsz, hw, DK_D), F32)],
        scratch_shapes=[pltpu.VMEM((bsz, hw, hw), F32)],
        compiler_params=_cparams("arbitrary"),
        name="delta",
    )(q, k, v, g, beta, zg, s0, gn)


def _delta_lanes_kernel(q_ref, k_ref, v_ref, g_ref, b_ref, zg_ref, gn_ref, s0_ref, o_ref, sf_ref,
                        qt_ref, kt_ref):
    t_new, bsz, hw = q_ref.shape
    ones = _group_ones(hw)
    sf_ref[...] = s0_ref[...]
    zero = jnp.zeros((DK_D, bsz), F32)
    for t in range(t_new):
        qt_ref[...] = jnp.transpose(q_ref[t]) * (DK_D ** -0.5)
        kt_ref[...] = jnp.transpose(k_ref[t])
        vt = jnp.transpose(v_ref[t])
        at = jnp.exp(jnp.transpose(g_ref[t]))
        bt = jnp.transpose(b_ref[t])
        outs = []
        for h in range(H_D):
            base = h * DK_D
            a = at[base:base + 1, :]
            beta = bt[base:base + 1, :]

            def k_dot_s(dk, acc, h=h, base=base):
                return acc + kt_ref[pl.ds(base + dk, 1), :] * sf_ref[h, dk]

            ks = lax.fori_loop(0, DK_D, k_dot_s, zero, unroll=8)
            delta = beta * (vt[base:base + DK_D, :] - a * ks)

            def update(dk, acc, h=h, base=base, a=a, delta=delta):
                s_new = a * sf_ref[h, dk] + kt_ref[pl.ds(base + dk, 1), :] * delta
                sf_ref[h, dk] = s_new
                return acc + qt_ref[pl.ds(base + dk, 1), :] * s_new

            outs.append(lax.fori_loop(0, DK_D, update, zero, unroll=8))
        o = jnp.transpose(jnp.concatenate(outs, axis=0))
        o = o * lax.rsqrt(_group_sum(o * o, ones) * (1.0 / DK_D) + 1e-6) * gn_ref[...]
        o_ref[t] = o * _silu(zg_ref[t])


def _delta_lanes(q, k, v, g, beta, zg, gn, s0):
    t_new, bsz, hw = q.shape
    return pl.pallas_call(
        _delta_lanes_kernel,
        out_shape=[jax.ShapeDtypeStruct((t_new, bsz, hw), F32),
                   jax.ShapeDtypeStruct(s0.shape, F32)],
        scratch_shapes=[pltpu.VMEM((hw, bsz), F32), pltpu.VMEM((hw, bsz), F32)],
        compiler_params=pltpu.CompilerParams(vmem_limit_bytes=VMEM_LIMIT),
        name="delta_lanes",
    )(q, k, v, g, beta, zg, gn, s0)


def _sample_mix_kernel(start_pos, u_ref, v_ref, xc_ref, qkv_ref, bd_ref, ad_ref, pool_prev_ref,
                       conv_prev_ref, wl_ref, bsl_ref, lng_ref, lnb_ref, pw_ref, gamma_ref, cw_ref,
                       alog_ref, dt_ref,
                       ob_ref, vn_ref, oc_ref, pool_new_ref, conv_new_ref, q_ref, k_ref, vd_ref,
                       g_ref, b_ref):
    t_new = u_ref.shape[0]
    bsz = u_ref.shape[1]
    ones = _group_ones(W_Q)
    vn = [_group_layer_norm(_gelu_tanh(v_ref[t]), ones, lng_ref[...], lnb_ref[...])
          for t in range(t_new)]
    for t in range(t_new):
        vn_ref[t] = vn[t]
        mixed = bsl_ref[t:t + 1, :] + wl_ref[t, 0:1, :] * vn[0]
        for s in range(1, t + 1):
            mixed = mixed + wl_ref[t, s:s + 1, :] * vn[s]
        ob_ref[t] = _gelu_tanh(u_ref[t]) * mixed
    n_prev = pool_prev_ref.shape[0]

    def pool_row(i):
        return pool_prev_ref[i] if i < n_prev else xc_ref[i - n_prev]

    lane = _iota((bsz, W_Q), 1) >> 6
    for t in range(t_new):
        acc = pool_row(n_prev + t)
        win = None
        done = 1
        for gi, wdw in enumerate(POOL_WINDOWS):
            for i in range(done, wdw):
                acc = acc + pool_row(n_prev + t - i)
            done = wdw
            cnt = float(min(start_pos + t + 1, wdw))
            mean = acc / cnt
            win = mean if win is None else jnp.where(lane >= gi, mean, win)
        d = win - xc_ref[t]
        oc_ref[t] = _bdot(d, pw_ref[...]) * gamma_ref[...]
    for i in range(n_prev):
        pool_new_ref[i] = pool_row(t_new + i)
    c_prev = conv_prev_ref.shape[0]

    def conv_row(i):
        return conv_prev_ref[i] if i < c_prev else qkv_ref[i - c_prev]

    for t in range(t_new):
        y = conv_row(t) * cw_ref[0:1, :]
        for j in range(1, CONV_K):
            y = y + conv_row(t + j) * cw_ref[j:j + 1, :]
        y = _silu(y)
        q_ref[t] = _l2_normalize(y[:, 0:W_Q], ones)
        k_ref[t] = _l2_normalize(y[:, W_Q:2 * W_Q], ones)
        vd_ref[t] = y[:, 2 * W_Q:3 * W_Q]
        beta, g = _delta_gates(bd_ref[t], ad_ref[t], alog_ref[...], dt_ref[...])
        b_ref[t] = beta
        g_ref[t] = g
    for i in range(c_prev):
        conv_new_ref[i] = conv_row(t_new + i)


def _sample_mix(start_pos, ub, vb, xc, qkv, bd, ad, pool_prev, conv_prev, wl, bsl, lng, lnb, pw,
                gamma, cw, a_log, dt_bias):
    t_new, bsz, _ = ub.shape
    slab = jax.ShapeDtypeStruct((t_new, bsz, W_Q), F32)
    out_shape = [slab, slab, slab,
                 jax.ShapeDtypeStruct(pool_prev.shape, F32),
                 jax.ShapeDtypeStruct(conv_prev.shape, F32),
                 slab, slab, slab, slab, slab]
    return pl.pallas_call(
        functools.partial(_sample_mix_kernel, start_pos),
        out_shape=out_shape,
        compiler_params=pltpu.CompilerParams(vmem_limit_bytes=VMEM_LIMIT),
        name="sample_mix",
    )(ub, vb, xc, qkv, bd, ad, pool_prev, conv_prev, wl, bsl, lng, lnb, pw, gamma, cw, a_log,
      dt_bias)


def _mm_ln_kernel(n_in, *refs):
    xs = refs[:n_in]
    ws = refs[n_in:2 * n_in]
    h_ref, g_ref, b_ref, o_ref = refs[2 * n_in:]
    acc = ALPHA * h_ref[...]
    for x_ref, w_ref in zip(xs, ws):
        acc = acc + _bdot(x_ref[...], w_ref[...])
    o_ref[...] = _layer_norm(acc, g_ref[...], b_ref[...])


def _mm_ln(xs, ws, h, g, b, tm):
    n, d = h.shape
    n_in = len(xs)
    return pl.pallas_call(
        functools.partial(_mm_ln_kernel, n_in),
        grid=(n // tm,),
        in_specs=([pl.BlockSpec((tm, x.shape[1]), lambda i: (i, 0)) for x in xs]
                  + [pl.BlockSpec(w.shape, lambda i: (0, 0)) for w in ws]
                  + [pl.BlockSpec((tm, d), lambda i: (i, 0)),
                     pl.BlockSpec((1, d), lambda i: (0, 0)),
                     pl.BlockSpec((1, d), lambda i: (0, 0))]),
        out_specs=pl.BlockSpec((tm, d), lambda i: (i, 0)),
        out_shape=jax.ShapeDtypeStruct((n, d), F32),
        compiler_params=_cparams("parallel"),
        name="mm_ln",
    )(*xs, *ws, h, g.reshape(1, d), b.reshape(1, d))


def _mm_kernel(scale, x_ref, w_ref, o_ref, ob_ref):
    y = _bdot(x_ref[...], w_ref[...])
    o_ref[...] = y
    ob_ref[...] = (y * scale).astype(BF16)


def _mm(x, w, tm, scale=1.0):
    n, d = x.shape
    m = w.shape[1]
    return pl.pallas_call(
        functools.partial(_mm_kernel, scale),
        grid=(n // tm,),
        in_specs=[pl.BlockSpec((tm, d), lambda i: (i, 0)),
                  pl.BlockSpec((d, m), lambda i: (0, 0))],
        out_specs=[pl.BlockSpec((tm, m), lambda i: (i, 0))] * 2,
        out_shape=[jax.ShapeDtypeStruct((n, m), F32), jax.ShapeDtypeStruct((n, m), BF16)],
        compiler_params=_cparams("parallel"),
        name="mm",
    )(x, w)


def _xattn_heads(q, k_ref, v_ref):
    dh = q.shape[1] // XA_HEADS
    outs = []
    for h in range(XA_HEADS):
        lanes = slice(h * dh, (h + 1) * dh)
        s = _bdot_nt(q[:, lanes], k_ref[0, :, lanes])
        s = s - jnp.max(s, axis=-1, keepdims=True)
        e = jnp.exp(s)
        p = e / jnp.sum(e, axis=-1, keepdims=True)
        outs.append(_bdot(p, v_ref[0, :, lanes]))
    return jnp.concatenate(outs, axis=1)


def _xattn_prompt_kernel(h_ref, wq_ref, wo_ref, k_ref, v_ref, g_ref, b_ref, o_ref):
    h = h_ref[...]
    dh = h.shape[1] // XA_HEADS
    q = (_bdot(h, wq_ref[...]) * (dh ** -0.5)).astype(BF16)
    o = _xattn_heads(q, k_ref, v_ref)
    o_ref[...] = _layer_norm(ALPHA * h + _bdot(o, wo_ref[...]), g_ref[...], b_ref[...])


def _xattn_prompt(h, wq, wo, mem_k, mem_v, g, b, seq, tm):
    n, d = h.shape
    n_mem = mem_k.shape[1]
    tps = seq // tm
    full = pl.BlockSpec((d, d), lambda i: (0, 0))
    mem = pl.BlockSpec((1, n_mem, d), lambda i: (i // tps, 0, 0))
    vec = pl.BlockSpec((1, d), lambda i: (0, 0))
    return pl.pallas_call(
        _xattn_prompt_kernel,
        grid=(n // tm,),
        in_specs=[pl.BlockSpec((tm, d), lambda i: (i, 0)), full, full, mem, mem, vec, vec],
        out_specs=pl.BlockSpec((tm, d), lambda i: (i, 0)),
        out_shape=jax.ShapeDtypeStruct((n, d), F32),
        compiler_params=_cparams("parallel"),
        name="xattn_prompt",
    )(h, wq, wo, mem_k, mem_v, g.reshape(1, d), b.reshape(1, d))


def _xattn_sample_kernel(q_ref, k_ref, v_ref, o_ref):
    q = q_ref[0]
    n_mem, n_heads, dh = k_ref.shape
    rows = q.shape[0]
    k2 = k_ref[...].reshape(n_mem * n_heads, dh).astype(BF16)
    v2 = v_ref[...].reshape(n_mem * n_heads, dh).astype(BF16)
    qs = jnp.concatenate([q[:, h * dh:(h + 1) * dh] for h in range(n_heads)], axis=0)
    shape = (n_heads * rows, n_mem * n_heads)
    own = (_iota(shape, 0) // rows) == (_iota(shape, 1) & (n_heads - 1))
    s = jnp.where(own, _bdot_nt(qs, k2), NEG_BIG)
    s = s - jnp.max(s, axis=-1, keepdims=True)
    e = jnp.exp(s)
    p = e / jnp.sum(e, axis=-1, keepdims=True)
    o = _bdot(p, v2)
    o_ref[0] = jnp.concatenate([o[h * rows:(h + 1) * rows] for h in range(n_heads)], axis=1)


def _xattn_sample(q8, mem_k, mem_v, layer):
    bsz, rows, d = q8.shape
    _, _, n_mem, n_heads, dh = mem_k.shape
    tok = pl.BlockSpec((1, rows, d), lambda b: (b, 0, 0))
    mem = pl.BlockSpec((None, None, n_mem, n_heads, dh), lambda b: (layer, b, 0, 0, 0))
    return pl.pallas_call(
        _xattn_sample_kernel,
        grid=(bsz,),
        in_specs=[tok, mem, mem],
        out_specs=tok,
        out_shape=jax.ShapeDtypeStruct((bsz, rows, d), F32),
        compiler_params=_cparams("parallel"),
        name="xattn_sample",
    )(q8, mem_k, mem_v)


def _top2_sum(a, b, c, d):
    m1, n1 = jnp.maximum(a, b), jnp.minimum(a, b)
    m2, n2 = jnp.maximum(c, d), jnp.minimum(c, d)
    return jnp.maximum(m1, m2) + jnp.maximum(jnp.minimum(m1, m2), jnp.maximum(n1, n2))


def _router_kernel(x_ref, wt_ref, bias_ref, o_ref):
    x = x_ref[...]
    wt = wt_ref[...]
    xh, xl = _split(x)
    wh, wl = _split(wt)
    nt = (((1,), (1,)), ((), ()))
    logits = (lax.dot_general(wh, xh, nt, preferred_element_type=F32)
              + lax.dot_general(wh, xl, nt, preferred_element_type=F32)
              + lax.dot_general(wl, xh, nt, preferred_element_type=F32))
    s_all = _sigmoid(logits)
    sel_all = s_all + bias_ref[...]
    s = [s_all[e:e + 1, :] for e in range(N_EXPERTS)]
    sel = [sel_all[e:e + 1, :] for e in range(N_EXPERTS)]
    epg = EXPERTS_PER_GROUP
    score = [_top2_sum(*sel[g * epg:(g + 1) * epg]) for g in range(N_GROUPS)]
    best = functools.reduce(jnp.maximum, score)
    taken = None
    in_group = []
    for g in range(N_GROUPS):
        hit = score[g] == best
        if taken is None:
            pick, taken = hit, hit
        else:
            pick = hit & jnp.logical_not(taken)
            taken = taken | hit
        in_group.append(pick)
    chosen = []
    for g in range(N_GROUPS):
        vals = sel[g * epg:(g + 1) * epg]
        for j in range(epg):
            ahead = jnp.zeros_like(vals[j])
            for i in range(epg):
                if i == j:
                    continue
                before = (vals[i] >= vals[j]) if i < j else (vals[i] > vals[j])
                ahead = ahead + jnp.where(before, 1.0, 0.0)
            chosen.append(in_group[g] & (ahead < 1.5))
    gates = [jnp.where(chosen[e], s[e], 0.0) for e in range(N_EXPERTS)]
    total = functools.reduce(lambda a, b: a + b, gates)
    o_ref[...] = jnp.concatenate(gates, axis=0) / total


def _router(x, w_router_t, bias, tm):
    n, d = x.shape
    e = w_router_t.shape[0]
    return pl.pallas_call(
        _router_kernel,
        grid=(n // tm,),
        in_specs=[pl.BlockSpec((tm, d), lambda i: (i, 0)),
                  pl.BlockSpec((e, d), lambda i: (0, 0)),
                  pl.BlockSpec((e, 1), lambda i: (0, 0))],
        out_specs=pl.BlockSpec((e, tm), lambda i: (0, i)),
        out_shape=jax.ShapeDtypeStruct((e, n), F32),
        compiler_params=_cparams("parallel"),
        name="router",
    )(x, w_router_t, bias.reshape(e, 1))


def _moe_kernel(x_ref, comb_ref, wg_ref, wu_ref, wd_ref, g_ref, b_ref, o_ref, xb_ref, acc_ref):
    e = pl.program_id(1)

    @pl.when(e == 0)
    def _():
        xb_ref[...] = x_ref[...].astype(BF16)
        acc_ref[...] = jnp.zeros_like(acc_ref)

    xb = xb_ref[...]
    hg = jnp.dot(xb, wg_ref[0], preferred_element_type=F32)
    hu = jnp.dot(xb, wu_ref[0], preferred_element_type=F32)
    comb = comb_ref[...]
    gate = jnp.sum(jnp.where(_iota(comb.shape, 1) == e, comb, 0.0), axis=1, keepdims=True)
    act = _silu(hg) * hu * gate
    acc_ref[...] += _bdot(act, wd_ref[0])

    @pl.when(e == pl.num_programs(1) - 1)
    def _():
        o_ref[...] = _layer_norm(ALPHA * x_ref[...] + acc_ref[...], g_ref[...], b_ref[...])


def _moe(x, comb, wg, wu, wd, g, b, tm):
    n, d = x.shape
    ne, _, f = wg.shape
    return pl.pallas_call(
        _moe_kernel,
        grid=(n // tm, ne),
        in_specs=[pl.BlockSpec((tm, d), lambda i, e: (i, 0)),
                  pl.BlockSpec((tm, ne), lambda i, e: (i, 0)),
                  pl.BlockSpec((1, d, f), lambda i, e: (e, 0, 0)),
                  pl.BlockSpec((1, d, f), lambda i, e: (e, 0, 0)),
                  pl.BlockSpec((1, f, d), lambda i, e: (e, 0, 0)),
                  pl.BlockSpec((1, d), lambda i, e: (0, 0)),
                  pl.BlockSpec((1, d), lambda i, e: (0, 0))],
        out_specs=pl.BlockSpec((tm, d), lambda i, e: (i, 0)),
        out_shape=jax.ShapeDtypeStruct((n, d), F32),
        scratch_shapes=[pltpu.VMEM((tm, d), BF16), pltpu.VMEM((tm, d), F32)],
        compiler_params=_cparams("parallel", "arbitrary"),
        name="moe",
    )(x, comb, wg, wu, wd, g.reshape(1, d), b.reshape(1, d))


def _lanes(x):
    if x.ndim == 1:
        x = jnp.repeat(x, GROUP_LANES)
    return x.reshape(1, W_Q).astype(F32)


def _block_diag(w):
    g, a, b = w.shape
    eye = jnp.eye(g, dtype=w.dtype)
    return (eye[:, None, :, None] * w[:, :, None, :]).reshape(g * a, g * b)


def _tile(n, pref):
    t = min(n, pref)
    while n % t:
        t //= 2
    return t


def kernel(x_prompt, x_sample, cache_sb_k, cache_sb_v, state_pool, state_conv, state_delta,
           cache_mem_k, cache_mem_v, page_table, mem_prompt,
           ln0_g, ln0_b, w_in, sb_bias, gmlp_ln_g, gmlp_ln_b, gmlp_ws, gmlp_bs, pool_w, pool_gamma,
           conv_w, dn_a_log, dn_dt_bias, dn_norm_g, w_out, ln1_g, ln1_b,
           xa_wq, xa_wk, xa_wv, xa_wo, ln2_g, ln2_b, w_router, router_bias,
           ex_wg, ex_wu, ex_wd, ln3_g, ln3_b):
    bp, seq, d = x_prompt.shape
    bs, t_new, _ = x_sample.shape
    n_p = bp * seq
    n_s = bs * t_new
    n_mem = mem_prompt.shape[1]
    page = cache_sb_k.shape[2]
    past_len = page_table.shape[1] * page
    xa_dh = d // XA_HEADS
    tm_p = _tile(seq, 512)
    tm_s = _tile(n_s, 512)
    tm_moe_p = _tile(n_p, 1024)
    n_sub = 8 if seq % (8 * DELTA_CHUNK) == 0 else 1

    cache_kt = cache_sb_k.transpose(0, 1, 3, 4, 2)
    cache_vt = cache_sb_v.transpose(0, 1, 3, 4, 2)
    w_router_t = w_router.T
    mem_flat = mem_prompt.reshape(bp * n_mem, d)

    def rows8(x):
        x = x.reshape(bs, t_new, -1)
        return jnp.pad(x, ((0, 0), (0, SB_ROWS - t_new), (0, 0)))

    def tmajor(x):
        return x.reshape(bs, t_new, -1).transpose(1, 0, 2)

    def bmajor(x):
        return x.transpose(1, 0, 2)

    h_p = _ln(x_prompt.reshape(n_p, d), ln0_g, ln0_b, tm_p)
    h_s = _ln(x_sample.reshape(n_s, d), ln0_g, ln0_b, tm_s)

    outs = {k: [] for k in ("kp", "vp", "ks", "vs", "gv", "pp", "ps", "cp", "cs", "dp", "ds",
                            "mk", "mv")}
    for l in range(DEPTH):
        w_in_l = _prep_w_in(w_in[l])
        w_out_l = w_out[l].astype(BF16)
        w_out_parts = [w_out_l[i * W_Q:(i + 1) * W_Q] for i in range(4)]
        ws_l = gmlp_ws[l]
        bs_lanes = jnp.repeat(gmlp_bs[l].T, GROUP_LANES, axis=1)
        lng, lnb = _lanes(gmlp_ln_g[l]), _lanes(gmlp_ln_b[l])
        pw = _block_diag(pool_w[l]).astype(BF16)
        gamma = pool_gamma[l].reshape(1, W_Q)
        cw = conv_w[l]
        a_log, dt_b = _lanes(dn_a_log[l]), _lanes(dn_dt_bias[l])
        gn = jnp.tile(dn_norm_g[l], H_D).reshape(1, W_Q)
        wq, wo = xa_wq[l].astype(BF16), xa_wo[l].astype(BF16)
        wg, wu, wd = ex_wg[l].astype(BF16), ex_wu[l].astype(BF16), ex_wd[l].astype(BF16)

        mk, mk_b = _mm(mem_flat, xa_wk[l].astype(BF16), _tile(bp * n_mem, 256))
        mv, mv_b = _mm(mem_flat, xa_wv[l].astype(BF16), _tile(bp * n_mem, 256))
        outs["mk"].append(mk.reshape(bp, n_mem, XA_HEADS, xa_dh))
        outs["mv"].append(mv.reshape(bp, n_mem, XA_HEADS, xa_dh))

        (q, k, v, k_b, v_b, ub, vb, xc, qkv, zg, bd, ad) = _proj_in(h_p, w_in_l, tm_p)
        outs["kp"].append(k.reshape(bp, seq, H_A, DH_A))
        outs["vp"].append(v.reshape(bp, seq, H_A, DH_A))
        outs["pp"].append(xc.reshape(bp, seq, W_Q)[:, seq - POOL_BUF:])
        outs["cp"].append(qkv.reshape(bp, seq, 3 * W_Q)[:, seq - (CONV_K - 1):])
        oa = _sb_prompt(q, k_b, v_b, sb_bias[l], bp, seq)
        ob = _gmlp_prompt(ub, vb, ws_l, bs_lanes, lng, lnb, tm_p)
        oc = _pool_prompt(xc, pw, gamma, seq, tm_p)
        qd, kd, vd, gd, betad = _delta_prep(qkv, bd, ad, cw, a_log, dt_b, seq, tm_p)
        r3 = lambda x: x.reshape(bp, seq, W_Q)
        od, s_fin = _delta(r3(qd), r3(kd), r3(vd), r3(gd), r3(betad), r3(zg),
                           jnp.zeros((bp, H_D * DK_D, DK_D), F32), gn, n_sub)
        outs["dp"].append(s_fin.reshape(bp, H_D, DK_D, DK_D))
        h_p = _mm_ln([oa, ob, oc, od.reshape(n_p, W_Q)], w_out_parts, h_p, ln1_g[l], ln1_b[l], tm_p)
        h_p = _xattn_prompt(h_p, wq, wo, mk_b.reshape(bp, n_mem, d), mv_b.reshape(bp, n_mem, d),
                            ln2_g[l], ln2_b[l], seq, tm_p)
        comb = _router(h_p, w_router_t, router_bias, tm_p).T
        h_p = _moe(h_p, comb, wg, wu, wd, ln3_g[l], ln3_b[l], tm_moe_p)

        (q, k, v, _, _, ub, vb, xc, qkv, zg, bd, ad) = _proj_in(h_s, w_in_l, tm_s)
        outs["ks"].append(k.reshape(bs, t_new, H_A, DH_A))
        outs["vs"].append(v.reshape(bs, t_new, H_A, DH_A))
        oa = _sb_sample(rows8(q), rows8(k), rows8(v), cache_kt, cache_vt, page_table, sb_bias[l],
                        l)[:, :t_new].reshape(n_s, W_Q)
        wl = jnp.repeat(jnp.where(jnp.tril(jnp.ones((t_new, t_new), bool)),
                                  ws_l[:, :t_new, :t_new], 0.0).transpose(1, 2, 0),
                        GROUP_LANES, axis=2)
        (ob, vn, oc, pool_new, conv_new, qd, kd, vd, gd, betad) = _sample_mix(
            past_len, tmajor(ub), tmajor(vb), tmajor(xc), tmajor(qkv), tmajor(bd), tmajor(ad),
            bmajor(state_pool[l]), bmajor(state_conv[l]), wl, bs_lanes[:t_new], lng, lnb, pw,
            gamma, cw, a_log, dt_b)
        outs["gv"].append(bmajor(vn))
        outs["ps"].append(bmajor(pool_new))
        outs["cs"].append(bmajor(conv_new))
        od, s_fin = _delta_lanes(qd, kd, vd, gd, betad, tmajor(zg), gn,
                                 state_delta[l].transpose(1, 2, 3, 0))
        outs["ds"].append(s_fin.transpose(3, 0, 1, 2))
        flat = lambda x: bmajor(x).reshape(n_s, W_Q)
        h_s = _mm_ln([oa, flat(ob), flat(oc), flat(od)], w_out_parts, h_s, ln1_g[l], ln1_b[l],
                     tm_s)
        _, qx = _mm(h_s, wq, tm_s, scale=xa_dh ** -0.5)
        ox = _xattn_sample(rows8(qx), cache_mem_k, cache_mem_v, l)[:, :t_new].reshape(n_s, d)
        h_s = _mm_ln([ox], [wo], h_s, ln2_g[l], ln2_b[l], tm_s)
        comb = _router(h_s, w_router_t, router_bias, tm_s).T
        h_s = _moe(h_s, comb, wg, wu, wd, ln3_g[l], ln3_b[l], tm_s)

    st = lambda key: jnp.stack(outs[key])
    return (h_p.reshape(bp, seq, d), h_s.reshape(bs, t_new, d),
            st("kp"), st("vp"), st("ks"), st("vs"), st("gv"),
            st("pp"), st("ps"), st("cp"), st("cs"), st("dp"), st("ds"),
            st("mk"), st("mv"))
```

```python
import functools

import jax
import jax.numpy as jnp
from jax import lax
from jax.experimental import pallas as pl
from jax.experimental.pallas import tpu as pltpu

F32 = jnp.float32
BF16 = jnp.bfloat16

DEPTH = 2
H_A = 4
DH_A = 64
G_B = 4
GMLP_CHUNK = 128
POOL_WINDOWS = (2, 4, 8, 16)
POOL_BUF = 15
H_D = 4
DK_D = 64
CONV_K = 4
DELTA_CHUNK = 64
XA_HEADS = 4
N_EXPERTS = 16
N_GROUPS = 4
EXPERTS_PER_GROUP = 4
ALPHA = (2 * DEPTH) ** 0.25
LN_EPS = 1e-5
GROUP_LANES = 64
W_Q = 256

SB_BLOCK = 128
VMEM_LIMIT = 48 * 1024 * 1024


def _cparams(*sem):
    return pltpu.CompilerParams(dimension_semantics=sem, vmem_limit_bytes=VMEM_LIMIT)


def _bdot(a, b):
    return jnp.dot(a.astype(BF16), b.astype(BF16), preferred_element_type=F32)


def _bdot_nt(a, b):
    return lax.dot_general(a.astype(BF16), b.astype(BF16), (((1,), (1,)), ((), ())),
                           preferred_element_type=F32)


def _bdot_tn(a, b):
    return lax.dot_general(a.astype(BF16), b.astype(BF16), (((0,), (0,)), ((), ())),
                           preferred_element_type=F32)


def _split(x):
    hi = x.astype(BF16)
    lo = (x - hi.astype(F32)).astype(BF16)
    return hi, lo


def _dot_exact_rhs(x, c):
    hi, lo = _split(x)
    return (jnp.dot(hi, c, preferred_element_type=F32)
            + jnp.dot(lo, c, preferred_element_type=F32))


def _dot_exact_lhs(c, x):
    hi, lo = _split(x)
    return (jnp.dot(c, hi, preferred_element_type=F32)
            + jnp.dot(c, lo, preferred_element_type=F32))


def _iota(shape, dim):
    return lax.broadcasted_iota(jnp.int32, shape, dim)


def _group_ones(n):
    return jnp.where((_iota((n, n), 0) >> 6) == (_iota((n, n), 1) >> 6), 1.0, 0.0).astype(BF16)


def _group_sum(x, ones):
    return _dot_exact_rhs(x, ones)


def _layer_norm(x, g, b):
    mu = jnp.mean(x, axis=-1, keepdims=True)
    xc = x - mu
    var = jnp.mean(xc * xc, axis=-1, keepdims=True)
    return xc * lax.rsqrt(var + LN_EPS) * g + b


def _sigmoid(x):
    return 1.0 / (1.0 + jnp.exp(-x))


def _silu(x):
    return x * _sigmoid(x)


def _softplus(x):
    return jnp.maximum(x, 0.0) + jnp.log(1.0 + jnp.exp(-jnp.abs(x)))


def _gelu_tanh(x):
    return 0.5 * x * (1.0 + jnp.tanh(0.7978845608028654 * (x + 0.044715 * (x * x * x))))


def _group_layer_norm(x, ones, g, b):
    mu = _group_sum(x, ones) * (1.0 / GROUP_LANES)
    xc = x - mu
    var = _group_sum(xc * xc, ones) * (1.0 / GROUP_LANES)
    return xc * lax.rsqrt(var + LN_EPS) * g + b


def _l2_normalize(x, ones):
    return x * lax.rsqrt(_group_sum(x * x, ones) + 1e-6)


def _ln_kernel(x_ref, g_ref, b_ref, o_ref):
    o_ref[...] = _layer_norm(x_ref[...], g_ref[...], b_ref[...])


def _ln(x, g, b, tm):
    n, d = x.shape
    return pl.pallas_call(
        _ln_kernel,
        grid=(n // tm,),
        in_specs=[pl.BlockSpec((tm, d), lambda i: (i, 0)),
                  pl.BlockSpec((1, d), lambda i: (0, 0)),
                  pl.BlockSpec((1, d), lambda i: (0, 0))],
        out_specs=pl.BlockSpec((tm, d), lambda i: (i, 0)),
        out_shape=jax.ShapeDtypeStruct((n, d), F32),
        compiler_params=_cparams("parallel"),
        name="ln0",
    )(x, g.reshape(1, d), b.reshape(1, d))


_SEG = {"q": (0, 256), "k": (256, 512), "v": (512, 768), "ub": (768, 1024), "vb": (1024, 1280),
        "xc": (1280, 1536), "qkv": (1536, 2304), "zg": (2304, 2560), "bd": (2560, 2816),
        "ad": (2816, 3072)}
N_IN_PAD = 3072


def _proj_in_kernel(x_ref, w_ref, q_ref, k_ref, v_ref, kb_ref, vb_ref, ub_ref, vbb_ref, xc_ref,
                    qkv_ref, zg_ref, bd_ref, ad_ref):
    x = x_ref[...].astype(BF16)

    def seg(name):
        a, b = _SEG[name]
        return jnp.dot(x, w_ref[:, a:b], preferred_element_type=F32)

    first_head = _iota((x.shape[0], SB_BLOCK), 1) < GROUP_LANES

    def head_stacked(y):
        yb = y.astype(BF16)
        zero = jnp.zeros((x.shape[0], SB_BLOCK), BF16)
        parts = []
        for p in range(H_A // 2):
            yp = yb[:, p * SB_BLOCK:(p + 1) * SB_BLOCK]
            parts += [jnp.where(first_head, yp, zero), jnp.where(first_head, zero, yp)]
        return jnp.concatenate(parts, axis=1)

    q_ref[...] = (seg("q") * (DH_A ** -0.5)).astype(BF16)
    k = seg("k")
    k_ref[...] = k
    kb_ref[...] = head_stacked(k)
    v = seg("v")
    v_ref[...] = v
    vb_ref[...] = head_stacked(v)
    ub_ref[...] = seg("ub")
    vbb_ref[...] = seg("vb")
    xc_ref[...] = seg("xc")
    qkv_ref[...] = seg("qkv")
    zg_ref[...] = seg("zg")
    bd_ref[...] = seg("bd")
    ad_ref[...] = seg("ad")


def _proj_in(x, w, tm):
    n, d = x.shape
    widths = [(256, BF16), (256, F32), (256, F32), (512, BF16), (512, BF16), (256, F32), (256, F32),
              (256, F32), (768, F32), (256, F32), (256, F32), (256, F32)]
    return pl.pallas_call(
        _proj_in_kernel,
        grid=(n // tm,),
        in_specs=[pl.BlockSpec((tm, d), lambda i: (i, 0)),
                  pl.BlockSpec((d, N_IN_PAD), lambda i: (0, 0))],
        out_specs=[pl.BlockSpec((tm, wd), lambda i: (i, 0)) for wd, _ in widths],
        out_shape=[jax.ShapeDtypeStruct((n, wd), dt) for wd, dt in widths],
        compiler_params=_cparams("parallel"),
        name="proj_in",
    )(x, w)


def _prep_w_in(w_in_l):
    main = w_in_l[:, :2560]
    bd = jnp.repeat(w_in_l[:, 2560:2564], GROUP_LANES, axis=1)
    ad = jnp.repeat(w_in_l[:, 2564:2568], GROUP_LANES, axis=1)
    return jnp.concatenate([main, bd, ad], axis=1).astype(BF16)


NEG_BIG = -1e30


LOG2E = 1.4426950408889634


def _pair_rows(x, p):
    w = 2 * SB_BLOCK
    return jnp.concatenate([x[:, p * w:p * w + SB_BLOCK], x[:, p * w + SB_BLOCK:(p + 1) * w]], axis=0)


def _suffix_ones_hilo():
    nb = SB_BLOCK
    r = lax.broadcasted_iota(jnp.int32, (2 * nb, 2 * nb), 0) & (nb - 1)
    c = lax.broadcasted_iota(jnp.int32, (2 * nb, 2 * nb), 1)
    return jnp.where((c >= nb) | (r > c), 1.0, 0.0).astype(BF16)


SB_QROWS = 2 * SB_BLOCK


def _sb_prompt_kernel(bias_ref, q_ref, k_ref, v_ref, uo_ref, o_ref, carry_ref, acc_ref,
                      z_a, att_a, z_b, att_b):
    i = pl.program_id(1)
    nb = SB_BLOCK
    nq = SB_QROWS
    pw = 2 * nb
    last = (nq // nb) * (i + 1) - 1
    head_of_lane = _iota((1, H_A * nb), 1) >> 7
    bias_row = jnp.zeros((1, H_A * nb), F32)
    for h in range(H_A):
        bias_row = jnp.where(head_of_lane == h, bias_ref[h], bias_row)
    col_minus_row = _iota((nq, nb), 1) - _iota((nq, nb), 0)

    acc_ref[...] = jnp.zeros_like(acc_ref)
    carry_ref[...] = jnp.zeros_like(carry_ref)
    z_b[...] = jnp.zeros_like(z_b)
    att_b[...] = jnp.zeros_like(att_b)

    def stages(t, z_w, att_w, z_r, att_r, masked):
        ja = jnp.maximum(last - t, 0)
        k = k_ref[pl.ds(pl.multiple_of(ja * nb, nb), nb), :]
        for p in range(2):
            z_w[:, p * pw:(p + 1) * pw] = lax.dot_general(
                q_ref[:, p * nb:(p + 1) * nb], _pair_rows(k, p), (((1,), (1,)), ((), ())),
                preferred_element_type=F32)
        z = z_r[...] + bias_row
        l = jnp.log(1.0 + jnp.exp2(jnp.abs(z) * (-LOG2E)))
        nlk = jnp.maximum(z, 0.0) + l
        lb = z - nlk
        if masked:
            tb = t - 1
            span = jnp.where((tb >= 0) & (tb <= last), (tb - (nq // nb - 1)) * nb, -2 * nq)
            valid = col_minus_row < span
            nlk = nlk * jnp.concatenate([jnp.where(valid, 1.0, 0.0)] * H_A, axis=1)
            lb = lb + jnp.concatenate([jnp.where(valid, 0.0, NEG_BIG)] * H_A, axis=1)
        hi, lo = _split(nlk)
        res = [jnp.dot(jnp.concatenate([hi[:, h * nb:(h + 1) * nb], lo[:, h * nb:(h + 1) * nb]],
                                       axis=1), uo_ref[...], preferred_element_type=F32)
               for h in range(H_A)]
        right = jnp.concatenate([x[:, :nb] for x in res], axis=1)
        total = jnp.concatenate([x[:, nb:] for x in res], axis=1)
        carry = carry_ref[...]
        att_w[...] = jnp.exp(lb - right - carry).astype(BF16)
        carry_ref[...] = carry + total
        jc = jnp.clip(last - t + 2, 0, last)
        v = v_ref[pl.ds(pl.multiple_of(jc * nb, nb), nb), :]
        for p in range(2):
            acc_ref[:, p * nb:(p + 1) * nb] += jnp.dot(att_r[:, p * pw:(p + 1) * pw],
                                                       _pair_rows(v, p),
                                                       preferred_element_type=F32)

    def four_trips(masked, u, carry_unused):
        stages(4 * u, z_a, att_a, z_b, att_b, masked)
        stages(4 * u + 1, z_b, att_b, z_a, att_a, masked)
        stages(4 * u + 2, z_a, att_a, z_b, att_b, masked)
        stages(4 * u + 3, z_b, att_b, z_a, att_a, masked)
        return carry_unused

    four_trips(True, 0, 0)
    n_plain = jnp.maximum((last - 2) // 4, 0)
    lax.fori_loop(1, n_plain + 1, functools.partial(four_trips, False), 0)
    lax.fori_loop(n_plain + 1, (last + 6) // 4, functools.partial(four_trips, True), 0)
    o_ref[...] = acc_ref[...]


def _sb_prompt(q, k, v, bias, batch, seq):
    n = q.shape[0]
    nq = seq // SB_QROWS
    hw = H_A * SB_BLOCK
    stage_bufs = [pltpu.VMEM((SB_QROWS, hw), F32),
                  pltpu.VMEM((SB_QROWS, hw), BF16)]
    return pl.pallas_call(
        _sb_prompt_kernel,
        grid=(batch, nq),
        in_specs=[pl.BlockSpec(memory_space=pltpu.SMEM),
                  pl.BlockSpec((SB_QROWS, W_Q), lambda b, i: (b * nq + i, 0)),
                  pl.BlockSpec((seq, hw), lambda b, i: (b, 0)),
                  pl.BlockSpec((seq, hw), lambda b, i: (b, 0)),
                  pl.BlockSpec((2 * SB_BLOCK, 2 * SB_BLOCK), lambda b, i: (0, 0))],
        out_specs=pl.BlockSpec((SB_QROWS, W_Q), lambda b, i: (b * nq + i, 0)),
        scratch_shapes=[pltpu.VMEM((SB_QROWS, hw), F32),
                        pltpu.VMEM((SB_QROWS, W_Q), F32)] + stage_bufs * 2,
        out_shape=jax.ShapeDtypeStruct((n, W_Q), F32),
        compiler_params=_cparams("parallel", "arbitrary"),
        name="sb_prompt",
    )(bias, q, k, v, _suffix_ones_hilo())


SB_ROWS = 8


def _sb_sample_kernel(n_pages, pt_ref, bias_ref, q_ref, kn_ref, vn_ref, uo_ref, *rest):
    kpages = rest[:n_pages]
    vpages = rest[n_pages:2 * n_pages]
    o_ref = rest[2 * n_pages]
    m = H_A * SB_ROWS
    row = _iota((m, W_Q), 0)
    lane = _iota((m, W_Q), 1)
    own = (row >> 3) == (lane >> 6)
    qbd = jnp.where(own, jnp.concatenate([q_ref[0]] * H_A, axis=0), jnp.zeros((m, W_Q), BF16))
    nblk = n_pages + 1
    width = nblk * SB_BLOCK
    rowh = _iota((m, width), 0) >> 3
    key = _iota((m, width), 1)
    bias = jnp.zeros((m, width), F32)
    for h in range(H_A):
        bias = jnp.where(rowh == h, bias_ref[h], bias)
    visible = (key < n_pages * SB_BLOCK) | ((key - n_pages * SB_BLOCK) < (_iota((m, width), 0) & (SB_ROWS - 1)))

    def page_t(ref):
        return ref[...].reshape(W_Q, SB_BLOCK).astype(BF16)

    pad = jnp.zeros((SB_BLOCK - SB_ROWS, W_Q), F32)
    kn = jnp.concatenate([kn_ref[0], pad], axis=0)
    vn = jnp.concatenate([vn_ref[0], pad], axis=0)
    kt_all = jnp.concatenate([page_t(r) for r in kpages], axis=1)
    z = jnp.concatenate([jnp.dot(qbd, kt_all, preferred_element_type=F32), _bdot_nt(qbd, kn)],
                        axis=1) + bias
    l = jnp.log(1.0 + jnp.exp2(jnp.abs(z) * (-LOG2E)))
    nlk = jnp.maximum(z, 0.0) + l
    lb = jnp.where(visible, z - nlk, NEG_BIG)
    hi, lo = _split(jnp.where(visible, nlk, 0.0))
    blocks = [slice(p * SB_BLOCK, (p + 1) * SB_BLOCK) for p in range(nblk)]
    sums = jnp.dot(jnp.concatenate([jnp.concatenate([hi[:, s], lo[:, s]], axis=1) for s in blocks],
                                   axis=0), uo_ref[...], preferred_element_type=F32)
    right = jnp.concatenate([sums[p * m:(p + 1) * m, :SB_BLOCK] for p in range(nblk)], axis=1)
    carry = jnp.zeros((m, SB_BLOCK), F32)
    carries = [None] * nblk
    for p in range(nblk - 1, -1, -1):
        carries[p] = carry
        carry = carry + sums[p * m:(p + 1) * m, SB_BLOCK:]
    att = jnp.exp(lb - right - jnp.concatenate(carries, axis=1)).astype(BF16)
    vt_all = jnp.concatenate([page_t(r) for r in vpages], axis=1)
    acc = (lax.dot_general(att[:, :n_pages * SB_BLOCK], vt_all, (((1,), (1,)), ((), ())),
                           preferred_element_type=F32)
           + _bdot(att[:, n_pages * SB_BLOCK:], vn))
    acc = jnp.where(own, acc, 0.0)
    out = acc[0:SB_ROWS]
    for h in range(1, H_A):
        out = out + acc[h * SB_ROWS:(h + 1) * SB_ROWS]
    o_ref[0] = out


def _sb_sample(q8, k8, v8, cache_kt, cache_vt, page_table, bias, layer):
    bsz, n_pages = page_table.shape
    page = cache_kt.shape[4]

    def page_spec(p):
        return pl.BlockSpec((None, None, H_A, DH_A, page), lambda b, pt: (layer, pt[b, p], 0, 0, 0))

    tok = pl.BlockSpec((1, SB_ROWS, W_Q), lambda b, pt: (b, 0, 0))
    return pl.pallas_call(
        functools.partial(_sb_sample_kernel, n_pages),
        grid_spec=pltpu.PrefetchScalarGridSpec(
            num_scalar_prefetch=1,
            grid=(bsz,),
            in_specs=([pl.BlockSpec(memory_space=pltpu.SMEM), tok, tok, tok,
                       pl.BlockSpec((2 * SB_BLOCK, 2 * SB_BLOCK), lambda b, pt: (0, 0))]
                      + [page_spec(p) for p in range(n_pages)] * 2),
            out_specs=tok),
        out_shape=jax.ShapeDtypeStruct((bsz, SB_ROWS, W_Q), F32),
        compiler_params=_cparams("parallel"),
        name="sb_sample",
    )(page_table, bias, q8, k8, v8, _suffix_ones_hilo(), *([cache_kt] * n_pages),
      *([cache_vt] * n_pages))


def _gmlp_prompt_kernel(u_ref, v_ref, ws_ref, bs_ref, g_ref, b_ref, o_ref):
    c = GMLP_CHUNK
    ones = _group_ones(W_Q)
    tril = _iota((c, c), 0) >= _iota((c, c), 1)
    lane_group = _iota((c, W_Q), 1) >> 6
    w = [jnp.where(tril, ws_ref[g], 0.0).astype(BF16) for g in range(G_B)]
    for n in range(u_ref.shape[0] // c):
        rows = slice(n * c, (n + 1) * c)
        u = _gelu_tanh(u_ref[rows, :])
        vn = _group_layer_norm(_gelu_tanh(v_ref[rows, :]), ones, g_ref[...], b_ref[...])
        vnb = vn.astype(BF16)
        mixed = bs_ref[...]
        for g in range(G_B):
            vg = jnp.where(lane_group == g, vnb, jnp.zeros_like(vnb))
            mixed = mixed + jnp.dot(w[g], vg, preferred_element_type=F32)
        o_ref[rows, :] = u * mixed


def _gmlp_prompt(ub, vb, ws, bs_lanes, g, b, tm):
    n = ub.shape[0]
    c = GMLP_CHUNK
    tile = pl.BlockSpec((tm, W_Q), lambda i: (i, 0))
    vec = pl.BlockSpec((1, W_Q), lambda i: (0, 0))
    return pl.pallas_call(
        _gmlp_prompt_kernel,
        grid=(n // tm,),
        in_specs=[tile, tile,
                  pl.BlockSpec((G_B, c, c), lambda i: (0, 0, 0)),
                  pl.BlockSpec((c, W_Q), lambda i: (0, 0)),
                  vec, vec],
        out_specs=tile,
        out_shape=jax.ShapeDtypeStruct((n, W_Q), F32),
        compiler_params=_cparams("parallel"),
        name="gmlp_prompt",
    )(ub, vb, ws, bs_lanes, g, b)


HALO = 16


def _pool_windows(load, t, start_pos):
    lane = _iota((t, SB_BLOCK), 1)
    pos = start_pos + _iota((t, SB_BLOCK), 0)
    halves = []
    for half, (w_small, w_big) in enumerate(((POOL_WINDOWS[0], POOL_WINDOWS[1]),
                                              (POOL_WINDOWS[2], POOL_WINDOWS[3]))):
        s = load(0, half)
        for i in range(1, w_small):
            s = s + load(i, half)
        big = s
        for i in range(w_small, w_big):
            big = big + load(i, half)
        small_lane = lane < GROUP_LANES
        win = jnp.where(small_lane, s, big)
        w = jnp.where(small_lane, w_small, w_big)
        cnt = jnp.minimum(pos + 1, w).astype(F32)
        halves.append(win / cnt)
    return jnp.concatenate(halves, axis=1)


def _pool_prompt_kernel(tiles_per_seq, x_ref, halo_ref, w_ref, gamma_ref, o_ref, xx_ref):
    i = pl.program_id(0)
    t = x_ref.shape[0]
    first = (i % tiles_per_seq) == 0
    xx_ref[0:HALO, :] = jnp.where(first, 0.0, halo_ref[...])
    xx_ref[HALO:, :] = x_ref[...]

    def load(shift, half):
        return xx_ref[pl.ds(HALO - shift, t), half * SB_BLOCK:(half + 1) * SB_BLOCK]

    start = (i % tiles_per_seq) * t
    means = _pool_windows(load, t, start)
    d = means - x_ref[...]
    o_ref[...] = _bdot(d, w_ref[...]) * gamma_ref[...]


def _pool_prompt(xc, w_bd, gamma, seq, tm):
    n = xc.shape[0]
    hb = tm // HALO
    return pl.pallas_call(
        functools.partial(_pool_prompt_kernel, seq // tm),
        grid=(n // tm,),
        in_specs=[pl.BlockSpec((tm, W_Q), lambda i: (i, 0)),
                  pl.BlockSpec((HALO, W_Q), lambda i: (jnp.maximum(i * hb - 1, 0), 0)),
                  pl.BlockSpec((W_Q, W_Q), lambda i: (0, 0)),
                  pl.BlockSpec((1, W_Q), lambda i: (0, 0))],
        out_specs=pl.BlockSpec((tm, W_Q), lambda i: (i, 0)),
        out_shape=jax.ShapeDtypeStruct((n, W_Q), F32),
        scratch_shapes=[pltpu.VMEM((tm + HALO, W_Q), F32)],
        compiler_params=_cparams("parallel"),
        name="pool_prompt",
    )(xc, xc, w_bd, gamma)


CONV_HALO = 8


def _delta_gates(bd, ad, a_log, dt_bias):
    beta = _sigmoid(bd)
    g = -jnp.exp(a_log) * _softplus(ad + dt_bias)
    return beta, g


def _delta_prep_kernel(tiles_per_seq, x_ref, halo_ref, bd_ref, ad_ref, cw_ref, alog_ref, dt_ref,
                       q_ref, k_ref, v_ref, g_ref, b_ref, xx_ref):
    i = pl.program_id(0)
    t = x_ref.shape[0]
    first = (i % tiles_per_seq) == 0
    xx_ref[0:CONV_HALO, :] = jnp.where(first, 0.0, halo_ref[...])
    xx_ref[CONV_HALO:, :] = x_ref[...]
    ones = _group_ones(W_Q)
    off = CONV_HALO - (CONV_K - 1)
    for part, out in enumerate((q_ref, k_ref, v_ref)):
        lanes = slice(part * W_Q, (part + 1) * W_Q)
        y = xx_ref[pl.ds(off, t), lanes] * cw_ref[0:1, lanes]
        for j in range(1, CONV_K):
            y = y + xx_ref[pl.ds(off + j, t), lanes] * cw_ref[j:j + 1, lanes]
        y = _silu(y)
        out[...] = y if part == 2 else _l2_normalize(y, ones)
    beta, g = _delta_gates(bd_ref[...], ad_ref[...], alog_ref[...], dt_ref[...])
    b_ref[...] = beta
    g_ref[...] = g


def _delta_prep(qkv, bd, ad, conv_w, a_log, dt_bias, seq, tm):
    n, wd = qkv.shape
    hb = tm // CONV_HALO
    tile = pl.BlockSpec((tm, W_Q), lambda i: (i, 0))
    vec = pl.BlockSpec((1, W_Q), lambda i: (0, 0))
    return pl.pallas_call(
        functools.partial(_delta_prep_kernel, seq // tm),
        grid=(n // tm,),
        in_specs=[pl.BlockSpec((tm, wd), lambda i: (i, 0)),
                  pl.BlockSpec((CONV_HALO, wd), lambda i: (jnp.maximum(i * hb - 1, 0), 0)),
                  tile, tile,
                  pl.BlockSpec((CONV_K, wd), lambda i: (0, 0)),
                  vec, vec],
        out_specs=[tile] * 5,
        out_shape=[jax.ShapeDtypeStruct((n, W_Q), F32)] * 5,
        scratch_shapes=[pltpu.VMEM((tm + CONV_HALO, wd), F32)],
        compiler_params=_cparams("parallel"),
        name="delta_prep",
    )(qkv, qkv, bd, ad, conv_w, a_log, dt_bias)


def _dot3_shared(lhs, b):
    bh, bl = _split(b)
    parts = [_split(a) for a in lhs]
    his = [h for h, _ in parts]
    los = [l for _, l in parts]
    n, rows = len(lhs), lhs[0].shape[0]
    r_h = jnp.dot(jnp.concatenate(his + los, axis=0), bh, preferred_element_type=F32)
    r_l = jnp.dot(jnp.concatenate(his, axis=0), bl, preferred_element_type=F32)
    piece = lambda r, i: r[i * rows:(i + 1) * rows]
    return [piece(r_h, i) + piece(r_h, n + i) + piece(r_l, i) for i in range(n)]


def _delta_kernel(n_seq, n_sub, q_ref, k_ref, v_ref, g_ref, b_ref, zg_ref, s0_ref, gn_ref, o_ref,
                  sf_ref, s_ref):
    c = pl.program_id(0)
    n_steps = pl.num_programs(0)
    cs = DELTA_CHUNK
    hw = H_D * DK_D
    row = _iota((cs, hw), 0)
    col = _iota((cs, hw), 1) & (DK_D - 1)
    incl = row >= col
    strict = row > col
    bdmask = (_iota((hw, hw), 0) >> 6) == (_iota((hw, hw), 1) >> 6)
    ones = _group_ones(hw)
    tril = jnp.where(_iota((cs, cs), 0) >= _iota((cs, cs), 1), 1.0, 0.0).astype(BF16)

    def bd(x):
        return jnp.where(bdmask, jnp.concatenate([x] * H_D, axis=0), 0.0)

    @pl.when(c == 0)
    def _():
        for s in range(n_seq):
            s_ref[s] = jnp.where(bdmask, jnp.concatenate([s0_ref[s]] * H_D, axis=1), 0.0)

    chunks = [(s, slice(n * cs, (n + 1) * cs)) for n in range(n_sub) for s in range(n_seq)]
    each = lambda f, *xs: [f(*x) for x in zip(*xs)]
    q = [q_ref[s, r, :] * (DK_D ** -0.5) for s, r in chunks]
    k = [k_ref[s, r, :] for s, r in chunks]
    g = [g_ref[s, r, :] for s, r in chunks]
    beta = [b_ref[s, r, :] for s, r in chunks]
    vb = [v_ref[s, r, :] * bt for (s, r), bt in zip(chunks, beta)]
    kb = each(lambda x, bt: x * bt, k, beta)
    gc = [_dot_exact_lhs(tril, x) for x in g]
    diff = [_dot_exact_lhs(tril, jnp.where(strict, x, 0.0)) for x in g]
    decay = [jnp.where(incl, jnp.exp(jnp.where(incl, x, 0.0)), 0.0) for x in diff]
    eg = [jnp.exp(x) for x in gc]
    kbd = [bd(x).astype(BF16) for x in k]
    neg_lower = each(lambda x, y, dc: jnp.where(strict, -(_bdot_nt(x, y) * dc), 0.0), kb, kbd, decay)
    eye = jnp.where(row == col, 1.0, 0.0)
    t_inv = [eye + x for x in neg_lower]
    p = [_dot3_shared([x], bd(x))[0] for x in neg_lower]
    for _ in range(4):
        both = each(lambda x, t: _dot3_shared([x, t], bd(x)), p, t_inv)
        t_inv = each(lambda t, r: t + r[1], t_inv, both)
        p = [r[0] for r in both]
    t_inv = each(lambda t, x: t + _dot3_shared([t], bd(x))[0], t_inv, p)
    u = each(lambda t, x: _bdot(t, bd(x)), t_inv, vb)
    w = each(lambda t, x, e: _bdot(t, bd(x * e)), t_inv, kb, eg)
    intra = each(lambda x, y, dc: jnp.where(incl, _bdot_nt(x, y) * dc, 0.0), q, kbd, decay)
    gc_last = [x[cs - 1:cs, :] for x in gc]
    qg = each(lambda x, e: x * e, q, eg)
    kg = each(lambda x, gl, c_: x * jnp.exp(gl - c_), k, gc_last, gc)
    g_last = [jnp.exp(x) for x in gc_last]
    state = [s_ref[s] for s in range(n_seq)]
    for i, (s, rows) in enumerate(chunks):
        v_new = u[i] - _bdot(w[i], state[s])
        o = _bdot(qg[i], state[s]) + _bdot(intra[i], bd(v_new))
        state[s] = state[s] * g_last[i] + jnp.where(bdmask, _bdot_tn(kg[i], v_new), 0.0)
        o = o * lax.rsqrt(_group_sum(o * o, ones) * (1.0 / DK_D) + 1e-6) * gn_ref[...]
        o_ref[s, rows, :] = o * _silu(zg_ref[s, rows, :])
    for s in range(n_seq):
        s_ref[s] = state[s]

    @pl.when(c == n_steps - 1)
    def _():
        for s in range(n_seq):
            sf = state[s][:, 0:DK_D]
            for h in range(1, H_D):
                sf = sf + state[s][:, h * DK_D:(h + 1) * DK_D]
            sf_ref[s] = sf


def _delta(q, k, v, g, beta, zg, s0, gn, n_sub):
    bsz, t, hw = q.shape
    rows = n_sub * DELTA_CHUNK
    tok = pl.BlockSpec((bsz, rows, hw), lambda c: (0, c, 0))
    st = pl.BlockSpec((bsz, hw, DK_D), lambda c: (0, 0, 0))
    return pl.pallas_call(
        functools.partial(_delta_kernel, bsz, n_sub),
        grid=(t // rows,),
        in_specs=[tok] * 6 + [st, pl.BlockSpec((1, hw), lambda c: (0, 0))],
        out_specs=[tok, st],
        out_shape=[jax.ShapeDtypeStruct((bsz, t, hw), F32),
                   jax.ShapeDtypeStruct((bsz, hw, DK_D), F32)],
        scratch_shapes=[pltpu.VMEM((bsz, hw, hw), F32)],
        compiler_params=_cparams("arbitrary"),
        name="delta",
    )(q, k, v, g, beta, zg, s0, gn)


def _delta_lanes_kernel(q_ref, k_ref, v_ref, g_ref, b_ref, zg_ref, gn_ref, s0_ref, o_ref, sf_ref,
                        qt_ref, kt_ref):
    t_new, bsz, hw = q_ref.shape
    ones = _group_ones(hw)
    sf_ref[...] = s0_ref[...]
    zero = jnp.zeros((DK_D, bsz), F32)
    for t in range(t_new):
        qt_ref[...] = jnp.transpose(q_ref[t]) * (DK_D ** -0.5)
        kt_ref[...] = jnp.transpose(k_ref[t])
        vt = jnp.transpose(v_ref[t])
        at = jnp.exp(jnp.transpose(g_ref[t]))
        bt = jnp.transpose(b_ref[t])
        outs = []
        for h in range(H_D):
            base = h * DK_D
            a = at[base:base + 1, :]
            beta = bt[base:base + 1, :]

            def k_dot_s(dk, acc, h=h, base=base):
                return acc + kt_ref[pl.ds(base + dk, 1), :] * sf_ref[h, dk]

            ks = lax.fori_loop(0, DK_D, k_dot_s, zero, unroll=8)
            delta = beta * (vt[base:base + DK_D, :] - a * ks)

            def update(dk, acc, h=h, base=base, a=a, delta=delta):
                s_new = a * sf_ref[h, dk] + kt_ref[pl.ds(base + dk, 1), :] * delta
                sf_ref[h, dk] = s_new
                return acc + qt_ref[pl.ds(base + dk, 1), :] * s_new

            outs.append(lax.fori_loop(0, DK_D, update, zero, unroll=8))
        o = jnp.transpose(jnp.concatenate(outs, axis=0))
        o = o * lax.rsqrt(_group_sum(o * o, ones) * (1.0 / DK_D) + 1e-6) * gn_ref[...]
        o_ref[t] = o * _silu(zg_ref[t])


def _delta_lanes(q, k, v, g, beta, zg, gn, s0):
    t_new, bsz, hw = q.shape
    return pl.pallas_call(
        _delta_lanes_kernel,
        out_shape=[jax.ShapeDtypeStruct((t_new, bsz, hw), F32),
                   jax.ShapeDtypeStruct(s0.shape, F32)],
        scratch_shapes=[pltpu.VMEM((hw, bsz), F32), pltpu.VMEM((hw, bsz), F32)],
        compiler_params=pltpu.CompilerParams(vmem_limit_bytes=VMEM_LIMIT),
        name="delta_lanes",
    )(q, k, v, g, beta, zg, gn, s0)


def _sample_mix_kernel(start_pos, u_ref, v_ref, xc_ref, qkv_ref, bd_ref, ad_ref, pool_prev_ref,
                       conv_prev_ref, wl_ref, bsl_ref, lng_ref, lnb_ref, pw_ref, gamma_ref, cw_ref,
                       alog_ref, dt_ref,
                       ob_ref, vn_ref, oc_ref, pool_new_ref, conv_new_ref, q_ref, k_ref, vd_ref,
                       g_ref, b_ref):
    t_new = u_ref.shape[0]
    bsz = u_ref.shape[1]
    ones = _group_ones(W_Q)
    vn = [_group_layer_norm(_gelu_tanh(v_ref[t]), ones, lng_ref[...], lnb_ref[...])
          for t in range(t_new)]
    for t in range(t_new):
        vn_ref[t] = vn[t]
        mixed = bsl_ref[t:t + 1, :] + wl_ref[t, 0:1, :] * vn[0]
        for s in range(1, t + 1):
            mixed = mixed + wl_ref[t, s:s + 1, :] * vn[s]
        ob_ref[t] = _gelu_tanh(u_ref[t]) * mixed
    n_prev = pool_prev_ref.shape[0]

    def pool_row(i):
        return pool_prev_ref[i] if i < n_prev else xc_ref[i - n_prev]

    lane = _iota((bsz, W_Q), 1) >> 6
    for t in range(t_new):
        acc = pool_row(n_prev + t)
        win = None
        done = 1
        for gi, wdw in enumerate(POOL_WINDOWS):
            for i in range(done, wdw):
                acc = acc + pool_row(n_prev + t - i)
            done = wdw
            cnt = float(min(start_pos + t + 1, wdw))
            mean = acc / cnt
            win = mean if win is None else jnp.where(lane >= gi, mean, win)
        d = win - xc_ref[t]
        oc_ref[t] = _bdot(d, pw_ref[...]) * gamma_ref[...]
    for i in range(n_prev):
        pool_new_ref[i] = pool_row(t_new + i)
    c_prev = conv_prev_ref.shape[0]

    def conv_row(i):
        return conv_prev_ref[i] if i < c_prev else qkv_ref[i - c_prev]

    for t in range(t_new):
        y = conv_row(t) * cw_ref[0:1, :]
        for j in range(1, CONV_K):
            y = y + conv_row(t + j) * cw_ref[j:j + 1, :]
        y = _silu(y)
        q_ref[t] = _l2_normalize(y[:, 0:W_Q], ones)
        k_ref[t] = _l2_normalize(y[:, W_Q:2 * W_Q], ones)
        vd_ref[t] = y[:, 2 * W_Q:3 * W_Q]
        beta, g = _delta_gates(bd_ref[t], ad_ref[t], alog_ref[...], dt_ref[...])
        b_ref[t] = beta
        g_ref[t] = g
    for i in range(c_prev):
        conv_new_ref[i] = conv_row(t_new + i)


def _sample_mix(start_pos, ub, vb, xc, qkv, bd, ad, pool_prev, conv_prev, wl, bsl, lng, lnb, pw,
                gamma, cw, a_log, dt_bias):
    t_new, bsz, _ = ub.shape
    slab = jax.ShapeDtypeStruct((t_new, bsz, W_Q), F32)
    out_shape = [slab, slab, slab,
                 jax.ShapeDtypeStruct(pool_prev.shape, F32),
                 jax.ShapeDtypeStruct(conv_prev.shape, F32),
                 slab, slab, slab, slab, slab]
    return pl.pallas_call(
        functools.partial(_sample_mix_kernel, start_pos),
        out_shape=out_shape,
        compiler_params=pltpu.CompilerParams(vmem_limit_bytes=VMEM_LIMIT),
        name="sample_mix",
    )(ub, vb, xc, qkv, bd, ad, pool_prev, conv_prev, wl, bsl, lng, lnb, pw, gamma, cw, a_log,
      dt_bias)


def _mm_ln_kernel(n_in, *refs):
    xs = refs[:n_in]
    ws = refs[n_in:2 * n_in]
    h_ref, g_ref, b_ref, o_ref = refs[2 * n_in:]
    acc = ALPHA * h_ref[...]
    for x_ref, w_ref in zip(xs, ws):
        acc = acc + _bdot(x_ref[...], w_ref[...])
    o_ref[...] = _layer_norm(acc, g_ref[...], b_ref[...])


def _mm_ln(xs, ws, h, g, b, tm):
    n, d = h.shape
    n_in = len(xs)
    return pl.pallas_call(
        functools.partial(_mm_ln_kernel, n_in),
        grid=(n // tm,),
        in_specs=([pl.BlockSpec((tm, x.shape[1]), lambda i: (i, 0)) for x in xs]
                  + [pl.BlockSpec(w.shape, lambda i: (0, 0)) for w in ws]
                  + [pl.BlockSpec((tm, d), lambda i: (i, 0)),
                     pl.BlockSpec((1, d), lambda i: (0, 0)),
                     pl.BlockSpec((1, d), lambda i: (0, 0))]),
        out_specs=pl.BlockSpec((tm, d), lambda i: (i, 0)),
        out_shape=jax.ShapeDtypeStruct((n, d), F32),
        compiler_params=_cparams("parallel"),
        name="mm_ln",
    )(*xs, *ws, h, g.reshape(1, d), b.reshape(1, d))


def _mm_kernel(scale, x_ref, w_ref, o_ref, ob_ref):
    y = _bdot(x_ref[...], w_ref[...])
    o_ref[...] = y
    ob_ref[...] = (y * scale).astype(BF16)


def _mm(x, w, tm, scale=1.0):
    n, d = x.shape
    m = w.shape[1]
    return pl.pallas_call(
        functools.partial(_mm_kernel, scale),
        grid=(n // tm,),
        in_specs=[pl.BlockSpec((tm, d), lambda i: (i, 0)),
                  pl.BlockSpec((d, m), lambda i: (0, 0))],
        out_specs=[pl.BlockSpec((tm, m), lambda i: (i, 0))] * 2,
        out_shape=[jax.ShapeDtypeStruct((n, m), F32), jax.ShapeDtypeStruct((n, m), BF16)],
        compiler_params=_cparams("parallel"),
        name="mm",
    )(x, w)


def _xattn_heads(q, k_ref, v_ref):
    dh = q.shape[1] // XA_HEADS
    outs = []
    for h in range(XA_HEADS):
        lanes = slice(h * dh, (h + 1) * dh)
        s = _bdot_nt(q[:, lanes], k_ref[0, :, lanes])
        s = s - jnp.max(s, axis=-1, keepdims=True)
        e = jnp.exp(s)
        p = e / jnp.sum(e, axis=-1, keepdims=True)
        outs.append(_bdot(p, v_ref[0, :, lanes]))
    return jnp.concatenate(outs, axis=1)


def _xattn_prompt_kernel(h_ref, wq_ref, wo_ref, k_ref, v_ref, g_ref, b_ref, o_ref):
    h = h_ref[...]
    dh = h.shape[1] // XA_HEADS
    q = (_bdot(h, wq_ref[...]) * (dh ** -0.5)).astype(BF16)
    o = _xattn_heads(q, k_ref, v_ref)
    o_ref[...] = _layer_norm(ALPHA * h + _bdot(o, wo_ref[...]), g_ref[...], b_ref[...])


def _xattn_prompt(h, wq, wo, mem_k, mem_v, g, b, seq, tm):
    n, d = h.shape
    n_mem = mem_k.shape[1]
    tps = seq // tm
    full = pl.BlockSpec((d, d), lambda i: (0, 0))
    mem = pl.BlockSpec((1, n_mem, d), lambda i: (i // tps, 0, 0))
    vec = pl.BlockSpec((1, d), lambda i: (0, 0))
    return pl.pallas_call(
        _xattn_prompt_kernel,
        grid=(n // tm,),
        in_specs=[pl.BlockSpec((tm, d), lambda i: (i, 0)), full, full, mem, mem, vec, vec],
        out_specs=pl.BlockSpec((tm, d), lambda i: (i, 0)),
        out_shape=jax.ShapeDtypeStruct((n, d), F32),
        compiler_params=_cparams("parallel"),
        name="xattn_prompt",
    )(h, wq, wo, mem_k, mem_v, g.reshape(1, d), b.reshape(1, d))


def _xattn_sample_kernel(q_ref, k_ref, v_ref, o_ref):
    q = q_ref[0]
    n_mem, n_heads, dh = k_ref.shape
    rows = q.shape[0]
    k2 = k_ref[...].reshape(n_mem * n_heads, dh).astype(BF16)
    v2 = v_ref[...].reshape(n_mem * n_heads, dh).astype(BF16)
    qs = jnp.concatenate([q[:, h * dh:(h + 1) * dh] for h in range(n_heads)], axis=0)
    shape = (n_heads * rows, n_mem * n_heads)
    own = (_iota(shape, 0) // rows) == (_iota(shape, 1) & (n_heads - 1))
    s = jnp.where(own, _bdot_nt(qs, k2), NEG_BIG)
    s = s - jnp.max(s, axis=-1, keepdims=True)
    e = jnp.exp(s)
    p = e / jnp.sum(e, axis=-1, keepdims=True)
    o = _bdot(p, v2)
    o_ref[0] = jnp.concatenate([o[h * rows:(h + 1) * rows] for h in range(n_heads)], axis=1)


def _xattn_sample(q8, mem_k, mem_v, layer):
    bsz, rows, d = q8.shape
    _, _, n_mem, n_heads, dh = mem_k.shape
    tok = pl.BlockSpec((1, rows, d), lambda b: (b, 0, 0))
    mem = pl.BlockSpec((None, None, n_mem, n_heads, dh), lambda b: (layer, b, 0, 0, 0))
    return pl.pallas_call(
        _xattn_sample_kernel,
        grid=(bsz,),
        in_specs=[tok, mem, mem],
        out_specs=tok,
        out_shape=jax.ShapeDtypeStruct((bsz, rows, d), F32),
        compiler_params=_cparams("parallel"),
        name="xattn_sample",
    )(q8, mem_k, mem_v)


def _top2_sum(a, b, c, d):
    m1, n1 = jnp.maximum(a, b), jnp.minimum(a, b)
    m2, n2 = jnp.maximum(c, d), jnp.minimum(c, d)
    return jnp.maximum(m1, m2) + jnp.maximum(jnp.minimum(m1, m2), jnp.maximum(n1, n2))


def _router_kernel(x_ref, wt_ref, bias_ref, o_ref):
    x = x_ref[...]
    wt = wt_ref[...]
    xh, xl = _split(x)
    wh, wl = _split(wt)
    nt = (((1,), (1,)), ((), ()))
    logits = (lax.dot_general(wh, xh, nt, preferred_element_type=F32)
              + lax.dot_general(wh, xl, nt, preferred_element_type=F32)
              + lax.dot_general(wl, xh, nt, preferred_element_type=F32))
    s_all = _sigmoid(logits)
    sel_all = s_all + bias_ref[...]
    s = [s_all[e:e + 1, :] for e in range(N_EXPERTS)]
    sel = [sel_all[e:e + 1, :] for e in range(N_EXPERTS)]
    epg = EXPERTS_PER_GROUP
    score = [_top2_sum(*sel[g * epg:(g + 1) * epg]) for g in range(N_GROUPS)]
    best = functools.reduce(jnp.maximum, score)
    taken = None
    in_group = []
    for g in range(N_GROUPS):
        hit = score[g] == best
        if taken is None:
            pick, taken = hit, hit
        else:
            pick = hit & jnp.logical_not(taken)
            taken = taken | hit
        in_group.append(pick)
    chosen = []
    for g in range(N_GROUPS):
        vals = sel[g * epg:(g + 1) * epg]
        for j in range(epg):
            ahead = jnp.zeros_like(vals[j])
            for i in range(epg):
                if i == j:
                    continue
                before = (vals[i] >= vals[j]) if i < j else (vals[i] > vals[j])
                ahead = ahead + jnp.where(before, 1.0, 0.0)
            chosen.append(in_group[g] & (ahead < 1.5))
    gates = [jnp.where(chosen[e], s[e], 0.0) for e in range(N_EXPERTS)]
    total = functools.reduce(lambda a, b: a + b, gates)
    o_ref[...] = jnp.concatenate(gates, axis=0) / total


def _router(x, w_router_t, bias, tm):
    n, d = x.shape
    e = w_router_t.shape[0]
    return pl.pallas_call(
        _router_kernel,
        grid=(n // tm,),
        in_specs=[pl.BlockSpec((tm, d), lambda i: (i, 0)),
                  pl.BlockSpec((e, d), lambda i: (0, 0)),
                  pl.BlockSpec((e, 1), lambda i: (0, 0))],
        out_specs=pl.BlockSpec((e, tm), lambda i: (0, i)),
        out_shape=jax.ShapeDtypeStruct((e, n), F32),
        compiler_params=_cparams("parallel"),
        name="router",
    )(x, w_router_t, bias.reshape(e, 1))


def _moe_kernel(x_ref, comb_ref, wg_ref, wu_ref, wd_ref, g_ref, b_ref, o_ref, xb_ref, acc_ref):
    e = pl.program_id(1)

    @pl.when(e == 0)
    def _():
        xb_ref[...] = x_ref[...].astype(BF16)
        acc_ref[...] = jnp.zeros_like(acc_ref)

    xb = xb_ref[...]
    hg = jnp.dot(xb, wg_ref[0].astype(BF16), preferred_element_type=F32)
    hu = jnp.dot(xb, wu_ref[0].astype(BF16), preferred_element_type=F32)
    comb = comb_ref[...]
    gate = jnp.sum(jnp.where(_iota(comb.shape, 1) == e, comb, 0.0), axis=1, keepdims=True)
    act = _silu(hg) * hu * gate
    acc_ref[...] += _bdot(act, wd_ref[0])

    @pl.when(e == pl.num_programs(1) - 1)
    def _():
        o_ref[...] = _layer_norm(ALPHA * x_ref[...] + acc_ref[...], g_ref[...], b_ref[...])


def _moe(x, comb, wg, wu, wd, g, b, tm, layer):
    n, d = x.shape
    ne = comb.shape[1]
    f = wg.shape[2]
    base = layer * ne
    return pl.pallas_call(
        _moe_kernel,
        grid=(n // tm, ne),
        in_specs=[pl.BlockSpec((tm, d), lambda i, e: (i, 0)),
                  pl.BlockSpec((tm, ne), lambda i, e: (i, 0)),
                  pl.BlockSpec((1, d, f), lambda i, e: (base + e, 0, 0)),
                  pl.BlockSpec((1, d, f), lambda i, e: (base + e, 0, 0)),
                  pl.BlockSpec((1, f, d), lambda i, e: (base + e, 0, 0)),
                  pl.BlockSpec((1, d), lambda i, e: (0, 0)),
                  pl.BlockSpec((1, d), lambda i, e: (0, 0))],
        out_specs=pl.BlockSpec((tm, d), lambda i, e: (i, 0)),
        out_shape=jax.ShapeDtypeStruct((n, d), F32),
        scratch_shapes=[pltpu.VMEM((tm, d), BF16), pltpu.VMEM((tm, d), F32)],
        compiler_params=_cparams("parallel", "arbitrary"),
        name="moe",
    )(x, comb, wg, wu, wd, g.reshape(1, d), b.reshape(1, d))


def _lanes(x):
    if x.ndim == 1:
        x = jnp.repeat(x, GROUP_LANES)
    return x.reshape(1, W_Q).astype(F32)


def _block_diag(w):
    g, a, b = w.shape
    eye = jnp.eye(g, dtype=w.dtype)
    return (eye[:, None, :, None] * w[:, :, None, :]).reshape(g * a, g * b)


def _tile(n, pref):
    t = min(n, pref)
    while n % t:
        t //= 2
    return t


def kernel(x_prompt, x_sample, cache_sb_k, cache_sb_v, state_pool, state_conv, state_delta,
           cache_mem_k, cache_mem_v, page_table, mem_prompt,
           ln0_g, ln0_b, w_in, sb_bias, gmlp_ln_g, gmlp_ln_b, gmlp_ws, gmlp_bs, pool_w, pool_gamma,
           conv_w, dn_a_log, dn_dt_bias, dn_norm_g, w_out, ln1_g, ln1_b,
           xa_wq, xa_wk, xa_wv, xa_wo, ln2_g, ln2_b, w_router, router_bias,
           ex_wg, ex_wu, ex_wd, ln3_g, ln3_b):
    bp, seq, d = x_prompt.shape
    bs, t_new, _ = x_sample.shape
    n_p = bp * seq
    n_s = bs * t_new
    n_mem = mem_prompt.shape[1]
    page = cache_sb_k.shape[2]
    past_len = page_table.shape[1] * page
    xa_dh = d // XA_HEADS
    tm_p = _tile(seq, 512)
    tm_s = _tile(n_s, 512)
    tm_moe_p = _tile(n_p, 1024)
    n_sub = 8 if seq % (8 * DELTA_CHUNK) == 0 else 1

    cache_kt = cache_sb_k.transpose(0, 1, 3, 4, 2)
    cache_vt = cache_sb_v.transpose(0, 1, 3, 4, 2)
    w_router_t = w_router.T
    wg, wu, wd = (w.reshape((DEPTH * N_EXPERTS,) + w.shape[2:]) for w in (ex_wg, ex_wu, ex_wd))
    mem_flat = mem_prompt.reshape(bp * n_mem, d)

    def rows8(x):
        x = x.reshape(bs, t_new, -1)
        return jnp.pad(x, ((0, 0), (0, SB_ROWS - t_new), (0, 0)))

    def tmajor(x):
        return x.reshape(bs, t_new, -1).transpose(1, 0, 2)

    def bmajor(x):
        return x.transpose(1, 0, 2)

    h_p = _ln(x_prompt.reshape(n_p, d), ln0_g, ln0_b, tm_p)
    h_s = _ln(x_sample.reshape(n_s, d), ln0_g, ln0_b, tm_s)

    outs = {k: [] for k in ("kp", "vp", "ks", "vs", "gv", "pp", "ps", "cp", "cs", "dp", "ds",
                            "mk", "mv")}
    for l in range(DEPTH):
        w_in_l = _prep_w_in(w_in[l])
        w_out_l = w_out[l].astype(BF16)
        w_out_parts = [w_out_l[i * W_Q:(i + 1) * W_Q] for i in range(4)]
        ws_l = gmlp_ws[l]
        bs_lanes = jnp.repeat(gmlp_bs[l].T, GROUP_LANES, axis=1)
        lng, lnb = _lanes(gmlp_ln_g[l]), _lanes(gmlp_ln_b[l])
        pw = _block_diag(pool_w[l]).astype(BF16)
        gamma = pool_gamma[l].reshape(1, W_Q)
        cw = conv_w[l]
        a_log, dt_b = _lanes(dn_a_log[l]), _lanes(dn_dt_bias[l])
        gn = jnp.tile(dn_norm_g[l], H_D).reshape(1, W_Q)
        wq, wo = xa_wq[l].astype(BF16), xa_wo[l].astype(BF16)

        mk, mk_b = _mm(mem_flat, xa_wk[l].astype(BF16), _tile(bp * n_mem, 256))
        mv, mv_b = _mm(mem_flat, xa_wv[l].astype(BF16), _tile(bp * n_mem, 256))
        outs["mk"].append(mk.reshape(bp, n_mem, XA_HEADS, xa_dh))
        outs["mv"].append(mv.reshape(bp, n_mem, XA_HEADS, xa_dh))

        (q, k, v, k_b, v_b, ub, vb, xc, qkv, zg, bd, ad) = _proj_in(h_p, w_in_l, tm_p)
        outs["kp"].append(k.reshape(bp, seq, H_A, DH_A))
        outs["vp"].append(v.reshape(bp, seq, H_A, DH_A))
        outs["pp"].append(xc.reshape(bp, seq, W_Q)[:, seq - POOL_BUF:])
        outs["cp"].append(qkv.reshape(bp, seq, 3 * W_Q)[:, seq - (CONV_K - 1):])
        oa = _sb_prompt(q, k_b, v_b, sb_bias[l], bp, seq)
        ob = _gmlp_prompt(ub, vb, ws_l, bs_lanes, lng, lnb, tm_p)
        oc = _pool_prompt(xc, pw, gamma, seq, tm_p)
        qd, kd, vd, gd, betad = _delta_prep(qkv, bd, ad, cw, a_log, dt_b, seq, tm_p)
        r3 = lambda x: x.reshape(bp, seq, W_Q)
        od, s_fin = _delta(r3(qd), r3(kd), r3(vd), r3(gd), r3(betad), r3(zg),
                           jnp.zeros((bp, H_D * DK_D, DK_D), F32), gn, n_sub)
        outs["dp"].append(s_fin.reshape(bp, H_D, DK_D, DK_D))
        h_p = _mm_ln([oa, ob, oc, od.reshape(n_p, W_Q)], w_out_parts, h_p, ln1_g[l], ln1_b[l], tm_p)
        h_p = _xattn_prompt(h_p, wq, wo, mk_b.reshape(bp, n_mem, d), mv_b.reshape(bp, n_mem, d),
                            ln2_g[l], ln2_b[l], seq, tm_p)
        comb = _router(h_p, w_router_t, router_bias, tm_p).T
        h_p = _moe(h_p, comb, wg, wu, wd, ln3_g[l], ln3_b[l], tm_moe_p, l)

        (q, k, v, _, _, ub, vb, xc, qkv, zg, bd, ad) = _proj_in(h_s, w_in_l, tm_s)
        outs["ks"].append(k.reshape(bs, t_new, H_A, DH_A))
        outs["vs"].append(v.reshape(bs, t_new, H_A, DH_A))
        oa = _sb_sample(rows8(q), rows8(k), rows8(v), cache_kt, cache_vt, page_table, sb_bias[l],
                        l)[:, :t_new].reshape(n_s, W_Q)
        wl = jnp.repeat(jnp.where(jnp.tril(jnp.ones((t_new, t_new), bool)),
                                  ws_l[:, :t_new, :t_new], 0.0).transpose(1, 2, 0),
                        GROUP_LANES, axis=2)
        (ob, vn, oc, pool_new, conv_new, qd, kd, vd, gd, betad) = _sample_mix(
            past_len, tmajor(ub), tmajor(vb), tmajor(xc), tmajor(qkv), tmajor(bd), tmajor(ad),
            bmajor(state_pool[l]), bmajor(state_conv[l]), wl, bs_lanes[:t_new], lng, lnb, pw,
            gamma, cw, a_log, dt_b)
        outs["gv"].append(bmajor(vn))
        outs["ps"].append(bmajor(pool_new))
        outs["cs"].append(bmajor(conv_new))
        od, s_fin = _delta_lanes(qd, kd, vd, gd, betad, tmajor(zg), gn,
                                 state_delta[l].transpose(1, 2, 3, 0))
        outs["ds"].append(s_fin.transpose(3, 0, 1, 2))
        flat = lambda x: bmajor(x).reshape(n_s, W_Q)
        h_s = _mm_ln([oa, flat(ob), flat(oc), flat(od)], w_out_parts, h_s, ln1_g[l], ln1_b[l],
                     tm_s)
        _, qx = _mm(h_s, wq, tm_s, scale=xa_dh ** -0.5)
        ox = _xattn_sample(rows8(qx), cache_mem_k, cache_mem_v, l)[:, :t_new].reshape(n_s, d)
        h_s = _mm_ln([ox], [wo], h_s, ln2_g[l], ln2_b[l], tm_s)
        comb = _router(h_s, w_router_t, router_bias, tm_s).T
        h_s = _moe(h_s, comb, wg, wu, wd, ln3_g[l], ln3_b[l], tm_s, l)

    st = lambda key: jnp.stack(outs[key])
    return (h_p.reshape(bp, seq, d), h_s.reshape(bs, t_new, d),
            st("kp"), st("vp"), st("ks"), st("vs"), st("gv"),
            st("pp"), st("ps"), st("cp"), st("cs"), st("dp"), st("ds"),
            st("mk"), st("mv"))
```
